```python
import math
import jax, jax.numpy as jnp
from jax import lax
import numpy as np

D_MODEL = 1024
BATCH = 8
SEQ = 8192
DEPTH = 1

CHUNK = 64
Q_BLOCK = 128
HEAD_DIM = 64
N_DIFF_HEADS = D_MODEL // (4 * HEAD_DIM)
ATTN_W = N_DIFF_HEADS * 2 * HEAD_DIM
ROPE_THETA = 10000.0
LRU_W = D_MODEL // 2
LRU_BLOCK = 64
N_LRU_BLOCKS = LRU_W // LRU_BLOCK
CONV_W = 4
LRU_C = 8.0
MIX_W = ATTN_W + LRU_W
IN_W = 3 * ATTN_W + 2 * LRU_W
N_EXPERTS = 32
TOP_K = 4
D_FF = D_MODEL
SWIGLU_ALPHA = 1.702
SWIGLU_LIMIT = 7.0
EXPERT_BLOCK = 256
NORM_EPS = 1e-6
SUBLN_EPS = 1e-5

kernel_name = "chunk_causal_hymba_diffattn_rglru_moe"


def rmsnorm(x, g, eps=NORM_EPS):
    xf = x.astype(jnp.float32)
    y = xf * lax.rsqrt(jnp.mean(xf * xf, axis=-1, keepdims=True) + eps)
    return (y * g.astype(jnp.float32)).astype(x.dtype)


def rope(x, pos):
    half = HEAD_DIM // 2
    inv = ROPE_THETA ** (-jnp.arange(half, dtype=jnp.float32) / half)
    ang = pos.astype(jnp.float32)[:, None] * inv[None, :]
    cos, sin = jnp.cos(ang), jnp.sin(ang)
    xf = x.astype(jnp.float32)
    x1, x2 = xf[..., :half], xf[..., half:]
    return jnp.concatenate([x1 * cos - x2 * sin, x2 * cos + x1 * sin], axis=-1).astype(x.dtype)


def lambda_init(layer_idx):
    return 0.8 - 0.6 * math.exp(-0.3 * layer_idx)


def diff_attention(q, k, v, lam):
    B, H2, S, d = q.shape
    H = H2 // 2
    scale = d ** -0.5
    outs = []
    for blk in range(S // Q_BLOCK):
        s0 = blk * Q_BLOCK
        s1 = s0 + Q_BLOCK
        qb, kb, vb = q[:, :, s0:s1], k[:, :, :s1], v[:, :, :s1]
        sc = jnp.einsum('bhqd,bhkd->bhqk', qb, kb, preferred_element_type=jnp.float32) * scale
        q_chunk = (s0 + jnp.arange(Q_BLOCK)) // CHUNK
        k_chunk = jnp.arange(s1) // CHUNK
        mask = k_chunk[None, :] <= q_chunk[:, None]
        p = jax.nn.softmax(jnp.where(mask, sc, -jnp.inf), axis=-1)
        p = p.reshape(B, H, 2, Q_BLOCK, s1)
        w = p[:, :, 0] - lam * p[:, :, 1]
        outs.append(jnp.einsum('bhqk,bhkd->bhqd', w.astype(v.dtype), vb))
    return jnp.concatenate(outs, axis=2)


def causal_conv(x, w, b):
    S = x.shape[1]
    xp = jnp.pad(x, ((0, 0), (CONV_W - 1, 0), (0, 0)))
    return sum(xp[:, j:j + S] * w[j] for j in range(CONV_W)) + b


def rg_lru(xr, w_a, b_a, w_x, b_x, lam):
    B, S, W = xr.shape
    xb = xr.reshape(B, S, N_LRU_BLOCKS, LRU_BLOCK)
    r = jax.nn.sigmoid(jnp.einsum('bsnc,ncd->bsnd', xb, w_a).reshape(B, S, W) + b_a)
    i = jax.nn.sigmoid(jnp.einsum('bsnc,ncd->bsnd', xb, w_x).reshape(B, S, W) + b_x)
    log_a = -LRU_C * r.astype(jnp.float32) * jax.nn.softplus(-lam.astype(jnp.float32))
    a = jnp.exp(log_a)
    u = jnp.sqrt(-jnp.expm1(2.0 * log_a)) * (i * xr).astype(jnp.float32)

    def combine(left, right):
        a_l, b_l = left
        a_r, b_r = right
        return a_l * a_r, a_r * b_l + b_r

    _, h = lax.associative_scan(combine, (a, u), axis=1)
    return h.astype(xr.dtype)


def moe(h, w_router, b_router, w1, b1, w2, b2):
    B, S, D = h.shape
    T = B * S
    A = T * TOP_K
    xf = h.reshape(T, D)
    logits = (xf @ w_router + b_router).astype(jnp.float32)
    top_val, top_idx = lax.top_k(logits, TOP_K)
    gates = jax.nn.softmax(top_val, axis=-1)
    flat_e = top_idx.reshape(A)
    flat_tok = jnp.arange(A, dtype=jnp.int32) // TOP_K
    order = jnp.argsort(flat_e)
    se, stok, sg = flat_e[order], flat_tok[order], gates.reshape(A)[order]
    counts = jnp.bincount(flat_e, length=N_EXPERTS)
    padded = (counts + EXPERT_BLOCK - 1) // EXPERT_BLOCK * EXPERT_BLOCK
    pend = jnp.cumsum(padded)
    pstart = pend - padded
    ustart = jnp.cumsum(counts) - counts
    dest = pstart[se] + (jnp.arange(A, dtype=jnp.int32) - ustart[se])
    n_blocks = -(-A // EXPERT_BLOCK) + N_EXPERTS
    R = n_blocks * EXPERT_BLOCK
    row_tok = jnp.zeros((R,), jnp.int32).at[dest].set(stok)
    row_g = jnp.zeros((R,), jnp.float32).at[dest].set(sg)
    block_e = jnp.minimum(jnp.searchsorted(pend, jnp.arange(n_blocks) * EXPERT_BLOCK, side='right'),
                          N_EXPERTS - 1)

    def expert_block(args):
        tok, e = args
        hu = xf[tok] @ w1[e] + b1[e]
        glu = jnp.minimum(hu[:, :D_FF], SWIGLU_LIMIT)
        lin = jnp.clip(hu[:, D_FF:], -SWIGLU_LIMIT, SWIGLU_LIMIT)
        act = glu * jax.nn.sigmoid(SWIGLU_ALPHA * glu) * (lin + 1.0)
        return act @ w2[e] + b2[e]

    y = lax.map(expert_block, (row_tok.reshape(n_blocks, EXPERT_BLOCK), block_e))
    y = y.reshape(R, D) * row_g[:, None].astype(y.dtype)
    out = jnp.zeros((T, D), h.dtype).at[row_tok].add(y)
    return out.reshape(B, S, D)


def setup_inputs(seed: int = 0) -> dict:
    key = jax.random.key(seed)
    ks = jax.random.split(key, 32)
    f32 = jnp.float32
    nrm = lambda k, shape, s: jax.random.normal(k, shape, f32) * s
    gain = lambda k, shape: 1.0 + 0.02 * jax.random.normal(k, shape, f32)
    L = DEPTH
    a0 = jax.random.uniform(ks[19], (L, LRU_W), f32, 0.9, 0.999)
    a_base = a0 ** (1.0 / LRU_C)
    lru_lambda = jnp.log(a_base) - jnp.log1p(-a_base)
    return {
        "x": nrm(ks[0], (BATCH, SEQ, D_MODEL), 1.0),
        "c": nrm(ks[1], (BATCH, D_MODEL), 1.0),
        "w_ada": nrm(ks[2], (L, D_MODEL, 6 * D_MODEL), 0.5 * D_MODEL ** -0.5),
        "b_ada": nrm(ks[3], (L, 6 * D_MODEL), 0.02),
        "norm1_g": gain(ks[4], (L, D_MODEL)),
        "w_in": nrm(ks[5], (L, D_MODEL, IN_W), D_MODEL ** -0.5),
        "q_norm_g": gain(ks[6], (L, HEAD_DIM)),
        "k_norm_g": gain(ks[7], (L, HEAD_DIM)),
        "lambda_q1": nrm(ks[8], (L, HEAD_DIM), 0.1),
        "lambda_k1": nrm(ks[9], (L, HEAD_DIM), 0.1),
        "lambda_q2": nrm(ks[10], (L, HEAD_DIM), 0.1),
        "lambda_k2": nrm(ks[11], (L, HEAD_DIM), 0.1),
        "subln_g": gain(ks[12], (L, 2 * HEAD_DIM)),
        "conv_w": nrm(ks[13], (L, CONV_W, LRU_W), 0.5),
        "conv_b": nrm(ks[14], (L, LRU_W), 0.01),
        "w_rg_a": nrm(ks[15], (L, N_LRU_BLOCKS, LRU_BLOCK, LRU_BLOCK), LRU_BLOCK ** -0.5),
        "b_rg_a": nrm(ks[16], (L, LRU_W), 0.01),
        "w_rg_x": nrm(ks[17], (L, N_LRU_BLOCKS, LRU_BLOCK, LRU_BLOCK), LRU_BLOCK ** -0.5),
        "b_rg_x": nrm(ks[18], (L, LRU_W), 0.01),
        "lru_lambda": lru_lambda,
        "w_out": nrm(ks[20], (L, MIX_W, D_MODEL), MIX_W ** -0.5),
        "norm2_g": gain(ks[21], (L, D_MODEL)),
        "w_router": nrm(ks[22], (L, D_MODEL, N_EXPERTS), D_MODEL ** -0.5),
        "b_router": nrm(ks[23], (L, N_EXPERTS), 0.01),
        "w_moe1": nrm(ks[24], (L, N_EXPERTS, D_MODEL, 2 * D_FF), D_MODEL ** -0.5),
        "b_moe1": nrm(ks[25], (L, N_EXPERTS, 2 * D_FF), 0.01),
        "w_moe2": nrm(ks[26], (L, N_EXPERTS, D_FF, D_MODEL), D_FF ** -0.5),
        "b_moe2": nrm(ks[27], (L, N_EXPERTS, D_MODEL), 0.01),
    }


def reference(x, c, w_ada, b_ada, norm1_g, w_in, q_norm_g, k_norm_g, lambda_q1, lambda_k1,
              lambda_q2, lambda_k2, subln_g, conv_w, conv_b, w_rg_a, b_rg_a, w_rg_x, b_rg_x,
              lru_lambda, w_out, norm2_g, w_router, b_router, w_moe1, b_moe1, w_moe2, b_moe2):
    B, S, D = x.shape
    pos = jnp.arange(S, dtype=jnp.int32)
    for l in range(DEPTH):
        mod = jax.nn.silu(c) @ w_ada[l] + b_ada[l]
        sh1, sc1, g1, sh2, sc2, g2 = [m[:, None, :] for m in jnp.split(mod, 6, axis=-1)]

        hn = rmsnorm(x, norm1_g[l]) * (1.0 + sc1) + sh1
        proj = hn @ w_in[l]
        q, k, v, xr, gr = jnp.split(
            proj, [ATTN_W, 2 * ATTN_W, 3 * ATTN_W, 3 * ATTN_W + LRU_W], axis=-1)

        q = rmsnorm(q.reshape(B, S, 2 * N_DIFF_HEADS, HEAD_DIM), q_norm_g[l])
        k = rmsnorm(k.reshape(B, S, 2 * N_DIFF_HEADS, HEAD_DIM), k_norm_g[l])
        q = rope(q.transpose(0, 2, 1, 3), pos)
        k = rope(k.transpose(0, 2, 1, 3), pos)
        v = v.reshape(B, S, N_DIFF_HEADS, 2 * HEAD_DIM).transpose(0, 2, 1, 3)
        lam_init = lambda_init(l)
        lam = (jnp.exp(jnp.sum(lambda_q1[l].astype(jnp.float32) * lambda_k1[l].astype(jnp.float32)))
               - jnp.exp(jnp.sum(lambda_q2[l].astype(jnp.float32) * lambda_k2[l].astype(jnp.float32)))
               + lam_init)
        att = diff_attention(q, k, v, lam)
        att = rmsnorm(att, subln_g[l], SUBLN_EPS) * (1.0 - lam_init)
        att = att.transpose(0, 2, 1, 3).reshape(B, S, ATTN_W)

        xc = causal_conv(xr, conv_w[l], conv_b[l])
        hr = rg_lru(xc, w_rg_a[l], b_rg_a[l], w_rg_x[l], b_rg_x[l], lru_lambda[l])
        rec = hr * jax.nn.gelu(gr, approximate=True)

        mix = jnp.concatenate([att, rec], axis=-1) @ w_out[l]
        x = x + g1 * mix

        hm = rmsnorm(x, norm2_g[l]) * (1.0 + sc2) + sh2
        x = x + g2 * moe(hm, w_router[l], b_router[l], w_moe1[l], b_moe1[l], w_moe2[l], b_moe2[l])
    return x
```

```python
import functools
import math

import jax
import jax.numpy as jnp
from jax import lax
from jax.experimental import pallas as pl
from jax.experimental.pallas import tpu as pltpu

F32 = jnp.float32
BF16 = jnp.bfloat16

CHUNK = 64
HEAD_DIM = 64
ROPE_THETA = 10000.0
LRU_BLOCK = 64
CONV_W = 4
LRU_C = 8.0
TOP_K = 4
SWIGLU_ALPHA = 1.702
SWIGLU_LIMIT = 7.0
NORM_EPS = 1e-6
SUBLN_EPS = 1e-5

LANES = 128
SUBLANES = 8
VMEM_LIMIT = 56 * 1024 * 1024

TM_PROJ = 512
TQ = 256
TK = 256
TT = 256
EB = 256
TM_COMB = 256


def _split_bf16(a):
    hi = a.astype(BF16)
    lo = (a - hi.astype(F32)).astype(BF16)
    return hi, lo


def _dot(a, b):
    return jnp.dot(a, b, preferred_element_type=F32)


def _dot3(a, b):
    ah, al = _split_bf16(a)
    bh, bl = _split_bf16(b)
    return _dot(ah, bh) + (_dot(ah, bl) + _dot(al, bh))


def _ada_kernel(c_ref, w_ref, b_ref, o_ref):
    c = c_ref[...]
    s = c * jax.nn.sigmoid(c)
    o_ref[...] = _dot3(s, w_ref[...]) + b_ref[...]


def _ada(c, w_ada, b_ada):
    B, D = c.shape
    N = w_ada.shape[1]
    tn = D
    return pl.pallas_call(
        _ada_kernel,
        grid=(N // tn,),
        in_specs=[pl.BlockSpec((B, D), lambda j: (0, 0)),
                  pl.BlockSpec((D, tn), lambda j: (0, j)),
                  pl.BlockSpec((1, tn), lambda j: (0, j))],
        out_specs=pl.BlockSpec((B, tn), lambda j: (0, j)),
        out_shape=jax.ShapeDtypeStruct((B, N), F32),
        compiler_params=pltpu.CompilerParams(dimension_semantics=("arbitrary",),
                                             vmem_limit_bytes=VMEM_LIMIT),
        name="ada",
    )(c, w_ada, b_ada.reshape(1, N))


def _inproj_kernel(x_ref, g_ref, sc_ref, sh_ref, w_ref, gsum_ref, qg_ref, kg_ref, cos_ref, sin_ref,
                   q_ref, kt_ref, v_ref, xr_ref, gr_ref, *, attn_w, lru_w):
    x = x_ref[0]
    ms = jnp.mean(x * x, axis=-1, keepdims=True)
    hn = x * lax.rsqrt(ms + NORM_EPS) * g_ref[...]
    hn = (hn * (1.0 + sc_ref[0]) + sh_ref[0]).astype(BF16)

    cos = cos_ref[...]
    sin = sin_ref[...]
    gsum = gsum_ref[...]
    lane = lax.broadcasted_iota(jnp.int32, cos.shape, 1)
    first_half = (lane % HEAD_DIM) < (HEAD_DIM // 2)

    def head_norm_rope(t, gain):
        hi, lo = _split_bf16(t * t)
        ssum = _dot(hi, gsum) + _dot(lo, gsum)
        t = t * lax.rsqrt(ssum * (1.0 / HEAD_DIM) + NORM_EPS) * gain
        half = HEAD_DIM // 2
        partner = jnp.where(first_half, pltpu.roll(t, attn_w - half, 1), pltpu.roll(t, half, 1))
        return t * cos + partner * sin

    q = head_norm_rope(_dot(hn, w_ref[:, 0:attn_w]), qg_ref[...])
    q_ref[0] = (q * (HEAD_DIM ** -0.5)).astype(BF16)

    k = head_norm_rope(_dot(hn, w_ref[:, attn_w:2 * attn_w]), kg_ref[...])
    for h in range(attn_w // LANES):
        kt_ref[0, h] = k[:, h * LANES:(h + 1) * LANES].T.astype(BF16)

    v_ref[0] = _dot(hn, w_ref[:, 2 * attn_w:3 * attn_w]).astype(BF16)
    xr_ref[0] = _dot(hn, w_ref[:, 3 * attn_w:3 * attn_w + lru_w])
    gr_ref[0] = _dot(hn, w_ref[:, 3 * attn_w + lru_w:3 * attn_w + 2 * lru_w])


def _inproj(x, g1n, sc1, sh1, w_in_b, gsum, qg, kg, cos_t, sin_t, attn_w, lru_w):
    B, S, D = x.shape
    tm = min(TM_PROJ, S)
    in_w = w_in_b.shape[1]
    nh = attn_w // LANES
    kern = functools.partial(_inproj_kernel, attn_w=attn_w, lru_w=lru_w)
    return pl.pallas_call(
        kern,
        grid=(S // tm, B),
        in_specs=[pl.BlockSpec((1, tm, D), lambda i, b: (b, i, 0)),
                  pl.BlockSpec((1, D), lambda i, b: (0, 0)),
                  pl.BlockSpec((1, 1, D), lambda i, b: (b, 0, 0)),
                  pl.BlockSpec((1, 1, D), lambda i, b: (b, 0, 0)),
                  pl.BlockSpec((D, in_w), lambda i, b: (0, 0)),
                  pl.BlockSpec((attn_w, attn_w), lambda i, b: (0, 0)),
                  pl.BlockSpec((1, attn_w), lambda i, b: (0, 0)),
                  pl.BlockSpec((1, attn_w), lambda i, b: (0, 0)),
                  pl.BlockSpec((tm, attn_w), lambda i, b: (i, 0)),
                  pl.BlockSpec((tm, attn_w), lambda i, b: (i, 0))],
        out_specs=[pl.BlockSpec((1, tm, attn_w), lambda i, b: (b, i, 0)),
                   pl.BlockSpec((1, nh, LANES, tm), lambda i, b: (b, 0, 0, i)),
                   pl.BlockSpec((1, tm, attn_w), lambda i, b: (b, i, 0)),
                   pl.BlockSpec((1, tm, lru_w), lambda i, b: (b, i, 0)),
                   pl.BlockSpec((1, tm, lru_w), lambda i, b: (b, i, 0))],
        out_shape=[jax.ShapeDtypeStruct((B, S, attn_w), BF16),
                   jax.ShapeDtypeStruct((B, nh, LANES, S), BF16),
                   jax.ShapeDtypeStruct((B, S, attn_w), BF16),
                   jax.ShapeDtypeStruct((B, S, lru_w), F32),
                   jax.ShapeDtypeStruct((B, S, lru_w), F32)],
        compiler_params=pltpu.CompilerParams(dimension_semantics=("arbitrary", "arbitrary"),
                                             vmem_limit_bytes=VMEM_LIMIT),
        name="inproj",
    )(x, g1n, sc1, sh1, w_in_b, gsum, qg, kg, cos_t, sin_t)


def _attn_kernel(q_ref, kt_ref, v_ref, lq1_ref, lk1_ref, lq2_ref, lk2_ref, sg_ref, o_ref,
                 *, tq, tk, lam_init):
    qi = pl.program_id(2)
    q = q_ref[0]
    lane = lax.broadcasted_iota(jnp.int32, q.shape, 1)
    zero = jnp.zeros_like(q)
    qs = jnp.concatenate([jnp.where(lane < HEAD_DIM, q, zero),
                          jnp.where(lane >= HEAD_DIM, q, zero)], axis=0)

    def step(j, carry, masked):
        m, l, acc = carry
        start = pl.multiple_of(j * tk, tk)
        kt = kt_ref[0, 0, :, pl.ds(start, tk)]
        vv = v_ref[0, pl.ds(start, tk), :]
        s = _dot(qs, kt)
        if masked:
            row = lax.broadcasted_iota(jnp.int32, s.shape, 0) % tq + qi * tq
            col = lax.broadcasted_iota(jnp.int32, s.shape, 1) + j * tk
            s = jnp.where((col // CHUNK) <= (row // CHUNK), s, -jnp.inf)
        m_new = jnp.maximum(m, jnp.max(s, axis=-1, keepdims=True))
        alpha = jnp.exp(m - m_new)
        p = jnp.exp(s - m_new)
        l = alpha * l + jnp.sum(p, axis=-1, keepdims=True)
        acc = alpha * acc + _dot(p.astype(BF16), vv)
        return m_new, l, acc

    carry = (jnp.full((2 * tq, 1), -jnp.inf, F32), jnp.zeros((2 * tq, 1), F32),
             jnp.zeros((2 * tq, LANES), F32))
    n_full = (qi * tq) // tk
    carry = lax.fori_loop(0, n_full, lambda j, c: step(j, c, False), carry)
    for d in range(tq // tk):
        carry = step(n_full + d, carry, True)
    m, l, acc = carry

    lam = (jnp.exp(jnp.sum(lq1_ref[...] * lk1_ref[...], axis=-1, keepdims=True))
           - jnp.exp(jnp.sum(lq2_ref[...] * lk2_ref[...], axis=-1, keepdims=True)) + lam_init)
    o = acc / l
    o = o[:tq] - lam * o[tq:]
    ms = jnp.mean(o * o, axis=-1, keepdims=True)
    o = o * lax.rsqrt(ms + SUBLN_EPS) * sg_ref[...] * (1.0 - lam_init)
    o_ref[0] = o.astype(BF16)


def _attn(q, kt, v, lq1, lk1, lq2, lk2, sg, lam_init):
    B, S, attn_w = q.shape
    nh = attn_w // LANES
    tq = min(TQ, S)
    tk = min(TK, tq)
    kern = functools.partial(_attn_kernel, tq=tq, tk=tk, lam_init=lam_init)
    vec = lambda n: pl.BlockSpec((1, n), lambda b, h, i: (0, 0))
    return pl.pallas_call(
        kern,
        grid=(B, nh, S // tq),
        in_specs=[pl.BlockSpec((1, tq, LANES), lambda b, h, i: (b, i, h)),
                  pl.BlockSpec((1, 1, LANES, S), lambda b, h, i: (b, h, 0, 0)),
                  pl.BlockSpec((1, S, LANES), lambda b, h, i: (b, 0, h)),
                  vec(HEAD_DIM), vec(HEAD_DIM), vec(HEAD_DIM), vec(HEAD_DIM), vec(LANES)],
        out_specs=pl.BlockSpec((1, tq, LANES), lambda b, h, i: (b, i, h)),
        out_shape=jax.ShapeDtypeStruct((B, S, attn_w), BF16),
        compiler_params=pltpu.CompilerParams(
            dimension_semantics=("arbitrary", "arbitrary", "arbitrary"), vmem_limit_bytes=VMEM_LIMIT),
        name="attn",
    )(q, kt, v, lq1, lk1, lq2, lk2, sg)


def _shift_rows(a, d, fill):
    row = lax.broadcasted_iota(jnp.int32, a.shape, 0)
    return jnp.where(row < d, fill, pltpu.roll(a, d, 0))


def _lru_kernel(xr_ref, gr_ref, cw_ref, cb_ref, wa_ref, ba_ref, wx_ref, bx_ref, lam_ref, o_ref,
                tail_ref, h_ref, *, tt):
    @pl.when(pl.program_id(1) == 0)
    def _():
        tail_ref[...] = jnp.zeros_like(tail_ref)
        h_ref[...] = jnp.zeros_like(h_ref)

    x = xr_ref[0]
    ext = jnp.concatenate([tail_ref[...], x], axis=0)
    xc = cb_ref[...] + cw_ref[CONV_W - 1:CONV_W, :] * x
    for j in range(CONV_W - 1):
        back = CONV_W - 1 - j
        xc = xc + cw_ref[j:j + 1, :] * ext[SUBLANES - back:SUBLANES - back + tt]
    tail_ref[...] = x[tt - SUBLANES:]

    xcb = xc.astype(BF16)
    r = jax.nn.sigmoid(_dot(xcb, wa_ref[...]) + ba_ref[...])
    i = jax.nn.sigmoid(_dot(xcb, wx_ref[...]) + bx_ref[...])
    nl = -lam_ref[...]
    softplus = jnp.maximum(nl, 0.0) + jnp.log(1.0 + jnp.exp(-jnp.abs(nl)))
    log_a = (-LRU_C) * r * softplus
    a = jnp.exp(log_a)
    th = jnp.tanh(log_a)
    u = jnp.sqrt((-2.0) * th / (1.0 - th)) * (i * xc)

    d = 1
    while d < tt:
        u = u + a * _shift_rows(u, d, 0.0)
        a = a * _shift_rows(a, d, 1.0)
        d *= 2
    h = a * h_ref[0:1, :] + u
    h_ref[...] = jnp.broadcast_to(h[tt - 1:tt, :], h_ref.shape)

    g = gr_ref[0]
    gelu = 0.5 * g * (1.0 + jnp.tanh(math.sqrt(2.0 / math.pi) * (g + 0.044715 * (g * g * g))))
    o_ref[0] = (h * gelu).astype(BF16)


def _lru(xr, gr, conv_w, conv_b, wa_bd, b_a, wx_bd, b_x, lam):
    B, S, W = xr.shape
    tt = min(TT, S)
    kern = functools.partial(_lru_kernel, tt=tt)
    vec = pl.BlockSpec((1, W), lambda b, i: (0, 0))
    mat = pl.BlockSpec((W, W), lambda b, i: (0, 0))
    return pl.pallas_call(
        kern,
        grid=(B, S // tt),
        in_specs=[pl.BlockSpec((1, tt, W), lambda b, i: (b, i, 0)),
                  pl.BlockSpec((1, tt, W), lambda b, i: (b, i, 0)),
                  pl.BlockSpec((CONV_W, W), lambda b, i: (0, 0)),
                  vec, mat, vec, mat, vec, vec],
        out_specs=pl.BlockSpec((1, tt, W), lambda b, i: (b, i, 0)),
        out_shape=jax.ShapeDtypeStruct((B, S, W), BF16),
        scratch_shapes=[pltpu.VMEM((SUBLANES, W), F32), pltpu.VMEM((SUBLANES, W), F32)],
        compiler_params=pltpu.CompilerParams(dimension_semantics=("arbitrary", "arbitrary"),
                                             vmem_limit_bytes=VMEM_LIMIT),
        name="lru",
    )(xr, gr, conv_w, conv_b, wa_bd, b_a, wx_bd, b_x, lam)


def _outproj_kernel(att_ref, rec_ref, x_ref, wo_ref, g1_ref, n2_ref, sc_ref, sh_ref, wr_ref, br_ref,
                    x1_ref, hm_ref, idx_ref, gate_ref, *, attn_w, n_exp):
    tm = x_ref.shape[1]
    mix = _dot(att_ref[0], wo_ref[0:attn_w, :]) + _dot(rec_ref[0], wo_ref[attn_w:, :])
    x1 = x_ref[0] + g1_ref[0] * mix
    x1_ref[0] = x1
    ms = jnp.mean(x1 * x1, axis=-1, keepdims=True)
    hm = x1 * lax.rsqrt(ms + NORM_EPS) * n2_ref[...]
    hm = hm * (1.0 + sc_ref[0]) + sh_ref[0]
    for c in range(hm.shape[1] // LANES):
        hm_ref[pl.ds(c, tm, stride=SUBLANES), :] = hm[:, c * LANES:(c + 1) * LANES]

    hh, hl = _split_bf16(hm)
    wh, wl = _split_bf16(wr_ref[...])
    nt = (((1,), (1,)), ((), ()))
    logits = (lax.dot_general(wh, hh, nt, preferred_element_type=F32)
              + (lax.dot_general(wh, hl, nt, preferred_element_type=F32)
                 + lax.dot_general(wl, hh, nt, preferred_element_type=F32))) + br_ref[...]
    eidx = lax.broadcasted_iota(jnp.int32, logits.shape, 0)
    vals, idxs = [], []
    for _ in range(TOP_K):
        m = jnp.max(logits, axis=0, keepdims=True)
        am = jnp.min(jnp.where(logits == m, eidx, n_exp), axis=0, keepdims=True)
        vals.append(m)
        idxs.append(am)
        logits = jnp.where(eidx == am, -jnp.inf, logits)
    ex = [jnp.exp(v - vals[0]) for v in vals]
    den = ex[0] + ex[1] + ex[2] + ex[3]
    idx_ref[0] = jnp.concatenate(idxs, axis=0)
    gate_ref[0] = jnp.concatenate([e / den for e in ex], axis=0)


def _outproj(att, rec, x, wo_b, g1, n2g, sc2, sh2, wr_t, br, n_exp):
    B, S, D = x.shape
    attn_w = att.shape[2]
    lru_w = rec.shape[2]
    tm = min(TM_PROJ, S)
    ns = S // tm
    rows = D // LANES
    kern = functools.partial(_outproj_kernel, attn_w=attn_w, n_exp=n_exp)
    bvec = pl.BlockSpec((1, 1, D), lambda b, i: (b, 0, 0))
    return pl.pallas_call(
        kern,
        grid=(B, ns),
        in_specs=[pl.BlockSpec((1, tm, attn_w), lambda b, i: (b, i, 0)),
                  pl.BlockSpec((1, tm, lru_w), lambda b, i: (b, i, 0)),
                  pl.BlockSpec((1, tm, D), lambda b, i: (b, i, 0)),
                  pl.BlockSpec((attn_w + lru_w, D), lambda b, i: (0, 0)),
                  bvec,
                  pl.BlockSpec((1, D), lambda b, i: (0, 0)),
                  bvec, bvec,
                  pl.BlockSpec((n_exp, D), lambda b, i: (0, 0)),
                  pl.BlockSpec((n_exp, 1), lambda b, i: (0, 0))],
        out_specs=[pl.BlockSpec((1, tm, D), lambda b, i: (b, i, 0)),
                   pl.BlockSpec((tm * rows, LANES), lambda b, i: (b * ns + i, 0)),
                   pl.BlockSpec((1, TOP_K, tm), lambda b, i: (b, 0, i)),
                   pl.BlockSpec((1, TOP_K, tm), lambda b, i: (b, 0, i))],
        out_shape=[jax.ShapeDtypeStruct((B, S, D), F32),
                   jax.ShapeDtypeStruct((B * S * rows, LANES), F32),
                   jax.ShapeDtypeStruct((B, TOP_K, S), jnp.int32),
                   jax.ShapeDtypeStruct((B, TOP_K, S), F32)],
        compiler_params=pltpu.CompilerParams(dimension_semantics=("arbitrary", "arbitrary"),
                                             vmem_limit_bytes=VMEM_LIMIT),
        name="outproj",
    )(att, rec, x, wo_b, g1, n2g, sc2, sh2, wr_t, br)


def _moe_kernel(be_ref, nv_ref, nused_ref, rowa_hbm, g_ref, hm_hbm, w1_ref, b1_ref, w2_ref, b2_ref,
                y_hbm, idx_smem, xbuf, ybuf, sem_i, sem_g, sem_s, *, n_tok, d_ff):
    i = pl.program_id(0)
    rows = SUBLANES

    @pl.when(i == 0)
    def _():
        xbuf[...] = jnp.zeros_like(xbuf)

    @pl.when(i < nused_ref[0])
    def _():
        nv = nv_ref[i]
        cp = pltpu.make_async_copy(rowa_hbm.at[pl.ds(pl.multiple_of(i * EB, EB), EB)], idx_smem, sem_i)
        cp.start()
        cp.wait()

        def gather_copy(r):
            tok = idx_smem[r] // TOP_K
            return pltpu.make_async_copy(hm_hbm.at[pl.ds(pl.multiple_of(tok * rows, rows), rows), :],
                                         xbuf.at[pl.ds(pl.multiple_of(r * rows, rows), rows), :], sem_g)

        def scatter_copy(r):
            a = idx_smem[r]
            slot = (a % TOP_K) * n_tok + a // TOP_K
            return pltpu.make_async_copy(ybuf.at[pl.ds(pl.multiple_of(r * rows, rows), rows), :],
                                         y_hbm.at[pl.ds(pl.multiple_of(slot * rows, rows), rows), :], sem_s)

        def _start_g(r, c):
            gather_copy(r).start()
            return c

        def _wait_g(r, c):
            gather_copy(r).wait()
            return c

        lax.fori_loop(0, nv, _start_g, 0)
        lax.fori_loop(0, nv, _wait_g, 0)

        x = jnp.concatenate([xbuf[pl.ds(c, EB, stride=rows), :] for c in range(rows)], axis=1)
        hu = _dot(x.astype(BF16), w1_ref[0]) + b1_ref[0]
        glu = jnp.minimum(hu[:, :d_ff], SWIGLU_LIMIT)
        lin = jnp.clip(hu[:, d_ff:], -SWIGLU_LIMIT, SWIGLU_LIMIT)
        act = glu * jax.nn.sigmoid(SWIGLU_ALPHA * glu) * (lin + 1.0)
        y = (_dot(act.astype(BF16), w2_ref[0]) + b2_ref[0]) * g_ref[...]
        for c in range(rows):
            ybuf[pl.ds(c, EB, stride=rows), :] = y[:, c * LANES:(c + 1) * LANES]

        def _start_s(r, c):
            scatter_copy(r).start()
            return c

        def _wait_s(r, c):
            scatter_copy(r).wait()
            return c

        lax.fori_loop(0, nv, _start_s, 0)
        lax.fori_loop(0, nv, _wait_s, 0)


def _moe(block_e, nv, n_used, row_a, row_g, hm2d, w1b, b1, w2b, b2, n_tok):
    n_blocks = block_e.shape[0]
    n_exp, D, ff2 = w1b.shape
    d_ff = ff2 // 2
    rows = D // LANES
    assert rows == SUBLANES, "a token row must fill exactly one (8,128) f32 tile"
    kern = functools.partial(_moe_kernel, n_tok=n_tok, d_ff=d_ff)
    grid_spec = pltpu.PrefetchScalarGridSpec(
        num_scalar_prefetch=3,
        grid=(n_blocks,),
        in_specs=[pl.BlockSpec(memory_space=pl.ANY),
                  pl.BlockSpec((EB, 1), lambda i, be, nv, nu: (i, 0)),
                  pl.BlockSpec(memory_space=pl.ANY),
                  pl.BlockSpec((1, D, ff2), lambda i, be, nv, nu: (be[i], 0, 0)),
                  pl.BlockSpec((1, 1, ff2), lambda i, be, nv, nu: (be[i], 0, 0)),
                  pl.BlockSpec((1, d_ff, D), lambda i, be, nv, nu: (be[i], 0, 0)),
                  pl.BlockSpec((1, 1, D), lambda i, be, nv, nu: (be[i], 0, 0))],
        out_specs=pl.BlockSpec(memory_space=pl.ANY),
        scratch_shapes=[pltpu.SMEM((EB,), jnp.int32),
                        pltpu.VMEM((EB * rows, LANES), F32),
                        pltpu.VMEM((EB * rows, LANES), F32),
                        pltpu.SemaphoreType.DMA(()),
                        pltpu.SemaphoreType.DMA(()),
                        pltpu.SemaphoreType.DMA(())],
    )
    return pl.pallas_call(
        kern,
        grid_spec=grid_spec,
        out_shape=jax.ShapeDtypeStruct((TOP_K * n_tok * rows, LANES), F32),
        compiler_params=pltpu.CompilerParams(dimension_semantics=("arbitrary",),
                                             vmem_limit_bytes=VMEM_LIMIT),
        name="moe",
    )(block_e, nv, n_used, row_a, row_g, hm2d, w1b, b1, w2b, b2)


def _combine_kernel(y_ref, x1_ref, g2_ref, o_ref, s_ref):
    tm = x1_ref.shape[1]
    s_ref[...] = (y_ref[0] + y_ref[1]) + (y_ref[2] + y_ref[3])
    moe = jnp.concatenate([s_ref[pl.ds(c, tm, stride=SUBLANES), :] for c in range(SUBLANES)], axis=1)
    o_ref[0] = x1_ref[0] + g2_ref[0] * moe


def _combine(y4, x1, g2):
    B, S, D = x1.shape
    rows = D // LANES
    tm = min(TM_COMB, S)
    ns = S // tm
    return pl.pallas_call(
        _combine_kernel,
        grid=(B, ns),
        in_specs=[pl.BlockSpec((TOP_K, tm * rows, LANES), lambda b, i: (0, b * ns + i, 0)),
                  pl.BlockSpec((1, tm, D), lambda b, i: (b, i, 0)),
                  pl.BlockSpec((1, 1, D), lambda b, i: (b, 0, 0))],
        out_specs=pl.BlockSpec((1, tm, D), lambda b, i: (b, i, 0)),
        out_shape=jax.ShapeDtypeStruct((B, S, D), F32),
        scratch_shapes=[pltpu.VMEM((tm * rows, LANES), F32)],
        compiler_params=pltpu.CompilerParams(dimension_semantics=("arbitrary", "arbitrary"),
                                             vmem_limit_bytes=VMEM_LIMIT),
        name="combine",
    )(y4.reshape(TOP_K, B * S * rows, LANES), x1, g2)


def _route(top_idx, gates, n_exp):
    B, K, S = top_idx.shape
    T = B * S
    A = T * K
    flat_e = top_idx.transpose(0, 2, 1).reshape(A)
    flat_g = gates.transpose(0, 2, 1).reshape(A)
    order = jnp.argsort(flat_e, stable=True).astype(jnp.int32)
    counts = jnp.sum(flat_e[:, None] == jnp.arange(n_exp, dtype=jnp.int32)[None, :], axis=0, dtype=jnp.int32)
    nblk = (counts + EB - 1) // EB
    bend = jnp.cumsum(nblk)
    bstart = bend - nblk
    ustart = jnp.cumsum(counts) - counts
    n_blocks = -(-A // EB) + n_exp
    blk = jnp.arange(n_blocks, dtype=jnp.int32)
    block_e = jnp.minimum(jnp.searchsorted(bend, blk, side='right'), n_exp - 1).astype(jnp.int32)
    off = (blk - bstart[block_e]) * EB
    nv = jnp.where(blk < bend[-1], jnp.clip(counts[block_e] - off, 0, EB), 0).astype(jnp.int32)
    r = jnp.arange(EB, dtype=jnp.int32)
    j = (ustart[block_e] + off)[:, None] + r[None, :]
    valid = r[None, :] < nv[:, None]
    row_a = jnp.where(valid, order[jnp.clip(j, 0, A - 1)], 0).astype(jnp.int32)
    row_g = jnp.where(valid, flat_g[row_a], 0.0).reshape(n_blocks * EB, 1)
    return block_e, nv, bend[-1:].astype(jnp.int32), row_a.reshape(n_blocks * EB), row_g


def _lambda_init(layer_idx):
    return 0.8 - 0.6 * math.exp(-0.3 * layer_idx)


def _block_diag(w):
    n, c, d = w.shape
    eye = jnp.eye(n, dtype=w.dtype)
    return (eye[:, None, :, None] * w[:, :, None, :]).reshape(n * c, n * d)


def kernel(x, c, w_ada, b_ada, norm1_g, w_in, q_norm_g, k_norm_g, lambda_q1, lambda_k1, lambda_q2, lambda_k2, subln_g, conv_w, conv_b, w_rg_a, b_rg_a, w_rg_x, b_rg_x, lru_lambda, w_out, norm2_g, w_router, b_router, w_moe1, b_moe1, w_moe2, b_moe2):
    B, S, D = x.shape
    depth = w_ada.shape[0]
    lru_w = conv_w.shape[2]
    attn_w = (w_in.shape[2] - 2 * lru_w) // 3
    n_exp = w_router.shape[2]
    n_heads = attn_w // HEAD_DIM
    assert attn_w % LANES == 0 and S % CHUNK == 0

    half = HEAD_DIM // 2
    inv = ROPE_THETA ** (-jnp.arange(half, dtype=F32) / half)
    ang = jnp.arange(S, dtype=jnp.int32).astype(F32)[:, None] * inv[None, :]
    cos_t = jnp.tile(jnp.concatenate([jnp.cos(ang), jnp.cos(ang)], axis=1), (1, n_heads))
    sin_t = jnp.tile(jnp.concatenate([-jnp.sin(ang), jnp.sin(ang)], axis=1), (1, n_heads))
    group = jnp.arange(attn_w, dtype=jnp.int32) // HEAD_DIM
    gsum = (group[:, None] == group[None, :]).astype(BF16)

    for l in range(depth):
        mod = _ada(c, w_ada[l], b_ada[l])
        sh1, sc1, g1, sh2, sc2, g2 = [m.reshape(B, 1, D) for m in jnp.split(mod, 6, axis=-1)]

        q, kt, v, xr, gr = _inproj(
            x, norm1_g[l].reshape(1, D), sc1, sh1, w_in[l].astype(BF16), gsum,
            jnp.tile(q_norm_g[l], n_heads).reshape(1, attn_w), jnp.tile(k_norm_g[l], n_heads).reshape(1, attn_w),
            cos_t, sin_t, attn_w, lru_w)

        att = _attn(q, kt, v, lambda_q1[l].reshape(1, -1), lambda_k1[l].reshape(1, -1),
                    lambda_q2[l].reshape(1, -1), lambda_k2[l].reshape(1, -1),
                    subln_g[l].reshape(1, -1), _lambda_init(l))

        rec = _lru(xr, gr, conv_w[l], conv_b[l].reshape(1, lru_w),
                   _block_diag(w_rg_a[l]).astype(BF16), b_rg_a[l].reshape(1, lru_w),
                   _block_diag(w_rg_x[l]).astype(BF16), b_rg_x[l].reshape(1, lru_w),
                   lru_lambda[l].reshape(1, lru_w))

        x1, hm2d, top_idx, gates = _outproj(
            att, rec, x, w_out[l].astype(BF16), g1, norm2_g[l].reshape(1, D), sc2, sh2,
            w_router[l].T, b_router[l].reshape(n_exp, 1), n_exp)

        block_e, nv, n_used, row_a, row_g = _route(top_idx, gates, n_exp)
        y4 = _moe(block_e, nv, n_used, row_a, row_g, hm2d,
                  w_moe1[l].astype(BF16), b_moe1[l].reshape(n_exp, 1, -1),
                  w_moe2[l].astype(BF16), b_moe2[l].reshape(n_exp, 1, -1), B * S)
        x = _combine(y4, x1, g2)
    return x
```

```python
import functools
import math

import jax
import jax.numpy as jnp
from jax import lax
from jax.experimental import pallas as pl
from jax.experimental.pallas import tpu as pltpu

F32 = jnp.float32
BF16 = jnp.bfloat16

CHUNK = 64
HEAD_DIM = 64
ROPE_THETA = 10000.0
LRU_BLOCK = 64
CONV_W = 4
LRU_C = 8.0
TOP_K = 4
SWIGLU_ALPHA = 1.702
SWIGLU_LIMIT = 7.0
NORM_EPS = 1e-6
SUBLN_EPS = 1e-5
LOG2_E = math.log2(math.e)
SCORE_BOUND = 60.0
BF16_SLACK = 1.02

LANES = 128
SUBLANES = 8
VMEM_LIMIT = 56 * 1024 * 1024

TM_PROJ = 512
TQ = 512
TK = 512
TT = 256
EB = 256
TM_COMB = 256


def _split_bf16(a):
    hi = a.astype(BF16)
    lo = (a - hi.astype(F32)).astype(BF16)
    return hi, lo


def _dot(a, b):
    return jnp.dot(a, b, preferred_element_type=F32)


def _dot3(a, b):
    ah, al = _split_bf16(a)
    bh, bl = _split_bf16(b)
    return _dot(ah, bh) + (_dot(ah, bl) + _dot(al, bh))


def _ada_kernel(c_ref, w_ref, b_ref, o_ref):
    c = c_ref[...]
    s = c * jax.nn.sigmoid(c)
    o_ref[...] = _dot3(s, w_ref[...]) + b_ref[...]


def _ada(c, w_ada, b_ada):
    B, D = c.shape
    N = w_ada.shape[1]
    tn = D
    return pl.pallas_call(
        _ada_kernel,
        grid=(N // tn,),
        in_specs=[pl.BlockSpec((B, D), lambda j: (0, 0)),
                  pl.BlockSpec((D, tn), lambda j: (0, j)),
                  pl.BlockSpec((1, tn), lambda j: (0, j))],
        out_specs=pl.BlockSpec((B, tn), lambda j: (0, j)),
        out_shape=jax.ShapeDtypeStruct((B, N), F32),
        compiler_params=pltpu.CompilerParams(dimension_semantics=("arbitrary",),
                                             vmem_limit_bytes=VMEM_LIMIT),
        name="ada",
    )(c, w_ada, b_ada.reshape(1, N))


def _inproj_kernel(x_ref, g_ref, sc_ref, sh_ref, w_ref, gsum_ref, qg_ref, kg_ref, cos_ref, sin_ref,
                   qt_ref, k_ref, vt_ref, xr_ref, gr_ref, *, attn_w, lru_w):
    x = x_ref[0]
    ms = jnp.mean(x * x, axis=-1, keepdims=True)
    hn = x * lax.rsqrt(ms + NORM_EPS) * g_ref[...]
    hn = (hn * (1.0 + sc_ref[0]) + sh_ref[0]).astype(BF16)

    cos = cos_ref[...]
    sin = sin_ref[...]
    gsum = gsum_ref[...]
    lane = lax.broadcasted_iota(jnp.int32, cos.shape, 1)
    first_half = (lane % HEAD_DIM) < (HEAD_DIM // 2)

    def head_norm_rope(t, gain):
        hi, lo = _split_bf16(t * t)
        ssum = _dot(hi, gsum) + _dot(lo, gsum)
        t = t * lax.rsqrt(ssum * (1.0 / HEAD_DIM) + NORM_EPS) * gain
        half = HEAD_DIM // 2
        partner = jnp.where(first_half, pltpu.roll(t, attn_w - half, 1), pltpu.roll(t, half, 1))
        return t * cos + partner * sin

    q = head_norm_rope(_dot(hn, w_ref[:, 0:attn_w]), qg_ref[...]) * (HEAD_DIM ** -0.5 * LOG2_E)
    for h in range(attn_w // LANES):
        qt_ref[0, h] = q[:, h * LANES:(h + 1) * LANES].T.astype(BF16)

    k = head_norm_rope(_dot(hn, w_ref[:, attn_w:2 * attn_w]), kg_ref[...])
    k_ref[0] = k.astype(BF16)

    v = _dot(hn, w_ref[:, 2 * attn_w:3 * attn_w])
    for h in range(attn_w // LANES):
        vt_ref[0, h] = v[:, h * LANES:(h + 1) * LANES].T.astype(BF16)
    xr_ref[0] = _dot(hn, w_ref[:, 3 * attn_w:3 * attn_w + lru_w])
    gr_ref[0] = _dot(hn, w_ref[:, 3 * attn_w + lru_w:3 * attn_w + 2 * lru_w])


def _inproj(x, g1n, sc1, sh1, w_in_b, gsum, qg, kg, cos_t, sin_t, attn_w, lru_w):
    B, S, D = x.shape
    tm = min(TM_PROJ, S)
    in_w = w_in_b.shape[1]
    nh = attn_w // LANES
    kern = functools.partial(_inproj_kernel, attn_w=attn_w, lru_w=lru_w)
    return pl.pallas_call(
        kern,
        grid=(S // tm, B),
        in_specs=[pl.BlockSpec((1, tm, D), lambda i, b: (b, i, 0)),
                  pl.BlockSpec((1, D), lambda i, b: (0, 0)),
                  pl.BlockSpec((1, 1, D), lambda i, b: (b, 0, 0)),
                  pl.BlockSpec((1, 1, D), lambda i, b: (b, 0, 0)),
                  pl.BlockSpec((D, in_w), lambda i, b: (0, 0)),
                  pl.BlockSpec((attn_w, attn_w), lambda i, b: (0, 0)),
                  pl.BlockSpec((1, attn_w), lambda i, b: (0, 0)),
                  pl.BlockSpec((1, attn_w), lambda i, b: (0, 0)),
                  pl.BlockSpec((tm, attn_w), lambda i, b: (i, 0)),
                  pl.BlockSpec((tm, attn_w), lambda i, b: (i, 0))],
        out_specs=[pl.BlockSpec((1, nh, LANES, tm), lambda i, b: (b, 0, 0, i)),
                   pl.BlockSpec((1, tm, attn_w), lambda i, b: (b, i, 0)),
                   pl.BlockSpec((1, nh, LANES, tm), lambda i, b: (b, 0, 0, i)),
                   pl.BlockSpec((1, tm, lru_w), lambda i, b: (b, i, 0)),
                   pl.BlockSpec((1, tm, lru_w), lambda i, b: (b, i, 0))],
        out_shape=[jax.ShapeDtypeStruct((B, nh, LANES, S), BF16),
                   jax.ShapeDtypeStruct((B, S, attn_w), BF16),
                   jax.ShapeDtypeStruct((B, nh, LANES, S), BF16),
                   jax.ShapeDtypeStruct((B, S, lru_w), F32),
                   jax.ShapeDtypeStruct((B, S, lru_w), F32)],
        compiler_params=pltpu.CompilerParams(dimension_semantics=("arbitrary", "arbitrary"),
                                             vmem_limit_bytes=VMEM_LIMIT),
        name="inproj",
    )(x, g1n, sc1, sh1, w_in_b, gsum, qg, kg, cos_t, sin_t)


def _attn_kernel(bounded_ref, qt_ref, k_ref, vt_ref, lq1_ref, lk1_ref, lq2_ref, lk2_ref, sg_ref, o_ref,
                 *, tq, tk, lam_init):
    qi = pl.program_id(2)
    qt = qt_ref[0, 0]
    row = lax.broadcasted_iota(jnp.int32, qt.shape, 0)
    zero = jnp.zeros_like(qt)
    qs = jnp.concatenate([jnp.where(row < HEAD_DIM, qt, zero),
                          jnp.where(row >= HEAD_DIM, qt, zero)], axis=1)

    def scores(j):
        kk = k_ref[0, pl.ds(pl.multiple_of(j * tk, tk), tk), :]
        return _dot(kk, qs)

    def values(j, p):
        vt = vt_ref[0, 0, :, pl.ds(pl.multiple_of(j * tk, tk), tk)]
        return _dot(vt, p)

    n_full = (qi * tq) // tk

    def visible(shape):
        key = lax.broadcasted_iota(jnp.int32, shape, 0) + n_full * tk
        qry = lax.broadcasted_iota(jnp.int32, shape, 1) % tq + qi * tq
        return (key // CHUNK) <= (qry // CHUNK)

    zeros = (jnp.zeros((1, 2 * tq), F32), jnp.zeros((LANES, 2 * tq), F32))

    def bounded_path():
        def body(j, carry):
            l, acc = carry
            p = jnp.exp2(scores(j))
            return l + jnp.sum(p, axis=0, keepdims=True), acc + values(j, p.astype(BF16))

        l, acc = lax.fori_loop(0, n_full, body, zeros)
        s = scores(n_full)
        p = jnp.where(visible(s.shape), jnp.exp2(s), 0.0)
        return l + jnp.sum(p, axis=0, keepdims=True), acc + values(n_full, p.astype(BF16))

    def online_path():
        def update(j, carry, masked):
            m, l, acc = carry
            s = scores(j)
            if masked:
                s = jnp.where(visible(s.shape), s, -jnp.inf)
            m_new = jnp.maximum(m, jnp.max(s, axis=0, keepdims=True))
            alpha = jnp.exp2(m - m_new)
            p = jnp.exp2(s - m_new)
            l = alpha * l + jnp.sum(p, axis=0, keepdims=True)
            return m_new, l, alpha * acc + values(j, p.astype(BF16))

        carry = (jnp.full((1, 2 * tq), -jnp.inf, F32),) + zeros
        carry = lax.fori_loop(0, n_full, lambda j, c: update(j, c, False), carry)
        _, l, acc = update(n_full, carry, True)
        return l, acc

    l, acc = lax.cond(bounded_ref[0] != 0, bounded_path, online_path)

    lam = (jnp.exp(jnp.sum(lq1_ref[...] * lk1_ref[...], axis=-1, keepdims=True))
           - jnp.exp(jnp.sum(lq2_ref[...] * lk2_ref[...], axis=-1, keepdims=True)) + lam_init)
    o = acc * (1.0 / l)
    o = o[:, :tq] - lam * o[:, tq:]
    ms = jnp.mean(o * o, axis=0, keepdims=True)
    o = o * lax.rsqrt(ms + SUBLN_EPS) * (sg_ref[...] * (1.0 - lam_init))
    o_ref[0] = o.T.astype(BF16)


def _attn(bounded, qt, k, vt, lq1, lk1, lq2, lk2, sg, lam_init):
    B, S, attn_w = k.shape
    nh = attn_w // LANES
    tq = min(TQ, S)
    tk = min(TK, S)
    assert tk % tq == 0, "one key tile must cover the whole query block (single masked tile)"
    kern = functools.partial(_attn_kernel, tq=tq, tk=tk, lam_init=lam_init)
    vec = lambda n: pl.BlockSpec((1, n), lambda b, h, i: (0, 0))
    return pl.pallas_call(
        kern,
        grid=(B, nh, S // tq),
        in_specs=[pl.BlockSpec(memory_space=pltpu.SMEM),
                  pl.BlockSpec((1, 1, LANES, tq), lambda b, h, i: (b, h, 0, i)),
                  pl.BlockSpec((1, S, LANES), lambda b, h, i: (b, 0, h)),
                  pl.BlockSpec((1, 1, LANES, S), lambda b, h, i: (b, h, 0, 0)),
                  vec(HEAD_DIM), vec(HEAD_DIM), vec(HEAD_DIM), vec(HEAD_DIM),
                  pl.BlockSpec((LANES, 1), lambda b, h, i: (0, 0))],
        out_specs=pl.BlockSpec((1, tq, LANES), lambda b, h, i: (b, i, h)),
        out_shape=jax.ShapeDtypeStruct((B, S, attn_w), BF16),
        compiler_params=pltpu.CompilerParams(
            dimension_semantics=("arbitrary", "arbitrary", "arbitrary"), vmem_limit_bytes=VMEM_LIMIT),
        name="attn",
    )(bounded, qt, k, vt, lq1, lk1, lq2, lk2, sg)


def _shift_rows(a, d, fill):
    row = lax.broadcasted_iota(jnp.int32, a.shape, 0)
    return jnp.where(row < d, fill, pltpu.roll(a, d, 0))


def _lru_kernel(xr_ref, gr_ref, cw_ref, cb_ref, wa_ref, ba_ref, wx_ref, bx_ref, lam_ref, o_ref,
                tail_ref, h_ref, *, tt):
    @pl.when(pl.program_id(1) == 0)
    def _():
        tail_ref[...] = jnp.zeros_like(tail_ref)
        h_ref[...] = jnp.zeros_like(h_ref)

    x = xr_ref[0]
    ext = jnp.concatenate([tail_ref[...], x], axis=0)
    xc = cb_ref[...] + cw_ref[CONV_W - 1:CONV_W, :] * x
    for j in range(CONV_W - 1):
        back = CONV_W - 1 - j
        xc = xc + cw_ref[j:j + 1, :] * ext[SUBLANES - back:SUBLANES - back + tt]
    tail_ref[...] = x[tt - SUBLANES:]

    xcb = xc.astype(BF16)
    r = jax.nn.sigmoid(_dot(xcb, wa_ref[...]) + ba_ref[...])
    i = jax.nn.sigmoid(_dot(xcb, wx_ref[...]) + bx_ref[...])
    nl = -lam_ref[...]
    softplus = jnp.maximum(nl, 0.0) + jnp.log(1.0 + jnp.exp(-jnp.abs(nl)))
    log_a = (-LRU_C) * r * softplus
    a = jnp.exp(log_a)
    th = jnp.tanh(log_a)
    u = jnp.sqrt((-2.0) * th / (1.0 - th)) * (i * xc)

    d = 1
    while d < tt:
        u = u + a * _shift_rows(u, d, 0.0)
        a = a * _shift_rows(a, d, 1.0)
        d *= 2
    h = a * h_ref[0:1, :] + u
    h_ref[...] = jnp.broadcast_to(h[tt - 1:tt, :], h_ref.shape)

    g = gr_ref[0]
    gelu = 0.5 * g * (1.0 + jnp.tanh(math.sqrt(2.0 / math.pi) * (g + 0.044715 * (g * g * g))))
    o_ref[0] = (h * gelu).astype(BF16)


def _lru(xr, gr, conv_w, conv_b, wa_bd, b_a, wx_bd, b_x, lam):
    B, S, W = xr.shape
    tt = min(TT, S)
    kern = functools.partial(_lru_kernel, tt=tt)
    vec = pl.BlockSpec((1, W), lambda b, i: (0, 0))
    mat = pl.BlockSpec((W, W), lambda b, i: (0, 0))
    return pl.pallas_call(
        kern,
        grid=(B, S // tt),
        in_specs=[pl.BlockSpec((1, tt, W), lambda b, i: (b, i, 0)),
                  pl.BlockSpec((1, tt, W), lambda b, i: (b, i, 0)),
                  pl.BlockSpec((CONV_W, W), lambda b, i: (0, 0)),
                  vec, mat, vec, mat, vec, vec],
        out_specs=pl.BlockSpec((1, tt, W), lambda b, i: (b, i, 0)),
        out_shape=jax.ShapeDtypeStruct((B, S, W), BF16),
        scratch_shapes=[pltpu.VMEM((SUBLANES, W), F32), pltpu.VMEM((SUBLANES, W), F32)],
        compiler_params=pltpu.CompilerParams(dimension_semantics=("arbitrary", "arbitrary"),
                                             vmem_limit_bytes=VMEM_LIMIT),
        name="lru",
    )(xr, gr, conv_w, conv_b, wa_bd, b_a, wx_bd, b_x, lam)


def _outproj_kernel(att_ref, rec_ref, x_ref, wo_ref, g1_ref, n2_ref, sc_ref, sh_ref, wr_ref, br_ref,
                    x1_ref, hm_ref, idx_ref, gate_ref, *, attn_w, n_exp):
    tm = x_ref.shape[1]
    mix = _dot(att_ref[0], wo_ref[0:attn_w, :]) + _dot(rec_ref[0], wo_ref[attn_w:, :])
    x1 = x_ref[0] + g1_ref[0] * mix
    x1_ref[0] = x1
    ms = jnp.mean(x1 * x1, axis=-1, keepdims=True)
    hm = x1 * lax.rsqrt(ms + NORM_EPS) * n2_ref[...]
    hm = hm * (1.0 + sc_ref[0]) + sh_ref[0]
    for c in range(hm.shape[1] // LANES):
        hm_ref[pl.ds(c, tm, stride=SUBLANES), :] = hm[:, c * LANES:(c + 1) * LANES]

    hh, hl = _split_bf16(hm)
    wh, wl = _split_bf16(wr_ref[...])
    nt = (((1,), (1,)), ((), ()))
    logits = (lax.dot_general(wh, hh, nt, preferred_element_type=F32)
              + (lax.dot_general(wh, hl, nt, preferred_element_type=F32)
                 + lax.dot_general(wl, hh, nt, preferred_element_type=F32))) + br_ref[...]
    eidx = lax.broadcasted_iota(jnp.int32, logits.shape, 0)
    vals, idxs = [], []
    for _ in range(TOP_K):
        m = jnp.max(logits, axis=0, keepdims=True)
        am = jnp.min(jnp.where(logits == m, eidx, n_exp), axis=0, keepdims=True)
        vals.append(m)
        idxs.append(am)
        logits = jnp.where(eidx == am, -jnp.inf, logits)
    ex = [jnp.exp(v - vals[0]) for v in vals]
    den = ex[0] + ex[1] + ex[2] + ex[3]
    idx_ref[0] = jnp.concatenate(idxs, axis=0)
    gate_ref[0] = jnp.concatenate([e / den for e in ex], axis=0)


def _outproj(att, rec, x, wo_b, g1, n2g, sc2, sh2, wr_t, br, n_exp):
    B, S, D = x.shape
    attn_w = att.shape[2]
    lru_w = rec.shape[2]
    tm = min(TM_PROJ, S)
    ns = S // tm
    rows = D // LANES
    kern = functools.partial(_outproj_kernel, attn_w=attn_w, n_exp=n_exp)
    bvec = pl.BlockSpec((1, 1, D), lambda b, i: (b, 0, 0))
    return pl.pallas_call(
        kern,
        grid=(B, ns),
        in_specs=[pl.BlockSpec((1, tm, attn_w), lambda b, i: (b, i, 0)),
                  pl.BlockSpec((1, tm, lru_w), lambda b, i: (b, i, 0)),
                  pl.BlockSpec((1, tm, D), lambda b, i: (b, i, 0)),
                  pl.BlockSpec((attn_w + lru_w, D), lambda b, i: (0, 0)),
                  bvec,
                  pl.BlockSpec((1, D), lambda b, i: (0, 0)),
                  bvec, bvec,
                  pl.BlockSpec((n_exp, D), lambda b, i: (0, 0)),
                  pl.BlockSpec((n_exp, 1), lambda b, i: (0, 0))],
        out_specs=[pl.BlockSpec((1, tm, D), lambda b, i: (b, i, 0)),
                   pl.BlockSpec((tm * rows, LANES), lambda b, i: (b * ns + i, 0)),
                   pl.BlockSpec((1, TOP_K, tm), lambda b, i: (b, 0, i)),
                   pl.BlockSpec((1, TOP_K, tm), lambda b, i: (b, 0, i))],
        out_shape=[jax.ShapeDtypeStruct((B, S, D), F32),
                   jax.ShapeDtypeStruct((B * S * rows, LANES), F32),
                   jax.ShapeDtypeStruct((B, TOP_K, S), jnp.int32),
                   jax.ShapeDtypeStruct((B, TOP_K, S), F32)],
        compiler_params=pltpu.CompilerParams(dimension_semantics=("arbitrary", "arbitrary"),
                                             vmem_limit_bytes=VMEM_LIMIT),
        name="outproj",
    )(att, rec, x, wo_b, g1, n2g, sc2, sh2, wr_t, br)


def _moe_kernel(be_ref, nv_ref, nused_ref, rowidx_hbm, g_ref, hm_hbm, w1_ref, b1_ref, w2_ref, b2_ref,
                y_hbm, idx_smem, xbuf, ybuf, sem_i, sem_g, sem_s, *, d_ff):
    i = pl.program_id(0)
    n_used = nused_ref[0]
    rows = SUBLANES
    slot = i % 2
    nxt = 1 - slot

    def idx_copy(blk, s):
        return pltpu.make_async_copy(rowidx_hbm.at[pl.ds(pl.multiple_of(blk * (2 * EB), 2 * EB), 2 * EB)],
                                     idx_smem.at[s], sem_i.at[s])

    def row_slice(r, n=1):
        return pl.ds(pl.multiple_of(r * rows, rows), n * rows)

    def start_gathers(blk, s):
        def body(r, c):
            tok = idx_smem[s, r]
            pltpu.make_async_copy(hm_hbm.at[row_slice(tok), :], xbuf.at[s, row_slice(r), :], sem_g.at[s]).start()
            return c
        lax.fori_loop(0, nv_ref[blk], body, 0)

    def start_scatters(blk, s):
        def body(r, c):
            dst = idx_smem[s, EB + r]
            pltpu.make_async_copy(ybuf.at[s, row_slice(r), :], y_hbm.at[row_slice(dst), :], sem_s.at[s]).start()
            return c
        lax.fori_loop(0, nv_ref[blk], body, 0)

    def wait_rows(n, s, gather):
        size = EB
        while size >= 1:
            @pl.when((n & size) != 0)
            def _():
                if gather:
                    pltpu.make_async_copy(hm_hbm.at[row_slice(0, size), :], xbuf.at[s, row_slice(0, size), :],
                                          sem_g.at[s]).wait()
                else:
                    pltpu.make_async_copy(ybuf.at[s, row_slice(0, size), :], y_hbm.at[row_slice(0, size), :],
                                          sem_s.at[s]).wait()
            size //= 2

    @pl.when(i == 0)
    def _():
        xbuf[...] = jnp.zeros_like(xbuf)
        idx_copy(0, 0).start()
        idx_copy(0, 0).wait()
        start_gathers(0, 0)

    @pl.when(i < n_used)
    def _():
        @pl.when(i + 1 < n_used)
        def _():
            idx_copy(i + 1, nxt).start()

        wait_rows(nv_ref[i], slot, True)

        @pl.when(i + 1 < n_used)
        def _():
            idx_copy(i + 1, nxt).wait()
            start_gathers(i + 1, nxt)

        @pl.when(i >= 2)
        def _():
            wait_rows(nv_ref[jnp.maximum(i - 2, 0)], slot, False)

        x = jnp.concatenate([xbuf[slot, pl.ds(c, EB, stride=rows), :] for c in range(rows)], axis=1)
        hu = _dot(x.astype(BF16), w1_ref[0]) + b1_ref[0]
        glu = jnp.minimum(hu[:, :d_ff], SWIGLU_LIMIT)
        lin = jnp.clip(hu[:, d_ff:], -SWIGLU_LIMIT, SWIGLU_LIMIT)
        act = glu * jax.nn.sigmoid(SWIGLU_ALPHA * glu) * (lin + 1.0)
        y = (_dot(act.astype(BF16), w2_ref[0]) + b2_ref[0]) * g_ref[...]
        for c in range(rows):
            ybuf[slot, pl.ds(c, EB, stride=rows), :] = y[:, c * LANES:(c + 1) * LANES]

        start_scatters(i, slot)

        @pl.when(i == n_used - 1)
        def _():
            wait_rows(nv_ref[i], slot, False)

            @pl.when(i >= 1)
            def _():
                wait_rows(nv_ref[jnp.maximum(i - 1, 0)], nxt, False)


def _moe(block_e, nv, n_used, row_a, row_g, hm2d, w1b, b1, w2b, b2, n_tok):
    n_blocks = block_e.shape[0]
    n_exp, D, ff2 = w1b.shape
    d_ff = ff2 // 2
    rows = D // LANES
    assert rows == SUBLANES, "a token row must fill exactly one (8,128) f32 tile"
    kern = functools.partial(_moe_kernel, d_ff=d_ff)
    grid_spec = pltpu.PrefetchScalarGridSpec(
        num_scalar_prefetch=3,
        grid=(n_blocks,),
        in_specs=[pl.BlockSpec(memory_space=pl.ANY),
                  pl.BlockSpec((EB, 1), lambda i, be, nv, nu: (i, 0)),
                  pl.BlockSpec(memory_space=pl.ANY),
                  pl.BlockSpec((1, D, ff2), lambda i, be, nv, nu: (be[i], 0, 0)),
                  pl.BlockSpec((1, 1, ff2), lambda i, be, nv, nu: (be[i], 0, 0)),
                  pl.BlockSpec((1, d_ff, D), lambda i, be, nv, nu: (be[i], 0, 0)),
                  pl.BlockSpec((1, 1, D), lambda i, be, nv, nu: (be[i], 0, 0))],
        out_specs=pl.BlockSpec(memory_space=pl.ANY),
        scratch_shapes=[pltpu.SMEM((2, 2 * EB), jnp.int32),
                        pltpu.VMEM((2, EB * rows, LANES), F32),
                        pltpu.VMEM((2, EB * rows, LANES), F32),
                        pltpu.SemaphoreType.DMA((2,)),
                        pltpu.SemaphoreType.DMA((2,)),
                        pltpu.SemaphoreType.DMA((2,))],
    )
    return pl.pallas_call(
        kern,
        grid_spec=grid_spec,
        out_shape=jax.ShapeDtypeStruct((TOP_K * n_tok * rows, LANES), F32),
        compiler_params=pltpu.CompilerParams(dimension_semantics=("arbitrary",),
                                             vmem_limit_bytes=VMEM_LIMIT),
        name="moe",
    )(block_e, nv, n_used, row_a, row_g, hm2d, w1b, b1, w2b, b2)


def _combine_kernel(y_ref, x1_ref, g2_ref, o_ref, s_ref):
    tm = x1_ref.shape[1]
    s_ref[...] = (y_ref[0] + y_ref[1]) + (y_ref[2] + y_ref[3])
    moe = jnp.concatenate([s_ref[pl.ds(c, tm, stride=SUBLANES), :] for c in range(SUBLANES)], axis=1)
    o_ref[0] = x1_ref[0] + g2_ref[0] * moe


def _combine(y4, x1, g2):
    B, S, D = x1.shape
    rows = D // LANES
    tm = min(TM_COMB, S)
    ns = S // tm
    return pl.pallas_call(
        _combine_kernel,
        grid=(B, ns),
        in_specs=[pl.BlockSpec((TOP_K, tm * rows, LANES), lambda b, i: (0, b * ns + i, 0)),
                  pl.BlockSpec((1, tm, D), lambda b, i: (b, i, 0)),
                  pl.BlockSpec((1, 1, D), lambda b, i: (b, 0, 0))],
        out_specs=pl.BlockSpec((1, tm, D), lambda b, i: (b, i, 0)),
        out_shape=jax.ShapeDtypeStruct((B, S, D), F32),
        scratch_shapes=[pltpu.VMEM((tm * rows, LANES), F32)],
        compiler_params=pltpu.CompilerParams(dimension_semantics=("arbitrary", "arbitrary"),
                                             vmem_limit_bytes=VMEM_LIMIT),
        name="combine",
    )(y4.reshape(TOP_K, B * S * rows, LANES), x1, g2)


def _route(top_idx, gates, n_exp):
    B, K, S = top_idx.shape
    T = B * S
    A = T * K
    flat_e = top_idx.transpose(0, 2, 1).reshape(A)
    flat_g = gates.transpose(0, 2, 1).reshape(A)
    order = jnp.argsort(flat_e, stable=True).astype(jnp.int32)
    counts = jnp.sum(flat_e[:, None] == jnp.arange(n_exp, dtype=jnp.int32)[None, :], axis=0, dtype=jnp.int32)
    nblk = (counts + EB - 1) // EB
    bend = jnp.cumsum(nblk)
    bstart = bend - nblk
    ustart = jnp.cumsum(counts) - counts
    n_blocks = -(-A // EB) + n_exp
    blk = jnp.arange(n_blocks, dtype=jnp.int32)
    block_e = jnp.minimum(jnp.searchsorted(bend, blk, side='right'), n_exp - 1).astype(jnp.int32)
    off = (blk - bstart[block_e]) * EB
    nv = jnp.where(blk < bend[-1], jnp.clip(counts[block_e] - off, 0, EB), 0).astype(jnp.int32)
    r = jnp.arange(EB, dtype=jnp.int32)
    j = (ustart[block_e] + off)[:, None] + r[None, :]
    valid = r[None, :] < nv[:, None]
    row_a = jnp.where(valid, order[jnp.clip(j, 0, A - 1)], 0).astype(jnp.int32)
    row_g = jnp.where(valid, flat_g[row_a], 0.0).reshape(n_blocks * EB, 1)
    row_tok = row_a // K
    row_dst = (row_a % K) * T + row_tok
    row_idx = jnp.concatenate([row_tok, row_dst], axis=1).reshape(n_blocks * 2 * EB)
    return block_e, nv, bend[-1:].astype(jnp.int32), row_idx, row_g


def _lambda_init(layer_idx):
    return 0.8 - 0.6 * math.exp(-0.3 * layer_idx)


def _block_diag(w):
    n, c, d = w.shape
    eye = jnp.eye(n, dtype=w.dtype)
    return (eye[:, None, :, None] * w[:, :, None, :]).reshape(n * c, n * d)


def kernel(x, c, w_ada, b_ada, norm1_g, w_in, q_norm_g, k_norm_g, lambda_q1, lambda_k1, lambda_q2, lambda_k2, subln_g, conv_w, conv_b, w_rg_a, b_rg_a, w_rg_x, b_rg_x, lru_lambda, w_out, norm2_g, w_router, b_router, w_moe1, b_moe1, w_moe2, b_moe2):
    B, S, D = x.shape
    depth = w_ada.shape[0]
    lru_w = conv_w.shape[2]
    attn_w = (w_in.shape[2] - 2 * lru_w) // 3
    n_exp = w_router.shape[2]
    n_heads = attn_w // HEAD_DIM
    assert attn_w % LANES == 0 and S % CHUNK == 0

    half = HEAD_DIM // 2
    inv = ROPE_THETA ** (-jnp.arange(half, dtype=F32) / half)
    ang = jnp.arange(S, dtype=jnp.int32).astype(F32)[:, None] * inv[None, :]
    cos_t = jnp.tile(jnp.concatenate([jnp.cos(ang), jnp.cos(ang)], axis=1), (1, n_heads))
    sin_t = jnp.tile(jnp.concatenate([-jnp.sin(ang), jnp.sin(ang)], axis=1), (1, n_heads))
    group = jnp.arange(attn_w, dtype=jnp.int32) // HEAD_DIM
    gsum = (group[:, None] == group[None, :]).astype(BF16)

    for l in range(depth):
        mod = _ada(c, w_ada[l], b_ada[l])
        sh1, sc1, g1, sh2, sc2, g2 = [m.reshape(B, 1, D) for m in jnp.split(mod, 6, axis=-1)]

        qt, k, vt, xr, gr = _inproj(
            x, norm1_g[l].reshape(1, D), sc1, sh1, w_in[l].astype(BF16), gsum,
            jnp.tile(q_norm_g[l], n_heads).reshape(1, attn_w), jnp.tile(k_norm_g[l], n_heads).reshape(1, attn_w),
            cos_t, sin_t, attn_w, lru_w)

        score_bound = (HEAD_DIM ** 0.5 * LOG2_E * BF16_SLACK) * jnp.max(jnp.abs(q_norm_g[l])) * jnp.max(jnp.abs(k_norm_g[l]))
        bounded = (score_bound <= SCORE_BOUND).astype(jnp.int32).reshape(1)
        att = _attn(bounded, qt, k, vt, lambda_q1[l].reshape(1, -1), lambda_k1[l].reshape(1, -1),
                    lambda_q2[l].reshape(1, -1), lambda_k2[l].reshape(1, -1),
                    subln_g[l].reshape(-1, 1), _lambda_init(l))

        rec = _lru(xr, gr, conv_w[l], conv_b[l].reshape(1, lru_w),
                   _block_diag(w_rg_a[l]).astype(BF16), b_rg_a[l].reshape(1, lru_w),
                   _block_diag(w_rg_x[l]).astype(BF16), b_rg_x[l].reshape(1, lru_w),
                   lru_lambda[l].reshape(1, lru_w))

        x1, hm2d, top_idx, gates = _outproj(
            att, rec, x, w_out[l].astype(BF16), g1, norm2_g[l].reshape(1, D), sc2, sh2,
            w_router[l].T, b_router[l].reshape(n_exp, 1), n_exp)

        block_e, nv, n_used, row_a, row_g = _route(top_idx, gates, n_exp)
        y4 = _moe(block_e, nv, n_used, row_a, row_g, hm2d,
                  w_moe1[l].astype(BF16), b_moe1[l].reshape(n_exp, 1, -1),
                  w_moe2[l].astype(BF16), b_moe2[l].reshape(n_exp, 1, -1), B * S)
        x = _combine(y4, x1, g2)
    return x
```

```python
import functools
import math

import jax
import jax.numpy as jnp
from jax import lax
from jax.experimental import pallas as pl
from jax.experimental.pallas import tpu as pltpu
from jax.experimental.pallas import tpu_sc as plsc

F32 = jnp.float32
BF16 = jnp.bfloat16

CHUNK = 64
HEAD_DIM = 64
ROPE_THETA = 10000.0
LRU_BLOCK = 64
CONV_W = 4
LRU_C = 8.0
TOP_K = 4
SWIGLU_ALPHA = 1.702
SWIGLU_LIMIT = 7.0
NORM_EPS = 1e-6
SUBLN_EPS = 1e-5
LOG2_E = math.log2(math.e)
SCORE_BOUND = 60.0
BF16_SLACK = 1.02

LANES = 128
SUBLANES = 8
VMEM_LIMIT = 56 * 1024 * 1024
SC_CORES = 2
SC_SUBCORES = 16
SC_CHUNK = 32

TM_PROJ = 512
TQ = 512
TK = 512
TT = 256
EB = 256
TM_COMB = 256


def _split_bf16(a):
    hi = a.astype(BF16)
    lo = (a - hi.astype(F32)).astype(BF16)
    return hi, lo


def _dot(a, b):
    return jnp.dot(a, b, preferred_element_type=F32)


def _dot3(a, b):
    ah, al = _split_bf16(a)
    bh, bl = _split_bf16(b)
    return _dot(ah, bh) + (_dot(ah, bl) + _dot(al, bh))


def _ada_kernel(c_ref, w_ref, b_ref, o_ref):
    c = c_ref[...]
    s = c * jax.nn.sigmoid(c)
    o_ref[...] = _dot3(s, w_ref[...]) + b_ref[...]


def _ada(c, w_ada, b_ada):
    B, D = c.shape
    N = w_ada.shape[1]
    tn = D
    return pl.pallas_call(
        _ada_kernel,
        grid=(N // tn,),
        in_specs=[pl.BlockSpec((B, D), lambda j: (0, 0)),
                  pl.BlockSpec((D, tn), lambda j: (0, j)),
                  pl.BlockSpec((1, tn), lambda j: (0, j))],
        out_specs=pl.BlockSpec((B, tn), lambda j: (0, j)),
        out_shape=jax.ShapeDtypeStruct((B, N), F32),
        compiler_params=pltpu.CompilerParams(dimension_semantics=("arbitrary",),
                                             vmem_limit_bytes=VMEM_LIMIT),
        name="ada",
    )(c, w_ada, b_ada.reshape(1, N))


def _inproj_kernel(x_ref, g_ref, sc_ref, sh_ref, w_ref, gsum_ref, qg_ref, kg_ref, cos_ref, sin_ref,
                   qt_ref, k_ref, vt_ref, xr_ref, gr_ref, *, attn_w, lru_w):
    x = x_ref[0]
    ms = jnp.mean(x * x, axis=-1, keepdims=True)
    hn = x * lax.rsqrt(ms + NORM_EPS) * g_ref[...]
    hn = (hn * (1.0 + sc_ref[0]) + sh_ref[0]).astype(BF16)

    cos = cos_ref[...]
    sin = sin_ref[...]
    gsum = gsum_ref[...]
    lane = lax.broadcasted_iota(jnp.int32, cos.shape, 1)
    first_half = (lane % HEAD_DIM) < (HEAD_DIM // 2)

    def head_norm_rope(t, gain):
        hi, lo = _split_bf16(t * t)
        ssum = _dot(hi, gsum) + _dot(lo, gsum)
        t = t * lax.rsqrt(ssum * (1.0 / HEAD_DIM) + NORM_EPS) * gain
        half = HEAD_DIM // 2
        partner = jnp.where(first_half, pltpu.roll(t, attn_w - half, 1), pltpu.roll(t, half, 1))
        return t * cos + partner * sin

    q = head_norm_rope(_dot(hn, w_ref[:, 0:attn_w]), qg_ref[...]) * (HEAD_DIM ** -0.5 * LOG2_E)
    for h in range(attn_w // LANES):
        qt_ref[0, h] = q[:, h * LANES:(h + 1) * LANES].T.astype(BF16)

    k = head_norm_rope(_dot(hn, w_ref[:, attn_w:2 * attn_w]), kg_ref[...])
    k_ref[0] = k.astype(BF16)

    v = _dot(hn, w_ref[:, 2 * attn_w:3 * attn_w])
    for h in range(attn_w // LANES):
        vt_ref[0, h] = v[:, h * LANES:(h + 1) * LANES].T.astype(BF16)
    xr_ref[0] = _dot(hn, w_ref[:, 3 * attn_w:3 * attn_w + lru_w])
    gr_ref[0] = _dot(hn, w_ref[:, 3 * attn_w + lru_w:3 * attn_w + 2 * lru_w])


def _inproj(x, g1n, sc1, sh1, w_in_b, gsum, qg, kg, cos_t, sin_t, attn_w, lru_w):
    B, S, D = x.shape
    tm = min(TM_PROJ, S)
    in_w = w_in_b.shape[1]
    nh = attn_w // LANES
    kern = functools.partial(_inproj_kernel, attn_w=attn_w, lru_w=lru_w)
    return pl.pallas_call(
        kern,
        grid=(S // tm, B),
        in_specs=[pl.BlockSpec((1, tm, D), lambda i, b: (b, i, 0)),
                  pl.BlockSpec((1, D), lambda i, b: (0, 0)),
                  pl.BlockSpec((1, 1, D), lambda i, b: (b, 0, 0)),
                  pl.BlockSpec((1, 1, D), lambda i, b: (b, 0, 0)),
                  pl.BlockSpec((D, in_w), lambda i, b: (0, 0)),
                  pl.BlockSpec((attn_w, attn_w), lambda i, b: (0, 0)),
                  pl.BlockSpec((1, attn_w), lambda i, b: (0, 0)),
                  pl.BlockSpec((1, attn_w), lambda i, b: (0, 0)),
                  pl.BlockSpec((tm, attn_w), lambda i, b: (i, 0)),
                  pl.BlockSpec((tm, attn_w), lambda i, b: (i, 0))],
        out_specs=[pl.BlockSpec((1, nh, LANES, tm), lambda i, b: (b, 0, 0, i)),
                   pl.BlockSpec((1, tm, attn_w), lambda i, b: (b, i, 0)),
                   pl.BlockSpec((1, nh, LANES, tm), lambda i, b: (b, 0, 0, i)),
                   pl.BlockSpec((1, tm, lru_w), lambda i, b: (b, i, 0)),
                   pl.BlockSpec((1, tm, lru_w), lambda i, b: (b, i, 0))],
        out_shape=[jax.ShapeDtypeStruct((B, nh, LANES, S), BF16),
                   jax.ShapeDtypeStruct((B, S, attn_w), BF16),
                   jax.ShapeDtypeStruct((B, nh, LANES, S), BF16),
                   jax.ShapeDtypeStruct((B, S, lru_w), F32),
                   jax.ShapeDtypeStruct((B, S, lru_w), F32)],
        compiler_params=pltpu.CompilerParams(dimension_semantics=("arbitrary", "arbitrary"),
                                             vmem_limit_bytes=VMEM_LIMIT),
        name="inproj",
    )(x, g1n, sc1, sh1, w_in_b, gsum, qg, kg, cos_t, sin_t)


def _attn_kernel(bounded_ref, qt_ref, k_ref, vt_ref, lq1_ref, lk1_ref, lq2_ref, lk2_ref, sg_ref, o_ref,
                 *, tq, tk, lam_init):
    qi = pl.program_id(2)
    qt = qt_ref[0, 0]
    row = lax.broadcasted_iota(jnp.int32, qt.shape, 0)
    zero = jnp.zeros_like(qt)
    qs = jnp.concatenate([jnp.where(row < HEAD_DIM, qt, zero),
                          jnp.where(row >= HEAD_DIM, qt, zero)], axis=1)

    def scores(j):
        kk = k_ref[0, pl.ds(pl.multiple_of(j * tk, tk), tk), :]
        return _dot(kk, qs)

    def values(j, p):
        vt = vt_ref[0, 0, :, pl.ds(pl.multiple_of(j * tk, tk), tk)]
        return _dot(vt, p)

    n_full = (qi * tq) // tk

    def visible(shape):
        key = lax.broadcasted_iota(jnp.int32, shape, 0) + n_full * tk
        qry = lax.broadcasted_iota(jnp.int32, shape, 1) % tq + qi * tq
        return (key // CHUNK) <= (qry // CHUNK)

    zeros = (jnp.zeros((1, 2 * tq), F32), jnp.zeros((LANES, 2 * tq), F32))

    def bounded_path():
        def body(j, carry):
            l, acc = carry
            p = jnp.exp2(scores(j))
            return l + jnp.sum(p, axis=0, keepdims=True), acc + values(j, p.astype(BF16))

        l, acc = lax.fori_loop(0, n_full, body, zeros)
        s = scores(n_full)
        p = jnp.where(visible(s.shape), jnp.exp2(s), 0.0)
        return l + jnp.sum(p, axis=0, keepdims=True), acc + values(n_full, p.astype(BF16))

    def online_path():
        def update(j, carry, masked):
            m, l, acc = carry
            s = scores(j)
            if masked:
                s = jnp.where(visible(s.shape), s, -jnp.inf)
            m_new = jnp.maximum(m, jnp.max(s, axis=0, keepdims=True))
            alpha = jnp.exp2(m - m_new)
            p = jnp.exp2(s - m_new)
            l = alpha * l + jnp.sum(p, axis=0, keepdims=True)
            return m_new, l, alpha * acc + values(j, p.astype(BF16))

        carry = (jnp.full((1, 2 * tq), -jnp.inf, F32),) + zeros
        carry = lax.fori_loop(0, n_full, lambda j, c: update(j, c, False), carry)
        _, l, acc = update(n_full, carry, True)
        return l, acc

    l, acc = lax.cond(bounded_ref[0] != 0, bounded_path, online_path)

    lam = (jnp.exp(jnp.sum(lq1_ref[...] * lk1_ref[...], axis=-1, keepdims=True))
           - jnp.exp(jnp.sum(lq2_ref[...] * lk2_ref[...], axis=-1, keepdims=True)) + lam_init)
    o = acc * (1.0 / l)
    o = o[:, :tq] - lam * o[:, tq:]
    ms = jnp.mean(o * o, axis=0, keepdims=True)
    o = o * lax.rsqrt(ms + SUBLN_EPS) * (sg_ref[...] * (1.0 - lam_init))
    o_ref[0] = o.T.astype(BF16)


def _attn(bounded, qt, k, vt, lq1, lk1, lq2, lk2, sg, lam_init):
    B, S, attn_w = k.shape
    nh = attn_w // LANES
    tq = min(TQ, S)
    tk = min(TK, S)
    assert tk % tq == 0, "one key tile must cover the whole query block (single masked tile)"
    kern = functools.partial(_attn_kernel, tq=tq, tk=tk, lam_init=lam_init)
    vec = lambda n: pl.BlockSpec((1, n), lambda b, h, i: (0, 0))
    return pl.pallas_call(
        kern,
        grid=(B, nh, S // tq),
        in_specs=[pl.BlockSpec(memory_space=pltpu.SMEM),
                  pl.BlockSpec((1, 1, LANES, tq), lambda b, h, i: (b, h, 0, i)),
                  pl.BlockSpec((1, S, LANES), lambda b, h, i: (b, 0, h)),
                  pl.BlockSpec((1, 1, LANES, S), lambda b, h, i: (b, h, 0, 0)),
                  vec(HEAD_DIM), vec(HEAD_DIM), vec(HEAD_DIM), vec(HEAD_DIM),
                  pl.BlockSpec((LANES, 1), lambda b, h, i: (0, 0))],
        out_specs=pl.BlockSpec((1, tq, LANES), lambda b, h, i: (b, i, h)),
        out_shape=jax.ShapeDtypeStruct((B, S, attn_w), BF16),
        compiler_params=pltpu.CompilerParams(
            dimension_semantics=("arbitrary", "arbitrary", "arbitrary"), vmem_limit_bytes=VMEM_LIMIT),
        name="attn",
    )(bounded, qt, k, vt, lq1, lk1, lq2, lk2, sg)


def _shift_rows(a, d, fill):
    row = lax.broadcasted_iota(jnp.int32, a.shape, 0)
    return jnp.where(row < d, fill, pltpu.roll(a, d, 0))


def _lru_kernel(xr_ref, gr_ref, cw_ref, cb_ref, wa_ref, ba_ref, wx_ref, bx_ref, lam_ref, o_ref,
                tail_ref, h_ref, *, tt):
    @pl.when(pl.program_id(1) == 0)
    def _():
        tail_ref[...] = jnp.zeros_like(tail_ref)
        h_ref[...] = jnp.zeros_like(h_ref)

    x = xr_ref[0]
    ext = jnp.concatenate([tail_ref[...], x], axis=0)
    xc = cb_ref[...] + cw_ref[CONV_W - 1:CONV_W, :] * x
    for j in range(CONV_W - 1):
        back = CONV_W - 1 - j
        xc = xc + cw_ref[j:j + 1, :] * ext[SUBLANES - back:SUBLANES - back + tt]
    tail_ref[...] = x[tt - SUBLANES:]

    xcb = xc.astype(BF16)
    r = jax.nn.sigmoid(_dot(xcb, wa_ref[...]) + ba_ref[...])
    i = jax.nn.sigmoid(_dot(xcb, wx_ref[...]) + bx_ref[...])
    nl = -lam_ref[...]
    softplus = jnp.maximum(nl, 0.0) + jnp.log(1.0 + jnp.exp(-jnp.abs(nl)))
    log_a = (-LRU_C) * r * softplus
    a = jnp.exp(log_a)
    th = jnp.tanh(log_a)
    u = jnp.sqrt((-2.0) * th / (1.0 - th)) * (i * xc)

    d = 1
    while d < tt:
        u = u + a * _shift_rows(u, d, 0.0)
        a = a * _shift_rows(a, d, 1.0)
        d *= 2
    h = a * h_ref[0:1, :] + u
    h_ref[...] = jnp.broadcast_to(h[tt - 1:tt, :], h_ref.shape)

    g = gr_ref[0]
    gelu = 0.5 * g * (1.0 + jnp.tanh(math.sqrt(2.0 / math.pi) * (g + 0.044715 * (g * g * g))))
    o_ref[0] = (h * gelu).astype(BF16)


def _lru(xr, gr, conv_w, conv_b, wa_bd, b_a, wx_bd, b_x, lam):
    B, S, W = xr.shape
    tt = min(TT, S)
    kern = functools.partial(_lru_kernel, tt=tt)
    vec = pl.BlockSpec((1, W), lambda b, i: (0, 0))
    mat = pl.BlockSpec((W, W), lambda b, i: (0, 0))
    return pl.pallas_call(
        kern,
        grid=(B, S // tt),
        in_specs=[pl.BlockSpec((1, tt, W), lambda b, i: (b, i, 0)),
                  pl.BlockSpec((1, tt, W), lambda b, i: (b, i, 0)),
                  pl.BlockSpec((CONV_W, W), lambda b, i: (0, 0)),
                  vec, mat, vec, mat, vec, vec],
        out_specs=pl.BlockSpec((1, tt, W), lambda b, i: (b, i, 0)),
        out_shape=jax.ShapeDtypeStruct((B, S, W), BF16),
        scratch_shapes=[pltpu.VMEM((SUBLANES, W), F32), pltpu.VMEM((SUBLANES, W), F32)],
        compiler_params=pltpu.CompilerParams(dimension_semantics=("arbitrary", "arbitrary"),
                                             vmem_limit_bytes=VMEM_LIMIT),
        name="lru",
    )(xr, gr, conv_w, conv_b, wa_bd, b_a, wx_bd, b_x, lam)


def _outproj_kernel(att_ref, rec_ref, x_ref, wo_ref, g1_ref, n2_ref, sc_ref, sh_ref, wr_ref, br_ref,
                    x1_ref, hm_ref, idx_ref, gate_ref, *, attn_w, n_exp):
    tm = x_ref.shape[1]
    mix = _dot(att_ref[0], wo_ref[0:attn_w, :]) + _dot(rec_ref[0], wo_ref[attn_w:, :])
    x1 = x_ref[0] + g1_ref[0] * mix
    x1_ref[0] = x1
    ms = jnp.mean(x1 * x1, axis=-1, keepdims=True)
    hm = x1 * lax.rsqrt(ms + NORM_EPS) * n2_ref[...]
    hm = hm * (1.0 + sc_ref[0]) + sh_ref[0]
    for c in range(hm.shape[1] // LANES):
        hm_ref[pl.ds(c, tm, stride=SUBLANES), :] = hm[:, c * LANES:(c + 1) * LANES]

    hh, hl = _split_bf16(hm)
    wh, wl = _split_bf16(wr_ref[...])
    nt = (((1,), (1,)), ((), ()))
    logits = (lax.dot_general(wh, hh, nt, preferred_element_type=F32)
              + (lax.dot_general(wh, hl, nt, preferred_element_type=F32)
                 + lax.dot_general(wl, hh, nt, preferred_element_type=F32))) + br_ref[...]
    eidx = lax.broadcasted_iota(jnp.int32, logits.shape, 0)
    vals, idxs = [], []
    for _ in range(TOP_K):
        m = jnp.max(logits, axis=0, keepdims=True)
        am = jnp.min(jnp.where(logits == m, eidx, n_exp), axis=0, keepdims=True)
        vals.append(m)
        idxs.append(am)
        logits = jnp.where(eidx == am, -jnp.inf, logits)
    ex = [jnp.exp(v - vals[0]) for v in vals]
    den = ex[0] + ex[1] + ex[2] + ex[3]
    idx_ref[0] = jnp.concatenate(idxs, axis=0)
    gate_ref[0] = jnp.concatenate([e / den for e in ex], axis=0)


def _outproj(att, rec, x, wo_b, g1, n2g, sc2, sh2, wr_t, br, n_exp):
    B, S, D = x.shape
    attn_w = att.shape[2]
    lru_w = rec.shape[2]
    tm = min(TM_PROJ, S)
    ns = S // tm
    rows = D // LANES
    kern = functools.partial(_outproj_kernel, attn_w=attn_w, n_exp=n_exp)
    bvec = pl.BlockSpec((1, 1, D), lambda b, i: (b, 0, 0))
    return pl.pallas_call(
        kern,
        grid=(B, ns),
        in_specs=[pl.BlockSpec((1, tm, attn_w), lambda b, i: (b, i, 0)),
                  pl.BlockSpec((1, tm, lru_w), lambda b, i: (b, i, 0)),
                  pl.BlockSpec((1, tm, D), lambda b, i: (b, i, 0)),
                  pl.BlockSpec((attn_w + lru_w, D), lambda b, i: (0, 0)),
                  bvec,
                  pl.BlockSpec((1, D), lambda b, i: (0, 0)),
                  bvec, bvec,
                  pl.BlockSpec((n_exp, D), lambda b, i: (0, 0)),
                  pl.BlockSpec((n_exp, 1), lambda b, i: (0, 0))],
        out_specs=[pl.BlockSpec((1, tm, D), lambda b, i: (b, i, 0)),
                   pl.BlockSpec((tm * rows, LANES), lambda b, i: (b * ns + i, 0)),
                   pl.BlockSpec((1, TOP_K, tm), lambda b, i: (b, 0, i)),
                   pl.BlockSpec((1, TOP_K, tm), lambda b, i: (b, 0, i))],
        out_shape=[jax.ShapeDtypeStruct((B, S, D), F32),
                   jax.ShapeDtypeStruct((B * S * rows, LANES), F32),
                   jax.ShapeDtypeStruct((B, TOP_K, S), jnp.int32),
                   jax.ShapeDtypeStruct((B, TOP_K, S), F32)],
        compiler_params=pltpu.CompilerParams(dimension_semantics=("arbitrary", "arbitrary"),
                                             vmem_limit_bytes=VMEM_LIMIT),
        name="outproj",
    )(att, rec, x, wo_b, g1, n2g, sc2, sh2, wr_t, br)


def _sc_gather_rows(table, idx):
    n = idx.shape[0]
    workers = SC_CORES * SC_SUBCORES
    per_worker = n // workers
    assert n % (workers * SC_CHUNK) == 0, "index count must split evenly into per-subcore chunks"
    n_chunks = per_worker // SC_CHUNK
    mesh = plsc.VectorSubcoreMesh(core_axis_name="c", subcore_axis_name="s",
                                  num_cores=SC_CORES, num_subcores=SC_SUBCORES)

    @functools.partial(
        pl.kernel, mesh=mesh,
        out_type=jax.ShapeDtypeStruct((n,) + table.shape[1:], table.dtype),
        scratch_types=[pltpu.VMEM((SC_CHUNK,), jnp.int32),
                       pltpu.VMEM((SC_CHUNK,) + table.shape[1:], table.dtype),
                       pltpu.SemaphoreType.DMA],
        name="sc_gather_rows",
    )
    def gather(table_hbm, idx_hbm, out_hbm, idx_v, rows_v, sem):
        base = (lax.axis_index("s") * SC_CORES + lax.axis_index("c")) * per_worker

        @pl.loop(0, n_chunks)
        def _(ci):
            off = pl.multiple_of(base + ci * SC_CHUNK, SC_CHUNK)
            pltpu.sync_copy(idx_hbm.at[pl.ds(off, SC_CHUNK)], idx_v)
            pltpu.async_copy(table_hbm.at[idx_v], rows_v, sem).wait()
            pltpu.sync_copy(rows_v, out_hbm.at[pl.ds(off, SC_CHUNK)])

    return gather(table, idx)


def _moe_kernel(be_ref, nused_ref, x_ref, g_ref, w1_ref, b1_ref, w2_ref, b2_ref, y_ref, *, d_ff):
    i = pl.program_id(0)
    rows = SUBLANES

    @pl.when(i < nused_ref[0])
    def _():
        x = jnp.concatenate([x_ref[pl.ds(c, EB, stride=rows), :] for c in range(rows)], axis=1)
        hu = _dot(x.astype(BF16), w1_ref[0]) + b1_ref[0]
        glu = jnp.minimum(hu[:, :d_ff], SWIGLU_LIMIT)
        lin = jnp.clip(hu[:, d_ff:], -SWIGLU_LIMIT, SWIGLU_LIMIT)
        act = glu * jax.nn.sigmoid(SWIGLU_ALPHA * glu) * (lin + 1.0)
        y = (_dot(act.astype(BF16), w2_ref[0]) + b2_ref[0]) * g_ref[...]
        for c in range(rows):
            y_ref[pl.ds(c, EB, stride=rows), :] = y[:, c * LANES:(c + 1) * LANES]

    @pl.when(i >= nused_ref[0])
    def _():
        y_ref[...] = jnp.zeros_like(y_ref)


def _moe(block_e, n_used, xs2d, row_g, w1b, b1, w2b, b2):
    n_blocks = block_e.shape[0]
    n_exp, D, ff2 = w1b.shape
    d_ff = ff2 // 2
    rows = D // LANES
    assert rows == SUBLANES, "a token row must fill exactly one (8,128) f32 tile"
    kern = functools.partial(_moe_kernel, d_ff=d_ff)
    grid_spec = pltpu.PrefetchScalarGridSpec(
        num_scalar_prefetch=2,
        grid=(n_blocks,),
        in_specs=[pl.BlockSpec((EB * rows, LANES), lambda i, be, nu: (i, 0)),
                  pl.BlockSpec((EB, 1), lambda i, be, nu: (i, 0)),
                  pl.BlockSpec((1, D, ff2), lambda i, be, nu: (be[i], 0, 0)),
                  pl.BlockSpec((1, 1, ff2), lambda i, be, nu: (be[i], 0, 0)),
                  pl.BlockSpec((1, d_ff, D), lambda i, be, nu: (be[i], 0, 0)),
                  pl.BlockSpec((1, 1, D), lambda i, be, nu: (be[i], 0, 0))],
        out_specs=pl.BlockSpec((EB * rows, LANES), lambda i, be, nu: (i, 0)),
    )
    return pl.pallas_call(
        kern,
        grid_spec=grid_spec,
        out_shape=jax.ShapeDtypeStruct((n_blocks * EB * rows, LANES), F32),
        compiler_params=pltpu.CompilerParams(dimension_semantics=("arbitrary",),
                                             vmem_limit_bytes=VMEM_LIMIT),
        name="moe",
    )(block_e, n_used, xs2d, row_g, w1b, b1, w2b, b2)


def _combine_kernel(y_ref, x1_ref, g2_ref, o_ref, s_ref):
    tm = x1_ref.shape[1]
    s_ref[...] = (y_ref[0] + y_ref[1]) + (y_ref[2] + y_ref[3])
    moe = jnp.concatenate([s_ref[pl.ds(c, tm, stride=SUBLANES), :] for c in range(SUBLANES)], axis=1)
    o_ref[0] = x1_ref[0] + g2_ref[0] * moe


def _combine(y4, x1, g2):
    B, S, D = x1.shape
    rows = D // LANES
    tm = min(TM_COMB, S)
    ns = S // tm
    return pl.pallas_call(
        _combine_kernel,
        grid=(B, ns),
        in_specs=[pl.BlockSpec((TOP_K, tm * rows, LANES), lambda b, i: (0, b * ns + i, 0)),
                  pl.BlockSpec((1, tm, D), lambda b, i: (b, i, 0)),
                  pl.BlockSpec((1, 1, D), lambda b, i: (b, 0, 0))],
        out_specs=pl.BlockSpec((1, tm, D), lambda b, i: (b, i, 0)),
        out_shape=jax.ShapeDtypeStruct((B, S, D), F32),
        scratch_shapes=[pltpu.VMEM((tm * rows, LANES), F32)],
        compiler_params=pltpu.CompilerParams(dimension_semantics=("arbitrary", "arbitrary"),
                                             vmem_limit_bytes=VMEM_LIMIT),
        name="combine",
    )(y4.reshape(TOP_K, B * S * rows, LANES), x1, g2)


def _route(top_idx, gates, n_exp):
    B, K, S = top_idx.shape
    T = B * S
    A = T * K
    flat_e = top_idx.transpose(0, 2, 1).reshape(A)
    flat_g = gates.transpose(0, 2, 1).reshape(A)
    experts = jnp.arange(n_exp, dtype=jnp.int32)
    csum = jnp.cumsum((flat_e[:, None] == experts[None, :]).astype(jnp.int32), axis=0)
    rank = jnp.take_along_axis(csum, flat_e[:, None], axis=1)[:, 0] - 1
    counts = csum[-1]
    nblk = (counts + EB - 1) // EB
    bend = jnp.cumsum(nblk)
    bstart = bend - nblk
    ustart = jnp.cumsum(counts) - counts
    n_blocks = -(-A // EB) + n_exp
    R = n_blocks * EB
    blk = jnp.arange(n_blocks, dtype=jnp.int32)
    block_e = jnp.minimum(jnp.sum(blk[:, None] >= bend[None, :], axis=1), n_exp - 1).astype(jnp.int32)
    row_of_a = bstart[flat_e] * EB + rank
    pos = row_of_a.reshape(T, K).T.reshape(A).astype(jnp.int32)
    order = jnp.argsort(flat_e, stable=True).astype(jnp.int32)
    off = (blk - bstart[block_e]) * EB
    nv = jnp.where(blk < bend[-1], jnp.clip(counts[block_e] - off, 0, EB), 0)
    r = jnp.arange(EB, dtype=jnp.int32)
    j = (ustart[block_e] + off)[:, None] + r[None, :]
    valid = (r[None, :] < nv[:, None]).reshape(R)
    row_a = order[jnp.clip(j, 0, A - 1).reshape(R)]
    row_tok = jnp.where(valid, row_a // K, jnp.arange(R, dtype=jnp.int32) % T).astype(jnp.int32)
    row_g = jnp.where(valid, flat_g[row_a], 0.0).reshape(R, 1)
    return block_e, bend[-1:].astype(jnp.int32), row_tok, row_g, pos


def _lambda_init(layer_idx):
    return 0.8 - 0.6 * math.exp(-0.3 * layer_idx)


def _block_diag(w):
    n, c, d = w.shape
    eye = jnp.eye(n, dtype=w.dtype)
    return (eye[:, None, :, None] * w[:, :, None, :]).reshape(n * c, n * d)


def kernel(x, c, w_ada, b_ada, norm1_g, w_in, q_norm_g, k_norm_g, lambda_q1, lambda_k1, lambda_q2, lambda_k2, subln_g, conv_w, conv_b, w_rg_a, b_rg_a, w_rg_x, b_rg_x, lru_lambda, w_out, norm2_g, w_router, b_router, w_moe1, b_moe1, w_moe2, b_moe2):
    B, S, D = x.shape
    depth = w_ada.shape[0]
    lru_w = conv_w.shape[2]
    attn_w = (w_in.shape[2] - 2 * lru_w) // 3
    n_exp = w_router.shape[2]
    n_heads = attn_w // HEAD_DIM
    assert attn_w % LANES == 0 and S % CHUNK == 0

    half = HEAD_DIM // 2
    inv = ROPE_THETA ** (-jnp.arange(half, dtype=F32) / half)
    ang = jnp.arange(S, dtype=jnp.int32).astype(F32)[:, None] * inv[None, :]
    cos_t = jnp.tile(jnp.concatenate([jnp.cos(ang), jnp.cos(ang)], axis=1), (1, n_heads))
    sin_t = jnp.tile(jnp.concatenate([-jnp.sin(ang), jnp.sin(ang)], axis=1), (1, n_heads))
    group = jnp.arange(attn_w, dtype=jnp.int32) // HEAD_DIM
    gsum = (group[:, None] == group[None, :]).astype(BF16)

    for l in range(depth):
        mod = _ada(c, w_ada[l], b_ada[l])
        sh1, sc1, g1, sh2, sc2, g2 = [m.reshape(B, 1, D) for m in jnp.split(mod, 6, axis=-1)]

        qt, k, vt, xr, gr = _inproj(
            x, norm1_g[l].reshape(1, D), sc1, sh1, w_in[l].astype(BF16), gsum,
            jnp.tile(q_norm_g[l], n_heads).reshape(1, attn_w), jnp.tile(k_norm_g[l], n_heads).reshape(1, attn_w),
            cos_t, sin_t, attn_w, lru_w)

        score_bound = (HEAD_DIM ** 0.5 * LOG2_E * BF16_SLACK) * jnp.max(jnp.abs(q_norm_g[l])) * jnp.max(jnp.abs(k_norm_g[l]))
        bounded = (score_bound <= SCORE_BOUND).astype(jnp.int32).reshape(1)
        att = _attn(bounded, qt, k, vt, lambda_q1[l].reshape(1, -1), lambda_k1[l].reshape(1, -1),
                    lambda_q2[l].reshape(1, -1), lambda_k2[l].reshape(1, -1),
                    subln_g[l].reshape(-1, 1), _lambda_init(l))

        rec = _lru(xr, gr, conv_w[l], conv_b[l].reshape(1, lru_w),
                   _block_diag(w_rg_a[l]).astype(BF16), b_rg_a[l].reshape(1, lru_w),
                   _block_diag(w_rg_x[l]).astype(BF16), b_rg_x[l].reshape(1, lru_w),
                   lru_lambda[l].reshape(1, lru_w))

        x1, hm2d, top_idx, gates = _outproj(
            att, rec, x, w_out[l].astype(BF16), g1, norm2_g[l].reshape(1, D), sc2, sh2,
            w_router[l].T, b_router[l].reshape(n_exp, 1), n_exp)

        block_e, n_used, row_tok, row_g, pos = _route(top_idx, gates, n_exp)
        rows = D // LANES
        xs = _sc_gather_rows(hm2d.reshape(B * S, rows, LANES), row_tok)
        ys = _moe(block_e, n_used, xs.reshape(-1, LANES), row_g,
                  w_moe1[l].astype(BF16), b_moe1[l].reshape(n_exp, 1, -1),
                  w_moe2[l].astype(BF16), b_moe2[l].reshape(n_exp, 1, -1))
        y4 = _sc_gather_rows(ys.reshape(-1, rows, LANES), pos)
        x = _combine(y4, x1, g2)
    return x
```

```python
import functools
import math

import jax
import jax.numpy as jnp
from jax import lax
from jax.experimental import pallas as pl
from jax.experimental.pallas import tpu as pltpu
from jax.experimental.pallas import tpu_sc as plsc

F32 = jnp.float32
BF16 = jnp.bfloat16

CHUNK = 64
HEAD_DIM = 64
ROPE_THETA = 10000.0
LRU_BLOCK = 64
CONV_W = 4
LRU_C = 8.0
TOP_K = 4
SWIGLU_ALPHA = 1.702
SWIGLU_LIMIT = 7.0
NORM_EPS = 1e-6
SUBLN_EPS = 1e-5
LOG2_E = math.log2(math.e)
SCORE_BOUND = 60.0
BF16_SLACK = 1.02

LANES = 128
SUBLANES = 8
VMEM_LIMIT = 56 * 1024 * 1024
SC_CORES = 2
SC_SUBCORES = 16
SC_CHUNK = 32

TM_PROJ = 512
TQ = 512
TK = 512
TT = 256
EB = 512
TM_COMB = 256


def _split_bf16(a):
    hi = a.astype(BF16)
    lo = (a - hi.astype(F32)).astype(BF16)
    return hi, lo


def _dot(a, b):
    return jnp.dot(a, b, preferred_element_type=F32)


def _dot3(a, b):
    ah, al = _split_bf16(a)
    bh, bl = _split_bf16(b)
    return _dot(ah, bh) + (_dot(ah, bl) + _dot(al, bh))


def _ada_kernel(c_ref, w_ref, b_ref, o_ref):
    c = c_ref[...]
    s = c * jax.nn.sigmoid(c)
    o_ref[...] = _dot3(s, w_ref[...]) + b_ref[...]


def _ada(c, w_ada, b_ada):
    B, D = c.shape
    N = w_ada.shape[1]
    tn = D
    return pl.pallas_call(
        _ada_kernel,
        grid=(N // tn,),
        in_specs=[pl.BlockSpec((B, D), lambda j: (0, 0)),
                  pl.BlockSpec((D, tn), lambda j: (0, j)),
                  pl.BlockSpec((1, tn), lambda j: (0, j))],
        out_specs=pl.BlockSpec((B, tn), lambda j: (0, j)),
        out_shape=jax.ShapeDtypeStruct((B, N), F32),
        compiler_params=pltpu.CompilerParams(dimension_semantics=("arbitrary",),
                                             vmem_limit_bytes=VMEM_LIMIT),
        name="ada",
    )(c, w_ada, b_ada.reshape(1, N))


def _inproj_kernel(x_ref, g_ref, sc_ref, sh_ref, w_ref, gsum_ref, qg_ref, kg_ref, cos_ref, sin_ref,
                   qt_ref, k_ref, vt_ref, xr_ref, gr_ref, *, attn_w, lru_w):
    x = x_ref[0]
    ms = jnp.mean(x * x, axis=-1, keepdims=True)
    hn = x * lax.rsqrt(ms + NORM_EPS) * g_ref[...]
    hn = (hn * (1.0 + sc_ref[0]) + sh_ref[0]).astype(BF16)

    cos = cos_ref[...]
    sin = sin_ref[...]
    gsum = gsum_ref[...]
    lane = lax.broadcasted_iota(jnp.int32, cos.shape, 1)
    first_half = (lane % HEAD_DIM) < (HEAD_DIM // 2)

    def head_norm_rope(t, gain):
        hi, lo = _split_bf16(t * t)
        ssum = _dot(hi, gsum) + _dot(lo, gsum)
        t = t * lax.rsqrt(ssum * (1.0 / HEAD_DIM) + NORM_EPS) * gain
        half = HEAD_DIM // 2
        partner = jnp.where(first_half, pltpu.roll(t, attn_w - half, 1), pltpu.roll(t, half, 1))
        return t * cos + partner * sin

    q = head_norm_rope(_dot(hn, w_ref[:, 0:attn_w]), qg_ref[...]) * (HEAD_DIM ** -0.5 * LOG2_E)
    for h in range(attn_w // LANES):
        qt_ref[0, h] = q[:, h * LANES:(h + 1) * LANES].T.astype(BF16)

    k = head_norm_rope(_dot(hn, w_ref[:, attn_w:2 * attn_w]), kg_ref[...])
    k_ref[0] = k.astype(BF16)

    v = _dot(hn, w_ref[:, 2 * attn_w:3 * attn_w])
    for h in range(attn_w // LANES):
        vt_ref[0, h] = v[:, h * LANES:(h + 1) * LANES].T.astype(BF16)
    xr_ref[0] = _dot(hn, w_ref[:, 3 * attn_w:3 * attn_w + lru_w])
    gr_ref[0] = _dot(hn, w_ref[:, 3 * attn_w + lru_w:3 * attn_w + 2 * lru_w])


def _inproj(x, g1n, sc1, sh1, w_in_b, gsum, qg, kg, cos_t, sin_t, attn_w, lru_w):
    B, S, D = x.shape
    tm = min(TM_PROJ, S)
    in_w = w_in_b.shape[1]
    nh = attn_w // LANES
    kern = functools.partial(_inproj_kernel, attn_w=attn_w, lru_w=lru_w)
    return pl.pallas_call(
        kern,
        grid=(S // tm, B),
        in_specs=[pl.BlockSpec((1, tm, D), lambda i, b: (b, i, 0)),
                  pl.BlockSpec((1, D), lambda i, b: (0, 0)),
                  pl.BlockSpec((1, 1, D), lambda i, b: (b, 0, 0)),
                  pl.BlockSpec((1, 1, D), lambda i, b: (b, 0, 0)),
                  pl.BlockSpec((D, in_w), lambda i, b: (0, 0)),
                  pl.BlockSpec((attn_w, attn_w), lambda i, b: (0, 0)),
                  pl.BlockSpec((1, attn_w), lambda i, b: (0, 0)),
                  pl.BlockSpec((1, attn_w), lambda i, b: (0, 0)),
                  pl.BlockSpec((tm, attn_w), lambda i, b: (i, 0)),
                  pl.BlockSpec((tm, attn_w), lambda i, b: (i, 0))],
        out_specs=[pl.BlockSpec((1, nh, LANES, tm), lambda i, b: (b, 0, 0, i)),
                   pl.BlockSpec((1, tm, attn_w), lambda i, b: (b, i, 0)),
                   pl.BlockSpec((1, nh, LANES, tm), lambda i, b: (b, 0, 0, i)),
                   pl.BlockSpec((1, tm, lru_w), lambda i, b: (b, i, 0)),
                   pl.BlockSpec((1, tm, lru_w), lambda i, b: (b, i, 0))],
        out_shape=[jax.ShapeDtypeStruct((B, nh, LANES, S), BF16),
                   jax.ShapeDtypeStruct((B, S, attn_w), BF16),
                   jax.ShapeDtypeStruct((B, nh, LANES, S), BF16),
                   jax.ShapeDtypeStruct((B, S, lru_w), F32),
                   jax.ShapeDtypeStruct((B, S, lru_w), F32)],
        compiler_params=pltpu.CompilerParams(dimension_semantics=("arbitrary", "arbitrary"),
                                             vmem_limit_bytes=VMEM_LIMIT),
        name="inproj",
    )(x, g1n, sc1, sh1, w_in_b, gsum, qg, kg, cos_t, sin_t)


def _attn_kernel(bounded_ref, qt_ref, k_ref, vt_ref, lq1_ref, lk1_ref, lq2_ref, lk2_ref, sg_ref, o_ref,
                 *, tq, tk, lam_init):
    qi = pl.program_id(2)
    qt = qt_ref[0, 0]
    row = lax.broadcasted_iota(jnp.int32, qt.shape, 0)
    zero = jnp.zeros_like(qt)
    qs = jnp.concatenate([jnp.where(row < HEAD_DIM, qt, zero),
                          jnp.where(row >= HEAD_DIM, qt, zero)], axis=1)

    def scores(j):
        kk = k_ref[0, pl.ds(pl.multiple_of(j * tk, tk), tk), :]
        return _dot(kk, qs)

    def values(j, p):
        vt = vt_ref[0, 0, :, pl.ds(pl.multiple_of(j * tk, tk), tk)]
        return _dot(vt, p)

    n_full = (qi * tq) // tk

    def visible(shape):
        key = lax.broadcasted_iota(jnp.int32, shape, 0) + n_full * tk
        qry = lax.broadcasted_iota(jnp.int32, shape, 1) % tq + qi * tq
        return (key // CHUNK) <= (qry // CHUNK)

    zeros = (jnp.zeros((1, 2 * tq), F32), jnp.zeros((LANES, 2 * tq), F32))

    def bounded_path():
        def body(j, carry):
            l, acc = carry
            p = jnp.exp2(scores(j))
            return l + jnp.sum(p, axis=0, keepdims=True), acc + values(j, p.astype(BF16))

        l, acc = lax.fori_loop(0, n_full, body, zeros)
        s = scores(n_full)
        p = jnp.where(visible(s.shape), jnp.exp2(s), 0.0)
        return l + jnp.sum(p, axis=0, keepdims=True), acc + values(n_full, p.astype(BF16))

    def online_path():
        def update(j, carry, masked):
            m, l, acc = carry
            s = scores(j)
            if masked:
                s = jnp.where(visible(s.shape), s, -jnp.inf)
            m_new = jnp.maximum(m, jnp.max(s, axis=0, keepdims=True))
            alpha = jnp.exp2(m - m_new)
            p = jnp.exp2(s - m_new)
            l = alpha * l + jnp.sum(p, axis=0, keepdims=True)
            return m_new, l, alpha * acc + values(j, p.astype(BF16))

        carry = (jnp.full((1, 2 * tq), -jnp.inf, F32),) + zeros
        carry = lax.fori_loop(0, n_full, lambda j, c: update(j, c, False), carry)
        _, l, acc = update(n_full, carry, True)
        return l, acc

    l, acc = lax.cond(bounded_ref[0] != 0, bounded_path, online_path)

    lam = (jnp.exp(jnp.sum(lq1_ref[...] * lk1_ref[...], axis=-1, keepdims=True))
           - jnp.exp(jnp.sum(lq2_ref[...] * lk2_ref[...], axis=-1, keepdims=True)) + lam_init)
    o = acc * (1.0 / l)
    o = o[:, :tq] - lam * o[:, tq:]
    ms = jnp.mean(o * o, axis=0, keepdims=True)
    o = o * lax.rsqrt(ms + SUBLN_EPS) * (sg_ref[...] * (1.0 - lam_init))
    o_ref[0] = o.T.astype(BF16)


def _attn(bounded, qt, k, vt, lq1, lk1, lq2, lk2, sg, lam_init):
    B, S, attn_w = k.shape
    nh = attn_w // LANES
    tq = min(TQ, S)
    tk = min(TK, S)
    assert tk % tq == 0, "one key tile must cover the whole query block (single masked tile)"
    kern = functools.partial(_attn_kernel, tq=tq, tk=tk, lam_init=lam_init)
    vec = lambda n: pl.BlockSpec((1, n), lambda b, h, i: (0, 0))
    return pl.pallas_call(
        kern,
        grid=(B, nh, S // tq),
        in_specs=[pl.BlockSpec(memory_space=pltpu.SMEM),
                  pl.BlockSpec((1, 1, LANES, tq), lambda b, h, i: (b, h, 0, i)),
                  pl.BlockSpec((1, S, LANES), lambda b, h, i: (b, 0, h)),
                  pl.BlockSpec((1, 1, LANES, S), lambda b, h, i: (b, h, 0, 0)),
                  vec(HEAD_DIM), vec(HEAD_DIM), vec(HEAD_DIM), vec(HEAD_DIM),
                  pl.BlockSpec((LANES, 1), lambda b, h, i: (0, 0))],
        out_specs=pl.BlockSpec((1, tq, LANES), lambda b, h, i: (b, i, h)),
        out_shape=jax.ShapeDtypeStruct((B, S, attn_w), BF16),
        compiler_params=pltpu.CompilerParams(
            dimension_semantics=("arbitrary", "arbitrary", "arbitrary"), vmem_limit_bytes=VMEM_LIMIT),
        name="attn",
    )(bounded, qt, k, vt, lq1, lk1, lq2, lk2, sg)


def _shift_rows(a, d, fill):
    row = lax.broadcasted_iota(jnp.int32, a.shape, 0)
    return jnp.where(row < d, fill, pltpu.roll(a, d, 0))


def _lru_kernel(xr_ref, gr_ref, cw_ref, cb_ref, wa_ref, ba_ref, wx_ref, bx_ref, lam_ref, o_ref,
                tail_ref, h_ref, *, tt):
    @pl.when(pl.program_id(1) == 0)
    def _():
        tail_ref[...] = jnp.zeros_like(tail_ref)
        h_ref[...] = jnp.zeros_like(h_ref)

    x = xr_ref[0]
    ext = jnp.concatenate([tail_ref[...], x], axis=0)
    xc = cb_ref[...] + cw_ref[CONV_W - 1:CONV_W, :] * x
    for j in range(CONV_W - 1):
        back = CONV_W - 1 - j
        xc = xc + cw_ref[j:j + 1, :] * ext[SUBLANES - back:SUBLANES - back + tt]
    tail_ref[...] = x[tt - SUBLANES:]

    xcb = xc.astype(BF16)
    r = jax.nn.sigmoid(_dot(xcb, wa_ref[...]) + ba_ref[...])
    i = jax.nn.sigmoid(_dot(xcb, wx_ref[...]) + bx_ref[...])
    nl = -lam_ref[...]
    softplus = jnp.maximum(nl, 0.0) + jnp.log(1.0 + jnp.exp(-jnp.abs(nl)))
    log_a = (-LRU_C) * r * softplus
    a = jnp.exp(log_a)
    th = jnp.tanh(log_a)
    u = jnp.sqrt((-2.0) * th / (1.0 - th)) * (i * xc)

    d = 1
    while d < tt:
        u = u + a * _shift_rows(u, d, 0.0)
        a = a * _shift_rows(a, d, 1.0)
        d *= 2
    h = a * h_ref[0:1, :] + u
    h_ref[...] = jnp.broadcast_to(h[tt - 1:tt, :], h_ref.shape)

    g = gr_ref[0]
    gelu = 0.5 * g * (1.0 + jnp.tanh(math.sqrt(2.0 / math.pi) * (g + 0.044715 * (g * g * g))))
    o_ref[0] = (h * gelu).astype(BF16)


def _lru(xr, gr, conv_w, conv_b, wa_bd, b_a, wx_bd, b_x, lam):
    B, S, W = xr.shape
    tt = min(TT, S)
    kern = functools.partial(_lru_kernel, tt=tt)
    vec = pl.BlockSpec((1, W), lambda b, i: (0, 0))
    mat = pl.BlockSpec((W, W), lambda b, i: (0, 0))
    return pl.pallas_call(
        kern,
        grid=(B, S // tt),
        in_specs=[pl.BlockSpec((1, tt, W), lambda b, i: (b, i, 0)),
                  pl.BlockSpec((1, tt, W), lambda b, i: (b, i, 0)),
                  pl.BlockSpec((CONV_W, W), lambda b, i: (0, 0)),
                  vec, mat, vec, mat, vec, vec],
        out_specs=pl.BlockSpec((1, tt, W), lambda b, i: (b, i, 0)),
        out_shape=jax.ShapeDtypeStruct((B, S, W), BF16),
        scratch_shapes=[pltpu.VMEM((SUBLANES, W), F32), pltpu.VMEM((SUBLANES, W), F32)],
        compiler_params=pltpu.CompilerParams(dimension_semantics=("arbitrary", "arbitrary"),
                                             vmem_limit_bytes=VMEM_LIMIT),
        name="lru",
    )(xr, gr, conv_w, conv_b, wa_bd, b_a, wx_bd, b_x, lam)


def _outproj_kernel(att_ref, rec_ref, x_ref, wo_ref, g1_ref, n2_ref, sc_ref, sh_ref, wr_ref, br_ref,
                    x1_ref, hm_ref, idx_ref, gate_ref, rank_ref, count_ref, cnt_ref, *, attn_w, n_exp):
    tm = x_ref.shape[1]
    mix = _dot(att_ref[0], wo_ref[0:attn_w, :]) + _dot(rec_ref[0], wo_ref[attn_w:, :])
    x1 = x_ref[0] + g1_ref[0] * mix
    x1_ref[0] = x1
    ms = jnp.mean(x1 * x1, axis=-1, keepdims=True)
    hm = x1 * lax.rsqrt(ms + NORM_EPS) * n2_ref[...]
    hm = hm * (1.0 + sc_ref[0]) + sh_ref[0]
    for c in range(hm.shape[1] // LANES):
        hm_ref[pl.ds(c, tm, stride=SUBLANES), :] = hm[:, c * LANES:(c + 1) * LANES]

    hh, hl = _split_bf16(hm)
    wh, wl = _split_bf16(wr_ref[...])
    nt = (((1,), (1,)), ((), ()))
    logits = (lax.dot_general(wh, hh, nt, preferred_element_type=F32)
              + (lax.dot_general(wh, hl, nt, preferred_element_type=F32)
                 + lax.dot_general(wl, hh, nt, preferred_element_type=F32))) + br_ref[...]
    eidx = lax.broadcasted_iota(jnp.int32, logits.shape, 0)
    vals, idxs = [], []
    for _ in range(TOP_K):
        m = jnp.max(logits, axis=0, keepdims=True)
        am = jnp.min(jnp.where(logits == m, eidx, n_exp), axis=0, keepdims=True)
        vals.append(m)
        idxs.append(am)
        logits = jnp.where(eidx == am, -jnp.inf, logits)
    ex = [jnp.exp(v - vals[0]) for v in vals]
    den = ex[0] + ex[1] + ex[2] + ex[3]
    idx_ref[0] = jnp.concatenate(idxs, axis=0)
    gate_ref[0] = jnp.concatenate([e / den for e in ex], axis=0)

    @pl.when((pl.program_id(0) == 0) & (pl.program_id(1) == 0))
    def _():
        cnt_ref[...] = jnp.zeros_like(cnt_ref)

    hits = [eidx == am for am in idxs]
    sel = (hits[0] | hits[1] | hits[2] | hits[3]).astype(F32)
    earlier = (lax.broadcasted_iota(jnp.int32, (tm, tm), 0)
               < lax.broadcasted_iota(jnp.int32, (tm, tm), 1)).astype(BF16)
    before = _dot(sel.astype(BF16), earlier) + cnt_ref[...]
    rank_ref[0] = jnp.concatenate(
        [jnp.sum(jnp.where(h, before, 0.0), axis=0, keepdims=True) for h in hits], axis=0).astype(jnp.int32)
    cnt_ref[...] += jnp.sum(sel, axis=1, keepdims=True)
    count_ref[...] = cnt_ref[...].astype(jnp.int32)


def _outproj(att, rec, x, wo_b, g1, n2g, sc2, sh2, wr_t, br, n_exp):
    B, S, D = x.shape
    attn_w = att.shape[2]
    lru_w = rec.shape[2]
    tm = min(TM_PROJ, S)
    ns = S // tm
    rows = D // LANES
    kern = functools.partial(_outproj_kernel, attn_w=attn_w, n_exp=n_exp)
    bvec = pl.BlockSpec((1, 1, D), lambda b, i: (b, 0, 0))
    return pl.pallas_call(
        kern,
        grid=(B, ns),
        in_specs=[pl.BlockSpec((1, tm, attn_w), lambda b, i: (b, i, 0)),
                  pl.BlockSpec((1, tm, lru_w), lambda b, i: (b, i, 0)),
                  pl.BlockSpec((1, tm, D), lambda b, i: (b, i, 0)),
                  pl.BlockSpec((attn_w + lru_w, D), lambda b, i: (0, 0)),
                  bvec,
                  pl.BlockSpec((1, D), lambda b, i: (0, 0)),
                  bvec, bvec,
                  pl.BlockSpec((n_exp, D), lambda b, i: (0, 0)),
                  pl.BlockSpec((n_exp, 1), lambda b, i: (0, 0))],
        out_specs=[pl.BlockSpec((1, tm, D), lambda b, i: (b, i, 0)),
                   pl.BlockSpec((tm * rows, LANES), lambda b, i: (b * ns + i, 0)),
                   pl.BlockSpec((1, TOP_K, tm), lambda b, i: (b, 0, i)),
                   pl.BlockSpec((1, TOP_K, tm), lambda b, i: (b, 0, i)),
                   pl.BlockSpec((1, TOP_K, tm), lambda b, i: (b, 0, i)),
                   pl.BlockSpec((n_exp, 1), lambda b, i: (0, 0))],
        out_shape=[jax.ShapeDtypeStruct((B, S, D), F32),
                   jax.ShapeDtypeStruct((B * S * rows, LANES), F32),
                   jax.ShapeDtypeStruct((B, TOP_K, S), jnp.int32),
                   jax.ShapeDtypeStruct((B, TOP_K, S), F32),
                   jax.ShapeDtypeStruct((B, TOP_K, S), jnp.int32),
                   jax.ShapeDtypeStruct((n_exp, 1), jnp.int32)],
        scratch_shapes=[pltpu.VMEM((n_exp, 1), F32)],
        compiler_params=pltpu.CompilerParams(dimension_semantics=("arbitrary", "arbitrary"),
                                             vmem_limit_bytes=VMEM_LIMIT),
        name="outproj",
    )(att, rec, x, wo_b, g1, n2g, sc2, sh2, wr_t, br)


def _sc_gather_rows(table, idx):
    n = idx.shape[0]
    workers = SC_CORES * SC_SUBCORES
    per_worker = n // workers
    assert n % (workers * SC_CHUNK) == 0, "index count must split evenly into per-subcore chunks"
    n_chunks = per_worker // SC_CHUNK
    mesh = plsc.VectorSubcoreMesh(core_axis_name="c", subcore_axis_name="s",
                                  num_cores=SC_CORES, num_subcores=SC_SUBCORES)

    @functools.partial(
        pl.kernel, mesh=mesh,
        out_type=jax.ShapeDtypeStruct((n,) + table.shape[1:], table.dtype),
        scratch_types=[pltpu.VMEM((SC_CHUNK,), jnp.int32),
                       pltpu.VMEM((SC_CHUNK,) + table.shape[1:], table.dtype),
                       pltpu.SemaphoreType.DMA],
        name="sc_gather_rows",
    )
    def gather(table_hbm, idx_hbm, out_hbm, idx_v, rows_v, sem):
        base = (lax.axis_index("s") * SC_CORES + lax.axis_index("c")) * per_worker

        @pl.loop(0, n_chunks)
        def _(ci):
            off = pl.multiple_of(base + ci * SC_CHUNK, SC_CHUNK)
            pltpu.sync_copy(idx_hbm.at[pl.ds(off, SC_CHUNK)], idx_v)
            pltpu.async_copy(table_hbm.at[idx_v], rows_v, sem).wait()
            pltpu.sync_copy(rows_v, out_hbm.at[pl.ds(off, SC_CHUNK)])

    return gather(table, idx)


def _sc_scatter_rows(src, pos, n_out):
    n_tok = src.shape[0]
    n_k = pos.shape[1]
    workers = SC_CORES * SC_SUBCORES
    per_worker = n_tok // workers
    assert n_tok % (workers * SC_CHUNK) == 0, "token count must split evenly into per-subcore chunks"
    n_chunks = per_worker // SC_CHUNK
    mesh = plsc.VectorSubcoreMesh(core_axis_name="c", subcore_axis_name="s",
                                  num_cores=SC_CORES, num_subcores=SC_SUBCORES)

    @functools.partial(
        pl.kernel, mesh=mesh,
        out_type=jax.ShapeDtypeStruct((n_out,) + src.shape[1:], src.dtype),
        scratch_types=[pltpu.VMEM((n_k, SC_CHUNK), jnp.int32),
                       pltpu.VMEM((SC_CHUNK,) + src.shape[1:], src.dtype),
                       pltpu.SemaphoreType.DMA],
        name="sc_scatter_rows",
    )
    def scatter(src_hbm, pos_hbm, out_hbm, idx_v, rows_v, sem):
        base = (lax.axis_index("s") * SC_CORES + lax.axis_index("c")) * n_chunks

        @pl.loop(0, n_chunks)
        def _(ci):
            chunk = base + ci
            pltpu.sync_copy(pos_hbm.at[chunk], idx_v)
            pltpu.sync_copy(src_hbm.at[pl.ds(pl.multiple_of(chunk * SC_CHUNK, SC_CHUNK), SC_CHUNK)], rows_v)
            copies = [pltpu.async_copy(rows_v, out_hbm.at[idx_v.at[k]], sem) for k in range(n_k)]
            for cp in copies:
                cp.wait()

    return scatter(src, pos)


def _moe_kernel(be_ref, nv_ref, nused_ref, x_ref, w1_ref, b1_ref, w2_ref, b2_ref, y_ref, w1b_ref, w2b_ref,
                *, d_ff):
    i = pl.program_id(0)
    rows = SUBLANES

    @pl.when(i < nused_ref[0])
    def _():
        @pl.when((i == 0) | (be_ref[i] != be_ref[jnp.maximum(i - 1, 0)]))
        def _():
            w1b_ref[...] = w1_ref[0].astype(BF16)
            w2b_ref[...] = w2_ref[0].astype(BF16)

        x = jnp.concatenate([x_ref[pl.ds(c, EB, stride=rows), :] for c in range(rows)], axis=1)
        valid = lax.broadcasted_iota(jnp.int32, (EB, 1), 0) < nv_ref[i]
        x = jnp.where(valid, x, 0.0).astype(BF16)
        hu = _dot(x, w1b_ref[...]) + b1_ref[0]
        glu = jnp.minimum(hu[:, :d_ff], SWIGLU_LIMIT)
        lin = jnp.clip(hu[:, d_ff:], -SWIGLU_LIMIT, SWIGLU_LIMIT)
        act = glu * jax.nn.sigmoid(SWIGLU_ALPHA * glu) * (lin + 1.0)
        y = _dot(act.astype(BF16), w2b_ref[...]) + b2_ref[0]
        for c in range(rows):
            y_ref[pl.ds(c, EB, stride=rows), :] = y[:, c * LANES:(c + 1) * LANES]

    @pl.when(i >= nused_ref[0])
    def _():
        y_ref[...] = jnp.zeros_like(y_ref)


def _moe(block_e, nv, n_used, xs2d, w1, b1, w2, b2):
    n_blocks = block_e.shape[0]
    n_exp, D, ff2 = w1.shape
    d_ff = ff2 // 2
    rows = D // LANES
    assert rows == SUBLANES, "a token row must fill exactly one (8,128) f32 tile"
    kern = functools.partial(_moe_kernel, d_ff=d_ff)
    grid_spec = pltpu.PrefetchScalarGridSpec(
        num_scalar_prefetch=3,
        grid=(n_blocks,),
        in_specs=[pl.BlockSpec((EB * rows, LANES), lambda i, be, nv, nu: (i, 0)),
                  pl.BlockSpec((1, D, ff2), lambda i, be, nv, nu: (be[i], 0, 0)),
                  pl.BlockSpec((1, 1, ff2), lambda i, be, nv, nu: (be[i], 0, 0)),
                  pl.BlockSpec((1, d_ff, D), lambda i, be, nv, nu: (be[i], 0, 0)),
                  pl.BlockSpec((1, 1, D), lambda i, be, nv, nu: (be[i], 0, 0))],
        out_specs=pl.BlockSpec((EB * rows, LANES), lambda i, be, nv, nu: (i, 0)),
        scratch_shapes=[pltpu.VMEM((D, ff2), BF16), pltpu.VMEM((d_ff, D), BF16)],
    )
    return pl.pallas_call(
        kern,
        grid_spec=grid_spec,
        out_shape=jax.ShapeDtypeStruct((n_blocks * EB * rows, LANES), F32),
        compiler_params=pltpu.CompilerParams(dimension_semantics=("arbitrary",),
                                             vmem_limit_bytes=VMEM_LIMIT),
        name="moe",
    )(block_e, nv, n_used, xs2d, w1, b1, w2, b2)


def _combine_kernel(y_ref, gate_ref, x1_ref, g2_ref, o_ref):
    tm = x1_ref.shape[1]
    diag = (lax.broadcasted_iota(jnp.int32, (tm, tm), 0) == lax.broadcasted_iota(jnp.int32, (tm, tm), 1))
    moe = jnp.zeros(x1_ref.shape[1:], F32)
    for k in range(TOP_K):
        yk = jnp.concatenate([y_ref[k, pl.ds(c, tm, stride=SUBLANES), :] for c in range(SUBLANES)], axis=1)
        gate_col = jnp.sum(jnp.where(diag, gate_ref[0, k:k + 1, :], 0.0), axis=1, keepdims=True)
        moe = moe + gate_col * yk
    o_ref[0] = x1_ref[0] + g2_ref[0] * moe


def _combine(y4, gates, x1, g2):
    B, S, D = x1.shape
    rows = D // LANES
    tm = min(TM_COMB, S)
    ns = S // tm
    return pl.pallas_call(
        _combine_kernel,
        grid=(B, ns),
        in_specs=[pl.BlockSpec((TOP_K, tm * rows, LANES), lambda b, i: (0, b * ns + i, 0)),
                  pl.BlockSpec((1, TOP_K, tm), lambda b, i: (b, 0, i)),
                  pl.BlockSpec((1, tm, D), lambda b, i: (b, i, 0)),
                  pl.BlockSpec((1, 1, D), lambda b, i: (b, 0, 0))],
        out_specs=pl.BlockSpec((1, tm, D), lambda b, i: (b, i, 0)),
        out_shape=jax.ShapeDtypeStruct((B, S, D), F32),
        compiler_params=pltpu.CompilerParams(dimension_semantics=("arbitrary", "arbitrary"),
                                             vmem_limit_bytes=VMEM_LIMIT),
        name="combine",
    )(y4.reshape(TOP_K, B * S * rows, LANES), gates, x1, g2)


def _route(top_idx, rank, counts):
    B, K, S = top_idx.shape
    T = B * S
    n_exp = counts.shape[0]
    nblk = (counts + EB - 1) // EB
    bend = jnp.cumsum(nblk)
    bstart = bend - nblk
    n_blocks = -(-(T * K) // EB) + n_exp
    blk = jnp.arange(n_blocks, dtype=jnp.int32)
    block_e = jnp.minimum(jnp.sum(blk[:, None] >= bend[None, :], axis=1), n_exp - 1).astype(jnp.int32)
    nv = jnp.where(blk < bend[-1], jnp.clip(counts[block_e] - (blk - bstart[block_e]) * EB, 0, EB), 0)
    start_row = jnp.sum(jnp.where(top_idx[..., None] == jnp.arange(n_exp, dtype=jnp.int32),
                                  (bstart * EB).astype(jnp.int32), 0), axis=-1)
    pos = (start_row + rank).transpose(1, 0, 2).reshape(K, T).astype(jnp.int32)
    return block_e, nv.astype(jnp.int32), bend[-1:].astype(jnp.int32), pos


def _lambda_init(layer_idx):
    return 0.8 - 0.6 * math.exp(-0.3 * layer_idx)


def _block_diag(w):
    n, c, d = w.shape
    eye = jnp.eye(n, dtype=w.dtype)
    return (eye[:, None, :, None] * w[:, :, None, :]).reshape(n * c, n * d)


def kernel(x, c, w_ada, b_ada, norm1_g, w_in, q_norm_g, k_norm_g, lambda_q1, lambda_k1, lambda_q2, lambda_k2, subln_g, conv_w, conv_b, w_rg_a, b_rg_a, w_rg_x, b_rg_x, lru_lambda, w_out, norm2_g, w_router, b_router, w_moe1, b_moe1, w_moe2, b_moe2):
    B, S, D = x.shape
    depth = w_ada.shape[0]
    lru_w = conv_w.shape[2]
    attn_w = (w_in.shape[2] - 2 * lru_w) // 3
    n_exp = w_router.shape[2]
    n_heads = attn_w // HEAD_DIM
    assert attn_w % LANES == 0 and S % CHUNK == 0

    half = HEAD_DIM // 2
    inv = ROPE_THETA ** (-jnp.arange(half, dtype=F32) / half)
    ang = jnp.arange(S, dtype=jnp.int32).astype(F32)[:, None] * inv[None, :]
    cos_t = jnp.tile(jnp.concatenate([jnp.cos(ang), jnp.cos(ang)], axis=1), (1, n_heads))
    sin_t = jnp.tile(jnp.concatenate([-jnp.sin(ang), jnp.sin(ang)], axis=1), (1, n_heads))
    group = jnp.arange(attn_w, dtype=jnp.int32) // HEAD_DIM
    gsum = (group[:, None] == group[None, :]).astype(BF16)

    for l in range(depth):
        mod = _ada(c, w_ada[l], b_ada[l])
        sh1, sc1, g1, sh2, sc2, g2 = [m.reshape(B, 1, D) for m in jnp.split(mod, 6, axis=-1)]

        qt, k, vt, xr, gr = _inproj(
            x, norm1_g[l].reshape(1, D), sc1, sh1, w_in[l].astype(BF16), gsum,
            jnp.tile(q_norm_g[l], n_heads).reshape(1, attn_w), jnp.tile(k_norm_g[l], n_heads).reshape(1, attn_w),
            cos_t, sin_t, attn_w, lru_w)

        score_bound = (HEAD_DIM ** 0.5 * LOG2_E * BF16_SLACK) * jnp.max(jnp.abs(q_norm_g[l])) * jnp.max(jnp.abs(k_norm_g[l]))
        bounded = (score_bound <= SCORE_BOUND).astype(jnp.int32).reshape(1)
        att = _attn(bounded, qt, k, vt, lambda_q1[l].reshape(1, -1), lambda_k1[l].reshape(1, -1),
                    lambda_q2[l].reshape(1, -1), lambda_k2[l].reshape(1, -1),
                    subln_g[l].reshape(-1, 1), _lambda_init(l))

        rec = _lru(xr, gr, conv_w[l], conv_b[l].reshape(1, lru_w),
                   _block_diag(w_rg_a[l]).astype(BF16), b_rg_a[l].reshape(1, lru_w),
                   _block_diag(w_rg_x[l]).astype(BF16), b_rg_x[l].reshape(1, lru_w),
                   lru_lambda[l].reshape(1, lru_w))

        x1, hm2d, top_idx, gates, rank, counts = _outproj(
            att, rec, x, w_out[l].astype(BF16), g1, norm2_g[l].reshape(1, D), sc2, sh2,
            w_router[l].T, b_router[l].reshape(n_exp, 1), n_exp)

        block_e, nv, n_used, pos = _route(top_idx, rank, counts.reshape(n_exp))
        rows = D // LANES
        T = B * S
        pos_chunks = pos.reshape(TOP_K, T // SC_CHUNK, SC_CHUNK).transpose(1, 0, 2)
        xs = _sc_scatter_rows(hm2d.reshape(T, rows, LANES), pos_chunks, block_e.shape[0] * EB)
        ys = _moe(block_e, nv, n_used, xs.reshape(-1, LANES),
                  w_moe1[l], b_moe1[l].reshape(n_exp, 1, -1), w_moe2[l], b_moe2[l].reshape(n_exp, 1, -1))
        y4 = _sc_gather_rows(ys.reshape(-1, rows, LANES), pos.reshape(TOP_K * T))
        x = _combine(y4, gates, x1, g2)
    return x
```

```python
import functools
import math

import jax
import jax.numpy as jnp
from jax import lax
from jax.experimental import pallas as pl
from jax.experimental.pallas import tpu as pltpu
from jax.experimental.pallas import tpu_sc as plsc

F32 = jnp.float32
BF16 = jnp.bfloat16

CHUNK = 64
HEAD_DIM = 64
ROPE_THETA = 10000.0
LRU_BLOCK = 64
CONV_W = 4
LRU_C = 8.0
TOP_K = 4
SWIGLU_ALPHA = 1.702
SWIGLU_LIMIT = 7.0
NORM_EPS = 1e-6
SUBLN_EPS = 1e-5
LOG2_E = math.log2(math.e)
SCORE_BOUND = 60.0
BF16_SLACK = 1.02

LANES = 128
SUBLANES = 8
VMEM_LIMIT = 56 * 1024 * 1024
SC_CORES = 2
SC_SUBCORES = 16
SC_CHUNK = 32

TM_PROJ = 512
TQ = 512
KV_WIDE = 2
TT = 256
EB = 512
TM_COMB = 256
MOE_GROUPS = 2


def _split_bf16(a):
    hi = a.astype(BF16)
    lo = (a - hi.astype(F32)).astype(BF16)
    return hi, lo


def _dot(a, b):
    return jnp.dot(a, b, preferred_element_type=F32)


def _dot3(a, b):
    ah, al = _split_bf16(a)
    bh, bl = _split_bf16(b)
    return _dot(ah, bh) + (_dot(ah, bl) + _dot(al, bh))


def _ada_kernel(c_ref, w_ref, b_ref, o_ref):
    c = c_ref[...]
    s = c * jax.nn.sigmoid(c)
    o_ref[...] = _dot3(s, w_ref[...]) + b_ref[...]


def _ada(c, w_ada, b_ada):
    B, D = c.shape
    N = w_ada.shape[1]
    tn = D
    return pl.pallas_call(
        _ada_kernel,
        grid=(N // tn,),
        in_specs=[pl.BlockSpec((B, D), lambda j: (0, 0)),
                  pl.BlockSpec((D, tn), lambda j: (0, j)),
                  pl.BlockSpec((1, tn), lambda j: (0, j))],
        out_specs=pl.BlockSpec((B, tn), lambda j: (0, j)),
        out_shape=jax.ShapeDtypeStruct((B, N), F32),
        compiler_params=pltpu.CompilerParams(dimension_semantics=("arbitrary",),
                                             vmem_limit_bytes=VMEM_LIMIT),
        name="ada",
    )(c, w_ada, b_ada.reshape(1, N))


def _inproj_kernel(x_ref, g_ref, sc_ref, sh_ref, w_ref, gsum_ref, qg_ref, kg_ref, cos_ref, sin_ref,
                   qt_ref, k_ref, vt_ref, xr_ref, gr_ref, *, attn_w, lru_w):
    x = x_ref[0]
    ms = jnp.mean(x * x, axis=-1, keepdims=True)
    hn = x * lax.rsqrt(ms + NORM_EPS) * g_ref[...]
    hn = (hn * (1.0 + sc_ref[0]) + sh_ref[0]).astype(BF16)

    cos = cos_ref[...]
    sin = sin_ref[...]
    gsum = gsum_ref[...]
    lane = lax.broadcasted_iota(jnp.int32, cos.shape, 1)
    first_half = (lane % HEAD_DIM) < (HEAD_DIM // 2)

    def head_norm_rope(t, gain):
        hi, lo = _split_bf16(t * t)
        ssum = _dot(hi, gsum) + _dot(lo, gsum)
        t = t * lax.rsqrt(ssum * (1.0 / HEAD_DIM) + NORM_EPS) * gain
        half = HEAD_DIM // 2
        partner = jnp.where(first_half, pltpu.roll(t, attn_w - half, 1), pltpu.roll(t, half, 1))
        return t * cos + partner * sin

    q = head_norm_rope(_dot(hn, w_ref[:, 0:attn_w]), qg_ref[...]) * (HEAD_DIM ** -0.5 * LOG2_E)
    for h in range(attn_w // LANES):
        qt_ref[0, h] = q[:, h * LANES:(h + 1) * LANES].T.astype(BF16)

    k = head_norm_rope(_dot(hn, w_ref[:, attn_w:2 * attn_w]), kg_ref[...])
    k_ref[0] = k.astype(BF16)

    v = _dot(hn, w_ref[:, 2 * attn_w:3 * attn_w])
    for h in range(attn_w // LANES):
        vt_ref[0, h] = v[:, h * LANES:(h + 1) * LANES].T.astype(BF16)
    xr_ref[0] = _dot(hn, w_ref[:, 3 * attn_w:3 * attn_w + lru_w])
    gr_ref[0] = _dot(hn, w_ref[:, 3 * attn_w + lru_w:3 * attn_w + 2 * lru_w])


def _inproj(x, g1n, sc1, sh1, w_in_b, gsum, qg, kg, cos_t, sin_t, attn_w, lru_w):
    B, S, D = x.shape
    tm = min(TM_PROJ, S)
    in_w = w_in_b.shape[1]
    nh = attn_w // LANES
    kern = functools.partial(_inproj_kernel, attn_w=attn_w, lru_w=lru_w)
    return pl.pallas_call(
        kern,
        grid=(S // tm, B),
        in_specs=[pl.BlockSpec((1, tm, D), lambda i, b: (b, i, 0)),
                  pl.BlockSpec((1, D), lambda i, b: (0, 0)),
                  pl.BlockSpec((1, 1, D), lambda i, b: (b, 0, 0)),
                  pl.BlockSpec((1, 1, D), lambda i, b: (b, 0, 0)),
                  pl.BlockSpec((D, in_w), lambda i, b: (0, 0)),
                  pl.BlockSpec((attn_w, attn_w), lambda i, b: (0, 0)),
                  pl.BlockSpec((1, attn_w), lambda i, b: (0, 0)),
                  pl.BlockSpec((1, attn_w), lambda i, b: (0, 0)),
                  pl.BlockSpec((tm, attn_w), lambda i, b: (i, 0)),
                  pl.BlockSpec((tm, attn_w), lambda i, b: (i, 0))],
        out_specs=[pl.BlockSpec((1, nh, LANES, tm), lambda i, b: (b, 0, 0, i)),
                   pl.BlockSpec((1, tm, attn_w), lambda i, b: (b, i, 0)),
                   pl.BlockSpec((1, nh, LANES, tm), lambda i, b: (b, 0, 0, i)),
                   pl.BlockSpec((1, tm, lru_w), lambda i, b: (b, i, 0)),
                   pl.BlockSpec((1, tm, lru_w), lambda i, b: (b, i, 0))],
        out_shape=[jax.ShapeDtypeStruct((B, nh, LANES, S), BF16),
                   jax.ShapeDtypeStruct((B, S, attn_w), BF16),
                   jax.ShapeDtypeStruct((B, nh, LANES, S), BF16),
                   jax.ShapeDtypeStruct((B, S, lru_w), F32),
                   jax.ShapeDtypeStruct((B, S, lru_w), F32)],
        compiler_params=pltpu.CompilerParams(dimension_semantics=("arbitrary", "arbitrary"),
                                             vmem_limit_bytes=VMEM_LIMIT),
        name="inproj",
    )(x, g1n, sc1, sh1, w_in_b, gsum, qg, kg, cos_t, sin_t)


def _attn_kernel(bounded_ref, qt_ref, k_ref, vt_ref, vis_ref, lq1_ref, lk1_ref, lq2_ref, lk2_ref, sg_ref, o_ref,
                 *, tq, wide, lam_init):
    qi = pl.program_id(2)
    qt = qt_ref[0, 0]
    row = lax.broadcasted_iota(jnp.int32, qt.shape, 0)
    zero = jnp.zeros_like(qt)
    qs = jnp.concatenate([jnp.where(row < HEAD_DIM, qt, zero),
                          jnp.where(row >= HEAD_DIM, qt, zero)], axis=1)

    def scores(start, size):
        kk = k_ref[0, pl.ds(pl.multiple_of(start, tq), size), :]
        return _dot(kk, qs)

    def values(start, size, p):
        vt = vt_ref[0, 0, :, pl.ds(pl.multiple_of(start, tq), size)]
        return _dot(vt, p)

    zeros = (jnp.zeros((1, 2 * tq), F32), jnp.zeros((LANES, 2 * tq), F32))

    def bounded_path():
        def tile(start, size, carry, vis=None):
            l, acc = carry
            p = jnp.exp2(scores(start, size))
            if vis is not None:
                p = p * vis
            return l + jnp.sum(p, axis=0, keepdims=True), acc + values(start, size, p.astype(BF16))

        carry = lax.fori_loop(0, qi // wide, lambda j, c: tile(j * (wide * tq), wide * tq, c), zeros)
        for r in range(wide - 1):
            carry = lax.cond(qi % wide > r, lambda c, r=r: tile((qi // wide * wide + r) * tq, tq, c),
                             lambda c: c, carry)
        return tile(qi * tq, tq, carry, vis_ref[...])

    def online_path():
        def update(j, carry, masked):
            m, l, acc = carry
            s = scores(j * tq, tq)
            if masked:
                s = jnp.where(vis_ref[...] > 0.0, s, -jnp.inf)
            m_new = jnp.maximum(m, jnp.max(s, axis=0, keepdims=True))
            alpha = jnp.exp2(m - m_new)
            p = jnp.exp2(s - m_new)
            l = alpha * l + jnp.sum(p, axis=0, keepdims=True)
            return m_new, l, alpha * acc + values(j * tq, tq, p.astype(BF16))

        carry = (jnp.full((1, 2 * tq), -jnp.inf, F32),) + zeros
        carry = lax.fori_loop(0, qi, lambda j, c: update(j, c, False), carry)
        _, l, acc = update(qi, carry, True)
        return l, acc

    l, acc = lax.cond(bounded_ref[0] != 0, bounded_path, online_path)

    lam = (jnp.exp(jnp.sum(lq1_ref[...] * lk1_ref[...], axis=-1, keepdims=True))
           - jnp.exp(jnp.sum(lq2_ref[...] * lk2_ref[...], axis=-1, keepdims=True)) + lam_init)
    o = acc * (1.0 / l)
    o = o[:, :tq] - lam * o[:, tq:]
    ms = jnp.mean(o * o, axis=0, keepdims=True)
    o = o * lax.rsqrt(ms + SUBLN_EPS) * (sg_ref[...] * (1.0 - lam_init))
    o_ref[0] = o.T.astype(BF16)


def _attn(bounded, qt, k, vt, lq1, lk1, lq2, lk2, sg, lam_init):
    B, S, attn_w = k.shape
    nh = attn_w // LANES
    tq = min(TQ, S)
    wide = max(min(KV_WIDE, S // tq), 1)
    assert tq % CHUNK == 0
    kern = functools.partial(_attn_kernel, tq=tq, wide=wide, lam_init=lam_init)
    vec = lambda n: pl.BlockSpec((1, n), lambda b, h, i: (0, 0))
    key_chunk = jnp.arange(tq, dtype=jnp.int32)[:, None] // CHUNK
    qry_chunk = (jnp.arange(2 * tq, dtype=jnp.int32)[None, :] % tq) // CHUNK
    vis = (key_chunk <= qry_chunk).astype(F32)
    return pl.pallas_call(
        kern,
        grid=(B, nh, S // tq),
        in_specs=[pl.BlockSpec(memory_space=pltpu.SMEM),
                  pl.BlockSpec((1, 1, LANES, tq), lambda b, h, i: (b, h, 0, i)),
                  pl.BlockSpec((1, S, LANES), lambda b, h, i: (b, 0, h)),
                  pl.BlockSpec((1, 1, LANES, S), lambda b, h, i: (b, h, 0, 0)),
                  pl.BlockSpec((tq, 2 * tq), lambda b, h, i: (0, 0)),
                  vec(HEAD_DIM), vec(HEAD_DIM), vec(HEAD_DIM), vec(HEAD_DIM),
                  pl.BlockSpec((LANES, 1), lambda b, h, i: (0, 0))],
        out_specs=pl.BlockSpec((1, tq, LANES), lambda b, h, i: (b, i, h)),
        out_shape=jax.ShapeDtypeStruct((B, S, attn_w), BF16),
        compiler_params=pltpu.CompilerParams(
            dimension_semantics=("arbitrary", "arbitrary", "arbitrary"), vmem_limit_bytes=VMEM_LIMIT),
        name="attn",
    )(bounded, qt, k, vt, vis, lq1, lk1, lq2, lk2, sg)


def _shift_rows(a, d, fill):
    row = lax.broadcasted_iota(jnp.int32, a.shape, 0)
    return jnp.where(row < d, fill, pltpu.roll(a, d, 0))


def _lru_kernel(xr_ref, gr_ref, cw_ref, cb_ref, wa_ref, ba_ref, wx_ref, bx_ref, lam_ref, o_ref,
                tail_ref, h_ref, *, tt):
    @pl.when(pl.program_id(1) == 0)
    def _():
        tail_ref[...] = jnp.zeros_like(tail_ref)
        h_ref[...] = jnp.zeros_like(h_ref)

    x = xr_ref[0]
    ext = jnp.concatenate([tail_ref[...], x], axis=0)
    xc = cb_ref[...] + cw_ref[CONV_W - 1:CONV_W, :] * x
    for j in range(CONV_W - 1):
        back = CONV_W - 1 - j
        xc = xc + cw_ref[j:j + 1, :] * ext[SUBLANES - back:SUBLANES - back + tt]
    tail_ref[...] = x[tt - SUBLANES:]

    xcb = xc.astype(BF16)
    r = jax.nn.sigmoid(_dot(xcb, wa_ref[...]) + ba_ref[...])
    i = jax.nn.sigmoid(_dot(xcb, wx_ref[...]) + bx_ref[...])
    nl = -lam_ref[...]
    softplus = jnp.maximum(nl, 0.0) + jnp.log(1.0 + jnp.exp(-jnp.abs(nl)))
    log_a = (-LRU_C) * r * softplus
    a = jnp.exp(log_a)
    th = jnp.tanh(log_a)
    u = jnp.sqrt((-2.0) * th / (1.0 - th)) * (i * xc)

    d = 1
    while d < tt:
        u = u + a * _shift_rows(u, d, 0.0)
        a = a * _shift_rows(a, d, 1.0)
        d *= 2
    h = a * h_ref[0:1, :] + u
    h_ref[...] = jnp.broadcast_to(h[tt - 1:tt, :], h_ref.shape)

    g = gr_ref[0]
    gelu = 0.5 * g * (1.0 + jnp.tanh(math.sqrt(2.0 / math.pi) * (g + 0.044715 * (g * g * g))))
    o_ref[0] = (h * gelu).astype(BF16)


def _lru(xr, gr, conv_w, conv_b, wa_bd, b_a, wx_bd, b_x, lam):
    B, S, W = xr.shape
    tt = min(TT, S)
    kern = functools.partial(_lru_kernel, tt=tt)
    vec = pl.BlockSpec((1, W), lambda b, i: (0, 0))
    mat = pl.BlockSpec((W, W), lambda b, i: (0, 0))
    return pl.pallas_call(
        kern,
        grid=(B, S // tt),
        in_specs=[pl.BlockSpec((1, tt, W), lambda b, i: (b, i, 0)),
                  pl.BlockSpec((1, tt, W), lambda b, i: (b, i, 0)),
                  pl.BlockSpec((CONV_W, W), lambda b, i: (0, 0)),
                  vec, mat, vec, mat, vec, vec],
        out_specs=pl.BlockSpec((1, tt, W), lambda b, i: (b, i, 0)),
        out_shape=jax.ShapeDtypeStruct((B, S, W), BF16),
        scratch_shapes=[pltpu.VMEM((SUBLANES, W), F32), pltpu.VMEM((SUBLANES, W), F32)],
        compiler_params=pltpu.CompilerParams(dimension_semantics=("arbitrary", "arbitrary"),
                                             vmem_limit_bytes=VMEM_LIMIT),
        name="lru",
    )(xr, gr, conv_w, conv_b, wa_bd, b_a, wx_bd, b_x, lam)


def _outproj_kernel(att_ref, rec_ref, x_ref, wo_ref, g1_ref, n2_ref, sc_ref, sh_ref, wr_ref, br_ref,
                    x1_ref, hm_ref, idx_ref, gate_ref, rank_ref, count_ref, cnt_ref, *, attn_w, n_exp):
    tm = x_ref.shape[1]
    mix = _dot(att_ref[0], wo_ref[0:attn_w, :]) + _dot(rec_ref[0], wo_ref[attn_w:, :])
    x1 = x_ref[0] + g1_ref[0] * mix
    x1_ref[0] = x1
    ms = jnp.mean(x1 * x1, axis=-1, keepdims=True)
    hm = x1 * lax.rsqrt(ms + NORM_EPS) * n2_ref[...]
    hm = hm * (1.0 + sc_ref[0]) + sh_ref[0]
    for c in range(hm.shape[1] // LANES):
        hm_ref[pl.ds(c, tm, stride=SUBLANES), :] = hm[:, c * LANES:(c + 1) * LANES]

    hh, hl = _split_bf16(hm)
    wh, wl = _split_bf16(wr_ref[...])
    nt = (((1,), (1,)), ((), ()))
    logits = (lax.dot_general(wh, hh, nt, preferred_element_type=F32)
              + (lax.dot_general(wh, hl, nt, preferred_element_type=F32)
                 + lax.dot_general(wl, hh, nt, preferred_element_type=F32))) + br_ref[...]
    eidx = lax.broadcasted_iota(jnp.int32, logits.shape, 0)
    vals, idxs = [], []
    for _ in range(TOP_K):
        m = jnp.max(logits, axis=0, keepdims=True)
        am = jnp.min(jnp.where(logits == m, eidx, n_exp), axis=0, keepdims=True)
        vals.append(m)
        idxs.append(am)
        logits = jnp.where(eidx == am, -jnp.inf, logits)
    ex = [jnp.exp(v - vals[0]) for v in vals]
    den = ex[0] + ex[1] + ex[2] + ex[3]
    idx_ref[0] = jnp.concatenate(idxs, axis=0)
    gate_ref[0] = jnp.concatenate([e / den for e in ex], axis=0)

    @pl.when((pl.program_id(0) == 0) & (pl.program_id(1) == 0))
    def _():
        cnt_ref[...] = jnp.zeros_like(cnt_ref)

    hits = [eidx == am for am in idxs]
    sel = (hits[0] | hits[1] | hits[2] | hits[3]).astype(F32)
    earlier = (lax.broadcasted_iota(jnp.int32, (tm, tm), 0)
               < lax.broadcasted_iota(jnp.int32, (tm, tm), 1)).astype(BF16)
    before = _dot(sel.astype(BF16), earlier) + cnt_ref[...]
    rank_ref[0] = jnp.concatenate(
        [jnp.sum(jnp.where(h, before, 0.0), axis=0, keepdims=True) for h in hits], axis=0).astype(jnp.int32)
    cnt_ref[...] += jnp.sum(sel, axis=1, keepdims=True)
    count_ref[...] = cnt_ref[...].astype(jnp.int32)


def _outproj(att, rec, x, wo_b, g1, n2g, sc2, sh2, wr_t, br, n_exp, b0, B):
    _, S, D = x.shape
    attn_w = att.shape[2]
    lru_w = rec.shape[2]
    tm = min(TM_PROJ, S)
    ns = S // tm
    rows = D // LANES
    kern = functools.partial(_outproj_kernel, attn_w=attn_w, n_exp=n_exp)
    bvec = pl.BlockSpec((1, 1, D), lambda b, i: (b + b0, 0, 0))
    return pl.pallas_call(
        kern,
        grid=(B, ns),
        in_specs=[pl.BlockSpec((1, tm, attn_w), lambda b, i: (b + b0, i, 0)),
                  pl.BlockSpec((1, tm, lru_w), lambda b, i: (b + b0, i, 0)),
                  pl.BlockSpec((1, tm, D), lambda b, i: (b + b0, i, 0)),
                  pl.BlockSpec((attn_w + lru_w, D), lambda b, i: (0, 0)),
                  bvec,
                  pl.BlockSpec((1, D), lambda b, i: (0, 0)),
                  bvec, bvec,
                  pl.BlockSpec((n_exp, D), lambda b, i: (0, 0)),
                  pl.BlockSpec((n_exp, 1), lambda b, i: (0, 0))],
        out_specs=[pl.BlockSpec((1, tm, D), lambda b, i: (b, i, 0)),
                   pl.BlockSpec((tm * rows, LANES), lambda b, i: (b * ns + i, 0)),
                   pl.BlockSpec((1, TOP_K, tm), lambda b, i: (b, 0, i)),
                   pl.BlockSpec((1, TOP_K, tm), lambda b, i: (b, 0, i)),
                   pl.BlockSpec((1, TOP_K, tm), lambda b, i: (b, 0, i)),
                   pl.BlockSpec((n_exp, 1), lambda b, i: (0, 0))],
        out_shape=[jax.ShapeDtypeStruct((B, S, D), F32),
                   jax.ShapeDtypeStruct((B * S * rows, LANES), F32),
                   jax.ShapeDtypeStruct((B, TOP_K, S), jnp.int32),
                   jax.ShapeDtypeStruct((B, TOP_K, S), F32),
                   jax.ShapeDtypeStruct((B, TOP_K, S), jnp.int32),
                   jax.ShapeDtypeStruct((n_exp, 1), jnp.int32)],
        scratch_shapes=[pltpu.VMEM((n_exp, 1), F32)],
        compiler_params=pltpu.CompilerParams(dimension_semantics=("arbitrary", "arbitrary"),
                                             vmem_limit_bytes=VMEM_LIMIT),
        name="outproj",
    )(att, rec, x, wo_b, g1, n2g, sc2, sh2, wr_t, br)


def _sc_gather_rows(table, idx):
    n = idx.shape[0]
    workers = SC_CORES * SC_SUBCORES
    per_worker = n // workers
    assert n % (workers * SC_CHUNK) == 0, "index count must split evenly into per-subcore chunks"
    n_chunks = per_worker // SC_CHUNK
    mesh = plsc.VectorSubcoreMesh(core_axis_name="c", subcore_axis_name="s",
                                  num_cores=SC_CORES, num_subcores=SC_SUBCORES)

    @functools.partial(
        pl.kernel, mesh=mesh,
        out_type=jax.ShapeDtypeStruct((n,) + table.shape[1:], table.dtype),
        scratch_types=[pltpu.VMEM((SC_CHUNK,), jnp.int32),
                       pltpu.VMEM((SC_CHUNK,) + table.shape[1:], table.dtype),
                       pltpu.SemaphoreType.DMA],
        name="sc_gather_rows",
    )
    def gather(table_hbm, idx_hbm, out_hbm, idx_v, rows_v, sem):
        base = (lax.axis_index("s") * SC_CORES + lax.axis_index("c")) * per_worker

        @pl.loop(0, n_chunks)
        def _(ci):
            off = pl.multiple_of(base + ci * SC_CHUNK, SC_CHUNK)
            pltpu.sync_copy(idx_hbm.at[pl.ds(off, SC_CHUNK)], idx_v)
            pltpu.async_copy(table_hbm.at[idx_v], rows_v, sem).wait()
            pltpu.sync_copy(rows_v, out_hbm.at[pl.ds(off, SC_CHUNK)])

    return gather(table, idx)


def _sc_scatter_rows(src, pos, n_out):
    n_tok = src.shape[0]
    n_k = pos.shape[1]
    workers = SC_CORES * SC_SUBCORES
    per_worker = n_tok // workers
    assert n_tok % (workers * SC_CHUNK) == 0, "token count must split evenly into per-subcore chunks"
    n_chunks = per_worker // SC_CHUNK
    mesh = plsc.VectorSubcoreMesh(core_axis_name="c", subcore_axis_name="s",
                                  num_cores=SC_CORES, num_subcores=SC_SUBCORES)

    @functools.partial(
        pl.kernel, mesh=mesh,
        out_type=jax.ShapeDtypeStruct((n_out,) + src.shape[1:], src.dtype),
        scratch_types=[pltpu.VMEM((n_k, SC_CHUNK), jnp.int32),
                       pltpu.VMEM((SC_CHUNK,) + src.shape[1:], src.dtype),
                       pltpu.SemaphoreType.DMA],
        name="sc_scatter_rows",
    )
    def scatter(src_hbm, pos_hbm, out_hbm, idx_v, rows_v, sem):
        base = (lax.axis_index("s") * SC_CORES + lax.axis_index("c")) * n_chunks

        @pl.loop(0, n_chunks)
        def _(ci):
            chunk = base + ci
            pltpu.sync_copy(pos_hbm.at[chunk], idx_v)
            pltpu.sync_copy(src_hbm.at[pl.ds(pl.multiple_of(chunk * SC_CHUNK, SC_CHUNK), SC_CHUNK)], rows_v)
            copies = [pltpu.async_copy(rows_v, out_hbm.at[idx_v.at[k]], sem) for k in range(n_k)]
            for cp in copies:
                cp.wait()

    return scatter(src, pos)


def _moe_kernel(be_ref, nv_ref, nused_ref, x_ref, w1_ref, b1_ref, w2_ref, b2_ref, y_ref, w1b_ref, w2b_ref,
                *, d_ff):
    i = pl.program_id(0)
    rows = SUBLANES

    @pl.when(i < nused_ref[0])
    def _():
        @pl.when((i == 0) | (be_ref[i] != be_ref[jnp.maximum(i - 1, 0)]))
        def _():
            w1b_ref[...] = w1_ref[0].astype(BF16)
            w2b_ref[...] = w2_ref[0].astype(BF16)

        x = jnp.concatenate([x_ref[pl.ds(c, EB, stride=rows), :] for c in range(rows)], axis=1)
        valid = lax.broadcasted_iota(jnp.int32, (EB, 1), 0) < nv_ref[i]
        x = jnp.where(valid, x, 0.0).astype(BF16)
        hu = _dot(x, w1b_ref[...]) + b1_ref[0]
        glu = jnp.minimum(hu[:, :d_ff], SWIGLU_LIMIT)
        lin = jnp.clip(hu[:, d_ff:], -SWIGLU_LIMIT, SWIGLU_LIMIT)
        act = glu * jax.nn.sigmoid(SWIGLU_ALPHA * glu) * (lin + 1.0)
        y = _dot(act.astype(BF16), w2b_ref[...]) + b2_ref[0]
        for c in range(rows):
            y_ref[pl.ds(c, EB, stride=rows), :] = y[:, c * LANES:(c + 1) * LANES]

    @pl.when(i >= nused_ref[0])
    def _():
        y_ref[...] = jnp.zeros_like(y_ref)


def _moe(block_e, nv, n_used, xs2d, w1, b1, w2, b2):
    n_blocks = block_e.shape[0]
    n_exp, D, ff2 = w1.shape
    d_ff = ff2 // 2
    rows = D // LANES
    assert rows == SUBLANES, "a token row must fill exactly one (8,128) f32 tile"
    kern = functools.partial(_moe_kernel, d_ff=d_ff)
    grid_spec = pltpu.PrefetchScalarGridSpec(
        num_scalar_prefetch=3,
        grid=(n_blocks,),
        in_specs=[pl.BlockSpec((EB * rows, LANES), lambda i, be, nv, nu: (i, 0)),
                  pl.BlockSpec((1, D, ff2), lambda i, be, nv, nu: (be[i], 0, 0)),
                  pl.BlockSpec((1, 1, ff2), lambda i, be, nv, nu: (be[i], 0, 0)),
                  pl.BlockSpec((1, d_ff, D), lambda i, be, nv, nu: (be[i], 0, 0)),
                  pl.BlockSpec((1, 1, D), lambda i, be, nv, nu: (be[i], 0, 0))],
        out_specs=pl.BlockSpec((EB * rows, LANES), lambda i, be, nv, nu: (i, 0)),
        scratch_shapes=[pltpu.VMEM((D, ff2), BF16), pltpu.VMEM((d_ff, D), BF16)],
    )
    return pl.pallas_call(
        kern,
        grid_spec=grid_spec,
        out_shape=jax.ShapeDtypeStruct((n_blocks * EB * rows, LANES), F32),
        compiler_params=pltpu.CompilerParams(dimension_semantics=("arbitrary",),
                                             vmem_limit_bytes=VMEM_LIMIT),
        name="moe",
    )(block_e, nv, n_used, xs2d, w1, b1, w2, b2)


def _combine_kernel(y_ref, gate_ref, x1_ref, g2_ref, *rest):
    o_ref = rest[-1]
    tm = x1_ref.shape[1]
    diag = (lax.broadcasted_iota(jnp.int32, (tm, tm), 0) == lax.broadcasted_iota(jnp.int32, (tm, tm), 1))
    moe = jnp.zeros(x1_ref.shape[1:], F32)
    for k in range(TOP_K):
        yk = jnp.concatenate([y_ref[k, pl.ds(c, tm, stride=SUBLANES), :] for c in range(SUBLANES)], axis=1)
        gate_col = jnp.sum(jnp.where(diag, gate_ref[0, k:k + 1, :], 0.0), axis=1, keepdims=True)
        moe = moe + gate_col * yk
    o_ref[0] = x1_ref[0] + g2_ref[0] * moe


def _combine(y4, gates, x1, g2, b0, n_batch, out_prev):
    B, S, D = x1.shape
    rows = D // LANES
    tm = min(TM_COMB, S)
    ns = S // tm
    in_specs = [pl.BlockSpec((TOP_K, tm * rows, LANES), lambda b, i: (0, b * ns + i, 0)),
                pl.BlockSpec((1, TOP_K, tm), lambda b, i: (b, 0, i)),
                pl.BlockSpec((1, tm, D), lambda b, i: (b, i, 0)),
                pl.BlockSpec((1, 1, D), lambda b, i: (b + b0, 0, 0))]
    args = [y4.reshape(TOP_K, B * S * rows, LANES), gates, x1, g2]
    aliases = {}
    if out_prev is not None:
        in_specs.append(pl.BlockSpec(memory_space=pl.ANY))
        args.append(out_prev)
        aliases = {len(args) - 1: 0}
    return pl.pallas_call(
        _combine_kernel,
        grid=(B, ns),
        in_specs=in_specs,
        out_specs=pl.BlockSpec((1, tm, D), lambda b, i: (b + b0, i, 0)),
        out_shape=jax.ShapeDtypeStruct((n_batch, S, D), F32),
        input_output_aliases=aliases,
        compiler_params=pltpu.CompilerParams(dimension_semantics=("arbitrary", "arbitrary"),
                                             vmem_limit_bytes=VMEM_LIMIT),
        name="combine",
    )(*args)


def _route(top_idx, rank, counts):
    B, K, S = top_idx.shape
    T = B * S
    n_exp = counts.shape[0]
    nblk = (counts + EB - 1) // EB
    bend = jnp.cumsum(nblk)
    bstart = bend - nblk
    n_blocks = -(-(T * K) // EB) + n_exp
    blk = jnp.arange(n_blocks, dtype=jnp.int32)
    block_e = jnp.minimum(jnp.sum(blk[:, None] >= bend[None, :], axis=1), n_exp - 1).astype(jnp.int32)
    nv = jnp.where(blk < bend[-1], jnp.clip(counts[block_e] - (blk - bstart[block_e]) * EB, 0, EB), 0)
    start_row = jnp.sum(jnp.where(top_idx[..., None] == jnp.arange(n_exp, dtype=jnp.int32),
                                  (bstart * EB).astype(jnp.int32), 0), axis=-1)
    pos = (start_row + rank).transpose(1, 0, 2).reshape(K, T).astype(jnp.int32)
    return block_e, nv.astype(jnp.int32), bend[-1:].astype(jnp.int32), pos


def _lambda_init(layer_idx):
    return 0.8 - 0.6 * math.exp(-0.3 * layer_idx)


def _block_diag(w):
    n, c, d = w.shape
    eye = jnp.eye(n, dtype=w.dtype)
    return (eye[:, None, :, None] * w[:, :, None, :]).reshape(n * c, n * d)


def kernel(x, c, w_ada, b_ada, norm1_g, w_in, q_norm_g, k_norm_g, lambda_q1, lambda_k1, lambda_q2, lambda_k2, subln_g, conv_w, conv_b, w_rg_a, b_rg_a, w_rg_x, b_rg_x, lru_lambda, w_out, norm2_g, w_router, b_router, w_moe1, b_moe1, w_moe2, b_moe2):
    B, S, D = x.shape
    depth = w_ada.shape[0]
    lru_w = conv_w.shape[2]
    attn_w = (w_in.shape[2] - 2 * lru_w) // 3
    n_exp = w_router.shape[2]
    n_heads = attn_w // HEAD_DIM
    assert attn_w % LANES == 0 and S % CHUNK == 0

    half = HEAD_DIM // 2
    inv = ROPE_THETA ** (-jnp.arange(half, dtype=F32) / half)
    ang = jnp.arange(S, dtype=jnp.int32).astype(F32)[:, None] * inv[None, :]
    cos_t = jnp.tile(jnp.concatenate([jnp.cos(ang), jnp.cos(ang)], axis=1), (1, n_heads))
    sin_t = jnp.tile(jnp.concatenate([-jnp.sin(ang), jnp.sin(ang)], axis=1), (1, n_heads))
    group = jnp.arange(attn_w, dtype=jnp.int32) // HEAD_DIM
    gsum = (group[:, None] == group[None, :]).astype(BF16)

    for l in range(depth):
        mod = _ada(c, w_ada[l], b_ada[l])
        sh1, sc1, g1, sh2, sc2, g2 = [m.reshape(B, 1, D) for m in jnp.split(mod, 6, axis=-1)]

        qt, k, vt, xr, gr = _inproj(
            x, norm1_g[l].reshape(1, D), sc1, sh1, w_in[l].astype(BF16), gsum,
            jnp.tile(q_norm_g[l], n_heads).reshape(1, attn_w), jnp.tile(k_norm_g[l], n_heads).reshape(1, attn_w),
            cos_t, sin_t, attn_w, lru_w)

        score_bound = (HEAD_DIM ** 0.5 * LOG2_E * BF16_SLACK) * jnp.max(jnp.abs(q_norm_g[l])) * jnp.max(jnp.abs(k_norm_g[l]))
        bounded = (score_bound <= SCORE_BOUND).astype(jnp.int32).reshape(1)
        att = _attn(bounded, qt, k, vt, lambda_q1[l].reshape(1, -1), lambda_k1[l].reshape(1, -1),
                    lambda_q2[l].reshape(1, -1), lambda_k2[l].reshape(1, -1),
                    subln_g[l].reshape(-1, 1), _lambda_init(l))

        rec = _lru(xr, gr, conv_w[l], conv_b[l].reshape(1, lru_w),
                   _block_diag(w_rg_a[l]).astype(BF16), b_rg_a[l].reshape(1, lru_w),
                   _block_diag(w_rg_x[l]).astype(BF16), b_rg_x[l].reshape(1, lru_w),
                   lru_lambda[l].reshape(1, lru_w))

        n_groups = MOE_GROUPS if B % MOE_GROUPS == 0 else 1
        nb = B // n_groups
        T = nb * S
        rows = D // LANES
        wo_b = w_out[l].astype(BF16)
        x_in, x_out = x, None
        for grp in range(n_groups):
            b0 = grp * nb
            x1, hm2d, top_idx, gates, rank, counts = _outproj(
                att, rec, x_in, wo_b, g1, norm2_g[l].reshape(1, D), sc2, sh2,
                w_router[l].T, b_router[l].reshape(n_exp, 1), n_exp, b0, nb)
            block_e, nv, n_used, pos = _route(top_idx, rank, counts.reshape(n_exp))
            pos_chunks = pos.reshape(TOP_K, T // SC_CHUNK, SC_CHUNK).transpose(1, 0, 2)
            xs = _sc_scatter_rows(hm2d.reshape(T, rows, LANES), pos_chunks, block_e.shape[0] * EB)
            ys = _moe(block_e, nv, n_used, xs.reshape(-1, LANES),
                      w_moe1[l], b_moe1[l].reshape(n_exp, 1, -1), w_moe2[l], b_moe2[l].reshape(n_exp, 1, -1))
            y4 = _sc_gather_rows(ys.reshape(-1, rows, LANES), pos.reshape(TOP_K * T))
            x_out = _combine(y4, gates, x1, g2, b0, B, x_out)
        x = x_out
    return x
```

```python
import functools
import math

import jax
import jax.numpy as jnp
from jax import lax
from jax.experimental import pallas as pl
from jax.experimental.pallas import tpu as pltpu
from jax.experimental.pallas import tpu_sc as plsc

F32 = jnp.float32
BF16 = jnp.bfloat16

CHUNK = 64
HEAD_DIM = 64
ROPE_THETA = 10000.0
LRU_BLOCK = 64
CONV_W = 4
LRU_C = 8.0
TOP_K = 4
SWIGLU_ALPHA = 1.702
SWIGLU_LIMIT = 7.0
NORM_EPS = 1e-6
SUBLN_EPS = 1e-5
LOG2_E = math.log2(math.e)
SCORE_BOUND = 60.0
BF16_SLACK = 1.02

LANES = 128
SUBLANES = 8
VMEM_LIMIT = 56 * 1024 * 1024
SC_CORES = 2
SC_SUBCORES = 16
SC_CHUNK = 32

TM_PROJ = 512
TQ = 512
KV_WIDE = 4
TT = 256
EB = 512
TM_COMB = 512
MOE_GROUPS = 2


def _split_bf16(a):
    hi = a.astype(BF16)
    lo = (a - hi.astype(F32)).astype(BF16)
    return hi, lo


def _dot(a, b):
    return jnp.dot(a, b, preferred_element_type=F32)


def _dot3(a, b):
    ah, al = _split_bf16(a)
    bh, bl = _split_bf16(b)
    return _dot(ah, bh) + (_dot(ah, bl) + _dot(al, bh))


def _ada_kernel(c_ref, w_ref, b_ref, o_ref):
    c = c_ref[...]
    s = c * jax.nn.sigmoid(c)
    o_ref[...] = _dot3(s, w_ref[...]) + b_ref[...]


def _ada(c, w_ada, b_ada):
    B, D = c.shape
    N = w_ada.shape[1]
    tn = D
    return pl.pallas_call(
        _ada_kernel,
        grid=(N // tn,),
        in_specs=[pl.BlockSpec((B, D), lambda j: (0, 0)),
                  pl.BlockSpec((D, tn), lambda j: (0, j)),
                  pl.BlockSpec((1, tn), lambda j: (0, j))],
        out_specs=pl.BlockSpec((B, tn), lambda j: (0, j)),
        out_shape=jax.ShapeDtypeStruct((B, N), F32),
        compiler_params=pltpu.CompilerParams(dimension_semantics=("arbitrary",),
                                             vmem_limit_bytes=VMEM_LIMIT),
        name="ada",
    )(c, w_ada, b_ada.reshape(1, N))


def _inproj_kernel(x_ref, g_ref, sc_ref, sh_ref, w_ref, gsum_ref, qg_ref, kg_ref, cos_ref, sin_ref,
                   qt_ref, k_ref, vt_ref, xr_ref, gr_ref, *, attn_w, lru_w):
    x = x_ref[0]
    ms = jnp.mean(x * x, axis=-1, keepdims=True)
    hn = x * lax.rsqrt(ms + NORM_EPS) * g_ref[...]
    hn = (hn * (1.0 + sc_ref[0]) + sh_ref[0]).astype(BF16)

    cos = cos_ref[...]
    sin = sin_ref[...]
    gsum = gsum_ref[...]
    lane = lax.broadcasted_iota(jnp.int32, cos.shape, 1)
    first_half = (lane % HEAD_DIM) < (HEAD_DIM // 2)

    def head_norm_rope(t, gain):
        hi, lo = _split_bf16(t * t)
        ssum = _dot(hi, gsum) + _dot(lo, gsum)
        t = t * lax.rsqrt(ssum * (1.0 / HEAD_DIM) + NORM_EPS) * gain
        half = HEAD_DIM // 2
        partner = jnp.where(first_half, pltpu.roll(t, attn_w - half, 1), pltpu.roll(t, half, 1))
        return t * cos + partner * sin

    q = head_norm_rope(_dot(hn, w_ref[:, 0:attn_w]), qg_ref[...]) * (HEAD_DIM ** -0.5 * LOG2_E)
    for h in range(attn_w // LANES):
        qt_ref[0, h] = q[:, h * LANES:(h + 1) * LANES].T.astype(BF16)

    k = head_norm_rope(_dot(hn, w_ref[:, attn_w:2 * attn_w]), kg_ref[...])
    k_ref[0] = k.astype(BF16)

    v = _dot(hn, w_ref[:, 2 * attn_w:3 * attn_w])
    for h in range(attn_w // LANES):
        vt_ref[0, h] = v[:, h * LANES:(h + 1) * LANES].T.astype(BF16)
    xr_ref[0] = _dot(hn, w_ref[:, 3 * attn_w:3 * attn_w + lru_w])
    gr_ref[0] = _dot(hn, w_ref[:, 3 * attn_w + lru_w:3 * attn_w + 2 * lru_w])


def _inproj(x, g1n, sc1, sh1, w_in_b, gsum, qg, kg, cos_t, sin_t, attn_w, lru_w):
    B, S, D = x.shape
    tm = min(TM_PROJ, S)
    in_w = w_in_b.shape[1]
    nh = attn_w // LANES
    kern = functools.partial(_inproj_kernel, attn_w=attn_w, lru_w=lru_w)
    return pl.pallas_call(
        kern,
        grid=(S // tm, B),
        in_specs=[pl.BlockSpec((1, tm, D), lambda i, b: (b, i, 0)),
                  pl.BlockSpec((1, D), lambda i, b: (0, 0)),
                  pl.BlockSpec((1, 1, D), lambda i, b: (b, 0, 0)),
                  pl.BlockSpec((1, 1, D), lambda i, b: (b, 0, 0)),
                  pl.BlockSpec((D, in_w), lambda i, b: (0, 0)),
                  pl.BlockSpec((attn_w, attn_w), lambda i, b: (0, 0)),
                  pl.BlockSpec((1, attn_w), lambda i, b: (0, 0)),
                  pl.BlockSpec((1, attn_w), lambda i, b: (0, 0)),
                  pl.BlockSpec((tm, attn_w), lambda i, b: (i, 0)),
                  pl.BlockSpec((tm, attn_w), lambda i, b: (i, 0))],
        out_specs=[pl.BlockSpec((1, nh, LANES, tm), lambda i, b: (b, 0, 0, i)),
                   pl.BlockSpec((1, tm, attn_w), lambda i, b: (b, i, 0)),
                   pl.BlockSpec((1, nh, LANES, tm), lambda i, b: (b, 0, 0, i)),
                   pl.BlockSpec((1, tm, lru_w), lambda i, b: (b, i, 0)),
                   pl.BlockSpec((1, tm, lru_w), lambda i, b: (b, i, 0))],
        out_shape=[jax.ShapeDtypeStruct((B, nh, LANES, S), BF16),
                   jax.ShapeDtypeStruct((B, S, attn_w), BF16),
                   jax.ShapeDtypeStruct((B, nh, LANES, S), BF16),
                   jax.ShapeDtypeStruct((B, S, lru_w), F32),
                   jax.ShapeDtypeStruct((B, S, lru_w), F32)],
        compiler_params=pltpu.CompilerParams(dimension_semantics=("arbitrary", "arbitrary"),
                                             vmem_limit_bytes=VMEM_LIMIT),
        name="inproj",
    )(x, g1n, sc1, sh1, w_in_b, gsum, qg, kg, cos_t, sin_t)


def _attn_kernel(bounded_ref, qt_ref, k_ref, vt_ref, vis_ref, lq1_ref, lk1_ref, lq2_ref, lk2_ref, sg_ref, o_ref,
                 *, tq, wide, lam_init):
    qi = pl.program_id(2)
    qt = qt_ref[0, 0]
    row = lax.broadcasted_iota(jnp.int32, qt.shape, 0)
    zero = jnp.zeros_like(qt)
    qs = jnp.concatenate([jnp.where(row < HEAD_DIM, qt, zero),
                          jnp.where(row >= HEAD_DIM, qt, zero)], axis=1)

    def scores(start, size):
        kk = k_ref[0, pl.ds(pl.multiple_of(start, tq), size), :]
        return _dot(kk, qs)

    def values(start, size, p):
        vt = vt_ref[0, 0, :, pl.ds(pl.multiple_of(start, tq), size)]
        return _dot(vt, p)

    zeros = (jnp.zeros((1, 2 * tq), F32), jnp.zeros((LANES, 2 * tq), F32))

    def bounded_path():
        def tile(start, size, carry, vis=None):
            l, acc = carry
            p = jnp.exp2(scores(start, size))
            if vis is not None:
                p = p * vis
            return l + jnp.sum(p, axis=0, keepdims=True), acc + values(start, size, p.astype(BF16))

        carry = lax.fori_loop(0, qi // wide, lambda j, c: tile(j * (wide * tq), wide * tq, c), zeros)
        start = (qi // wide) * (wide * tq)
        piece = wide // 2
        while piece >= 1:
            take = ((qi % wide) & piece) != 0
            carry = lax.cond(take, lambda c, s=start, n=piece * tq: tile(s, n, c), lambda c: c, carry)
            start = start + jnp.where(take, piece * tq, 0)
            piece //= 2
        return tile(qi * tq, tq, carry, vis_ref[...])

    def online_path():
        def update(j, carry, masked):
            m, l, acc = carry
            s = scores(j * tq, tq)
            if masked:
                s = jnp.where(vis_ref[...] > 0.0, s, -jnp.inf)
            m_new = jnp.maximum(m, jnp.max(s, axis=0, keepdims=True))
            alpha = jnp.exp2(m - m_new)
            p = jnp.exp2(s - m_new)
            l = alpha * l + jnp.sum(p, axis=0, keepdims=True)
            return m_new, l, alpha * acc + values(j * tq, tq, p.astype(BF16))

        carry = (jnp.full((1, 2 * tq), -jnp.inf, F32),) + zeros
        carry = lax.fori_loop(0, qi, lambda j, c: update(j, c, False), carry)
        _, l, acc = update(qi, carry, True)
        return l, acc

    l, acc = lax.cond(bounded_ref[0] != 0, bounded_path, online_path)

    lam = (jnp.exp(jnp.sum(lq1_ref[...] * lk1_ref[...], axis=-1, keepdims=True))
           - jnp.exp(jnp.sum(lq2_ref[...] * lk2_ref[...], axis=-1, keepdims=True)) + lam_init)
    o = acc * (1.0 / l)
    o = o[:, :tq] - lam * o[:, tq:]
    ms = jnp.mean(o * o, axis=0, keepdims=True)
    o = o * lax.rsqrt(ms + SUBLN_EPS) * (sg_ref[...] * (1.0 - lam_init))
    o_ref[0] = o.T.astype(BF16)


def _attn(bounded, qt, k, vt, lq1, lk1, lq2, lk2, sg, lam_init):
    B, S, attn_w = k.shape
    nh = attn_w // LANES
    tq = min(TQ, S)
    wide = max(min(KV_WIDE, S // tq), 1)
    assert tq % CHUNK == 0 and wide & (wide - 1) == 0
    kern = functools.partial(_attn_kernel, tq=tq, wide=wide, lam_init=lam_init)
    vec = lambda n: pl.BlockSpec((1, n), lambda b, h, i: (0, 0))
    key_chunk = jnp.arange(tq, dtype=jnp.int32)[:, None] // CHUNK
    qry_chunk = (jnp.arange(2 * tq, dtype=jnp.int32)[None, :] % tq) // CHUNK
    vis = (key_chunk <= qry_chunk).astype(F32)
    return pl.pallas_call(
        kern,
        grid=(B, nh, S // tq),
        in_specs=[pl.BlockSpec(memory_space=pltpu.SMEM),
                  pl.BlockSpec((1, 1, LANES, tq), lambda b, h, i: (b, h, 0, i)),
                  pl.BlockSpec((1, S, LANES), lambda b, h, i: (b, 0, h)),
                  pl.BlockSpec((1, 1, LANES, S), lambda b, h, i: (b, h, 0, 0)),
                  pl.BlockSpec((tq, 2 * tq), lambda b, h, i: (0, 0)),
                  vec(HEAD_DIM), vec(HEAD_DIM), vec(HEAD_DIM), vec(HEAD_DIM),
                  pl.BlockSpec((LANES, 1), lambda b, h, i: (0, 0))],
        out_specs=pl.BlockSpec((1, tq, LANES), lambda b, h, i: (b, i, h)),
        out_shape=jax.ShapeDtypeStruct((B, S, attn_w), BF16),
        compiler_params=pltpu.CompilerParams(
            dimension_semantics=("arbitrary", "arbitrary", "arbitrary"), vmem_limit_bytes=VMEM_LIMIT),
        name="attn",
    )(bounded, qt, k, vt, vis, lq1, lk1, lq2, lk2, sg)


def _lru_kernel(xr_ref, gr_ref, cw_ref, cb_ref, wa_ref, ba_ref, wx_ref, bx_ref, lam_ref, o_ref,
                tail_ref, h_ref, *, tt):
    @pl.when(pl.program_id(1) == 0)
    def _():
        tail_ref[...] = jnp.zeros_like(tail_ref)
        h_ref[...] = jnp.zeros_like(h_ref)

    x = xr_ref[0]
    ext = jnp.concatenate([tail_ref[...], x], axis=0)
    xc = cb_ref[...] + cw_ref[CONV_W - 1:CONV_W, :] * x
    for j in range(CONV_W - 1):
        back = CONV_W - 1 - j
        xc = xc + cw_ref[j:j + 1, :] * ext[SUBLANES - back:SUBLANES - back + tt]
    tail_ref[...] = x[tt - SUBLANES:]

    xcb = xc.astype(BF16)
    r = jax.nn.sigmoid(_dot(xcb, wa_ref[...]) + ba_ref[...])
    i = jax.nn.sigmoid(_dot(xcb, wx_ref[...]) + bx_ref[...])
    nl = -lam_ref[...]
    softplus = jnp.maximum(nl, 0.0) + jnp.log(1.0 + jnp.exp(-jnp.abs(nl)))
    log_a = (-LRU_C) * r * softplus
    a = jnp.exp(log_a)
    th = jnp.tanh(log_a)
    u = jnp.sqrt((-2.0) * th / (1.0 - th)) * (i * xc)

    n_groups = tt // SUBLANES
    a = a.reshape(n_groups, SUBLANES, a.shape[1])
    u = u.reshape(a.shape)
    in_group = lax.broadcasted_iota(jnp.int32, a.shape, 1)
    d = 1
    while d < SUBLANES:
        keep = in_group >= d
        u = u + a * jnp.where(keep, pltpu.roll(u, d, 1), 0.0)
        a = a * jnp.where(keep, pltpu.roll(a, d, 1), 1.0)
        d *= 2
    carry = h_ref[0:1, :]
    groups = []
    for g in range(n_groups):
        hg = a[g] * carry + u[g]
        groups.append(hg)
        carry = hg[SUBLANES - 1:SUBLANES, :]
    h = jnp.concatenate(groups, axis=0)
    h_ref[...] = jnp.broadcast_to(carry, h_ref.shape)

    g = gr_ref[0]
    gelu = 0.5 * g * (1.0 + jnp.tanh(math.sqrt(2.0 / math.pi) * (g + 0.044715 * (g * g * g))))
    o_ref[0] = (h * gelu).astype(BF16)


def _lru(xr, gr, conv_w, conv_b, wa_bd, b_a, wx_bd, b_x, lam):
    B, S, W = xr.shape
    tt = min(TT, S)
    kern = functools.partial(_lru_kernel, tt=tt)
    vec = pl.BlockSpec((1, W), lambda b, i: (0, 0))
    mat = pl.BlockSpec((W, W), lambda b, i: (0, 0))
    return pl.pallas_call(
        kern,
        grid=(B, S // tt),
        in_specs=[pl.BlockSpec((1, tt, W), lambda b, i: (b, i, 0)),
                  pl.BlockSpec((1, tt, W), lambda b, i: (b, i, 0)),
                  pl.BlockSpec((CONV_W, W), lambda b, i: (0, 0)),
                  vec, mat, vec, mat, vec, vec],
        out_specs=pl.BlockSpec((1, tt, W), lambda b, i: (b, i, 0)),
        out_shape=jax.ShapeDtypeStruct((B, S, W), BF16),
        scratch_shapes=[pltpu.VMEM((SUBLANES, W), F32), pltpu.VMEM((SUBLANES, W), F32)],
        compiler_params=pltpu.CompilerParams(dimension_semantics=("arbitrary", "arbitrary"),
                                             vmem_limit_bytes=VMEM_LIMIT),
        name="lru",
    )(xr, gr, conv_w, conv_b, wa_bd, b_a, wx_bd, b_x, lam)


def _outproj_kernel(att_ref, rec_ref, x_ref, wo_ref, g1_ref, n2_ref, sc_ref, sh_ref, wr_ref, br_ref,
                    x1_ref, hm_ref, idx_ref, gate_ref, rank_ref, count_ref, cnt_ref, *, attn_w, n_exp):
    tm = x_ref.shape[1]
    mix = _dot(att_ref[0], wo_ref[0:attn_w, :]) + _dot(rec_ref[0], wo_ref[attn_w:, :])
    x1 = x_ref[0] + g1_ref[0] * mix
    x1_ref[0] = x1
    ms = jnp.mean(x1 * x1, axis=-1, keepdims=True)
    hm = x1 * lax.rsqrt(ms + NORM_EPS) * n2_ref[...]
    hm = hm * (1.0 + sc_ref[0]) + sh_ref[0]
    for c in range(hm.shape[1] // LANES):
        hm_ref[pl.ds(c, tm, stride=SUBLANES), :] = hm[:, c * LANES:(c + 1) * LANES]

    hh, hl = _split_bf16(hm)
    wh, wl = _split_bf16(wr_ref[...])
    nt = (((1,), (1,)), ((), ()))
    logits = (lax.dot_general(wh, hh, nt, preferred_element_type=F32)
              + (lax.dot_general(wh, hl, nt, preferred_element_type=F32)
                 + lax.dot_general(wl, hh, nt, preferred_element_type=F32))) + br_ref[...]
    eidx = lax.broadcasted_iota(jnp.int32, logits.shape, 0)
    vals, idxs = [], []
    for _ in range(TOP_K):
        m = jnp.max(logits, axis=0, keepdims=True)
        am = jnp.min(jnp.where(logits == m, eidx, n_exp), axis=0, keepdims=True)
        vals.append(m)
        idxs.append(am)
        logits = jnp.where(eidx == am, -jnp.inf, logits)
    ex = [jnp.exp(v - vals[0]) for v in vals]
    den = ex[0] + ex[1] + ex[2] + ex[3]
    idx_ref[0] = jnp.concatenate(idxs, axis=0)
    gate_ref[0] = jnp.concatenate([e / den for e in ex], axis=0)

    @pl.when((pl.program_id(0) == 0) & (pl.program_id(1) == 0))
    def _():
        cnt_ref[...] = jnp.zeros_like(cnt_ref)

    hits = [eidx == am for am in idxs]
    sel = (hits[0] | hits[1] | hits[2] | hits[3]).astype(F32)
    earlier = (lax.broadcasted_iota(jnp.int32, (tm, tm), 0)
               < lax.broadcasted_iota(jnp.int32, (tm, tm), 1)).astype(BF16)
    before = _dot(sel.astype(BF16), earlier) + cnt_ref[...]
    rank_ref[0] = jnp.concatenate(
        [jnp.sum(jnp.where(h, before, 0.0), axis=0, keepdims=True) for h in hits], axis=0).astype(jnp.int32)
    cnt_ref[...] += jnp.sum(sel, axis=1, keepdims=True)
    count_ref[...] = cnt_ref[...].astype(jnp.int32)


def _outproj(att, rec, x, wo_b, g1, n2g, sc2, sh2, wr_t, br, n_exp, b0, B):
    _, S, D = x.shape
    attn_w = att.shape[2]
    lru_w = rec.shape[2]
    tm = min(TM_PROJ, S)
    ns = S // tm
    rows = D // LANES
    kern = functools.partial(_outproj_kernel, attn_w=attn_w, n_exp=n_exp)
    bvec = pl.BlockSpec((1, 1, D), lambda b, i: (b + b0, 0, 0))
    return pl.pallas_call(
        kern,
        grid=(B, ns),
        in_specs=[pl.BlockSpec((1, tm, attn_w), lambda b, i: (b + b0, i, 0)),
                  pl.BlockSpec((1, tm, lru_w), lambda b, i: (b + b0, i, 0)),
                  pl.BlockSpec((1, tm, D), lambda b, i: (b + b0, i, 0)),
                  pl.BlockSpec((attn_w + lru_w, D), lambda b, i: (0, 0)),
                  bvec,
                  pl.BlockSpec((1, D), lambda b, i: (0, 0)),
                  bvec, bvec,
                  pl.BlockSpec((n_exp, D), lambda b, i: (0, 0)),
                  pl.BlockSpec((n_exp, 1), lambda b, i: (0, 0))],
        out_specs=[pl.BlockSpec((1, tm, D), lambda b, i: (b, i, 0)),
                   pl.BlockSpec((tm * rows, LANES), lambda b, i: (b * ns + i, 0)),
                   pl.BlockSpec((1, TOP_K, tm), lambda b, i: (b, 0, i)),
                   pl.BlockSpec((1, TOP_K, tm), lambda b, i: (b, 0, i)),
                   pl.BlockSpec((1, TOP_K, tm), lambda b, i: (b, 0, i)),
                   pl.BlockSpec((n_exp, 1), lambda b, i: (0, 0))],
        out_shape=[jax.ShapeDtypeStruct((B, S, D), F32),
                   jax.ShapeDtypeStruct((B * S * rows, LANES), F32),
                   jax.ShapeDtypeStruct((B, TOP_K, S), jnp.int32),
                   jax.ShapeDtypeStruct((B, TOP_K, S), F32),
                   jax.ShapeDtypeStruct((B, TOP_K, S), jnp.int32),
                   jax.ShapeDtypeStruct((n_exp, 1), jnp.int32)],
        scratch_shapes=[pltpu.VMEM((n_exp, 1), F32)],
        compiler_params=pltpu.CompilerParams(dimension_semantics=("arbitrary", "arbitrary"),
                                             vmem_limit_bytes=VMEM_LIMIT),
        name="outproj",
    )(att, rec, x, wo_b, g1, n2g, sc2, sh2, wr_t, br)


def _sc_gather_rows(table, idx):
    n = idx.shape[0]
    workers = SC_CORES * SC_SUBCORES
    per_worker = n // workers
    assert n % (workers * SC_CHUNK) == 0, "index count must split evenly into per-subcore chunks"
    n_chunks = per_worker // SC_CHUNK
    mesh = plsc.VectorSubcoreMesh(core_axis_name="c", subcore_axis_name="s",
                                  num_cores=SC_CORES, num_subcores=SC_SUBCORES)

    @functools.partial(
        pl.kernel, mesh=mesh,
        out_type=jax.ShapeDtypeStruct((n,) + table.shape[1:], table.dtype),
        scratch_types=[pltpu.VMEM((SC_CHUNK,), jnp.int32),
                       pltpu.VMEM((SC_CHUNK,) + table.shape[1:], table.dtype),
                       pltpu.SemaphoreType.DMA],
        name="sc_gather_rows",
    )
    def gather(table_hbm, idx_hbm, out_hbm, idx_v, rows_v, sem):
        base = (lax.axis_index("s") * SC_CORES + lax.axis_index("c")) * per_worker

        @pl.loop(0, n_chunks)
        def _(ci):
            off = pl.multiple_of(base + ci * SC_CHUNK, SC_CHUNK)
            pltpu.sync_copy(idx_hbm.at[pl.ds(off, SC_CHUNK)], idx_v)
            pltpu.async_copy(table_hbm.at[idx_v], rows_v, sem).wait()
            pltpu.sync_copy(rows_v, out_hbm.at[pl.ds(off, SC_CHUNK)])

    return gather(table, idx)


def _sc_scatter_rows(src, pos, n_out):
    n_tok = src.shape[0]
    n_k = pos.shape[1]
    workers = SC_CORES * SC_SUBCORES
    per_worker = n_tok // workers
    assert n_tok % (workers * SC_CHUNK) == 0, "token count must split evenly into per-subcore chunks"
    n_chunks = per_worker // SC_CHUNK
    mesh = plsc.VectorSubcoreMesh(core_axis_name="c", subcore_axis_name="s",
                                  num_cores=SC_CORES, num_subcores=SC_SUBCORES)

    @functools.partial(
        pl.kernel, mesh=mesh,
        out_type=jax.ShapeDtypeStruct((n_out,) + src.shape[1:], src.dtype),
        scratch_types=[pltpu.VMEM((n_k, SC_CHUNK), jnp.int32),
                       pltpu.VMEM((SC_CHUNK,) + src.shape[1:], src.dtype),
                       pltpu.SemaphoreType.DMA],
        name="sc_scatter_rows",
    )
    def scatter(src_hbm, pos_hbm, out_hbm, idx_v, rows_v, sem):
        base = (lax.axis_index("s") * SC_CORES + lax.axis_index("c")) * n_chunks

        @pl.loop(0, n_chunks)
        def _(ci):
            chunk = base + ci
            pltpu.sync_copy(pos_hbm.at[chunk], idx_v)
            pltpu.sync_copy(src_hbm.at[pl.ds(pl.multiple_of(chunk * SC_CHUNK, SC_CHUNK), SC_CHUNK)], rows_v)
            copies = [pltpu.async_copy(rows_v, out_hbm.at[idx_v.at[k]], sem) for k in range(n_k)]
            for cp in copies:
                cp.wait()

    return scatter(src, pos)


def _moe_kernel(be_ref, nv_ref, nused_ref, x_ref, w1_ref, b1_ref, w2_ref, b2_ref, y_ref, w1b_ref, w2b_ref,
                *, d_ff):
    i = pl.program_id(0)
    rows = SUBLANES

    @pl.when(i < nused_ref[0])
    def _():
        @pl.when((i == 0) | (be_ref[i] != be_ref[jnp.maximum(i - 1, 0)]))
        def _():
            w1b_ref[...] = w1_ref[0].astype(BF16)
            w2b_ref[...] = w2_ref[0].astype(BF16)

        x = jnp.concatenate([x_ref[pl.ds(c, EB, stride=rows), :] for c in range(rows)], axis=1)
        valid = lax.broadcasted_iota(jnp.int32, (EB, 1), 0) < nv_ref[i]
        x = jnp.where(valid, x, 0.0).astype(BF16)
        hu = _dot(x, w1b_ref[...]) + b1_ref[0]
        glu = jnp.minimum(hu[:, :d_ff], SWIGLU_LIMIT)
        lin = jnp.clip(hu[:, d_ff:], -SWIGLU_LIMIT, SWIGLU_LIMIT)
        act = glu * jax.nn.sigmoid(SWIGLU_ALPHA * glu) * (lin + 1.0)
        y = _dot(act.astype(BF16), w2b_ref[...]) + b2_ref[0]
        for c in range(rows):
            y_ref[pl.ds(c, EB, stride=rows), :] = y[:, c * LANES:(c + 1) * LANES]

    @pl.when(i >= nused_ref[0])
    def _():
        y_ref[...] = jnp.zeros_like(y_ref)


def _moe(block_e, nv, n_used, xs2d, w1, b1, w2, b2):
    n_blocks = block_e.shape[0]
    n_exp, D, ff2 = w1.shape
    d_ff = ff2 // 2
    rows = D // LANES
    assert rows == SUBLANES, "a token row must fill exactly one (8,128) f32 tile"
    kern = functools.partial(_moe_kernel, d_ff=d_ff)
    grid_spec = pltpu.PrefetchScalarGridSpec(
        num_scalar_prefetch=3,
        grid=(n_blocks,),
        in_specs=[pl.BlockSpec((EB * rows, LANES), lambda i, be, nv, nu: (i, 0)),
                  pl.BlockSpec((1, D, ff2), lambda i, be, nv, nu: (be[i], 0, 0)),
                  pl.BlockSpec((1, 1, ff2), lambda i, be, nv, nu: (be[i], 0, 0)),
                  pl.BlockSpec((1, d_ff, D), lambda i, be, nv, nu: (be[i], 0, 0)),
                  pl.BlockSpec((1, 1, D), lambda i, be, nv, nu: (be[i], 0, 0))],
        out_specs=pl.BlockSpec((EB * rows, LANES), lambda i, be, nv, nu: (i, 0)),
        scratch_shapes=[pltpu.VMEM((D, ff2), BF16), pltpu.VMEM((d_ff, D), BF16)],
    )
    return pl.pallas_call(
        kern,
        grid_spec=grid_spec,
        out_shape=jax.ShapeDtypeStruct((n_blocks * EB * rows, LANES), F32),
        compiler_params=pltpu.CompilerParams(dimension_semantics=("arbitrary",),
                                             vmem_limit_bytes=VMEM_LIMIT),
        name="moe",
    )(block_e, nv, n_used, xs2d, w1, b1, w2, b2)


def _combine_kernel(y_ref, gate_ref, x1_ref, g2_ref, *rest):
    o_ref = rest[-1]
    tm = x1_ref.shape[1]
    diag = (lax.broadcasted_iota(jnp.int32, (tm, tm), 0) == lax.broadcasted_iota(jnp.int32, (tm, tm), 1))
    moe = jnp.zeros(x1_ref.shape[1:], F32)
    for k in range(TOP_K):
        yk = jnp.concatenate([y_ref[k, pl.ds(c, tm, stride=SUBLANES), :] for c in range(SUBLANES)], axis=1)
        gate_col = jnp.sum(jnp.where(diag, gate_ref[0, k:k + 1, :], 0.0), axis=1, keepdims=True)
        moe = moe + gate_col * yk
    o_ref[0] = x1_ref[0] + g2_ref[0] * moe


def _combine(y4, gates, x1, g2, b0, n_batch, out_prev):
    B, S, D = x1.shape
    rows = D // LANES
    tm = min(TM_COMB, S)
    ns = S // tm
    in_specs = [pl.BlockSpec((TOP_K, tm * rows, LANES), lambda b, i: (0, b * ns + i, 0)),
                pl.BlockSpec((1, TOP_K, tm), lambda b, i: (b, 0, i)),
                pl.BlockSpec((1, tm, D), lambda b, i: (b, i, 0)),
                pl.BlockSpec((1, 1, D), lambda b, i: (b + b0, 0, 0))]
    args = [y4.reshape(TOP_K, B * S * rows, LANES), gates, x1, g2]
    aliases = {}
    if out_prev is not None:
        in_specs.append(pl.BlockSpec(memory_space=pl.ANY))
        args.append(out_prev)
        aliases = {len(args) - 1: 0}
    return pl.pallas_call(
        _combine_kernel,
        grid=(B, ns),
        in_specs=in_specs,
        out_specs=pl.BlockSpec((1, tm, D), lambda b, i: (b + b0, i, 0)),
        out_shape=jax.ShapeDtypeStruct((n_batch, S, D), F32),
        input_output_aliases=aliases,
        compiler_params=pltpu.CompilerParams(dimension_semantics=("arbitrary", "arbitrary"),
                                             vmem_limit_bytes=VMEM_LIMIT),
        name="combine",
    )(*args)


def _route(top_idx, rank, counts):
    B, K, S = top_idx.shape
    T = B * S
    n_exp = counts.shape[0]
    nblk = (counts + EB - 1) // EB
    bend = jnp.cumsum(nblk)
    bstart = bend - nblk
    n_blocks = -(-(T * K) // EB) + n_exp
    blk = jnp.arange(n_blocks, dtype=jnp.int32)
    block_e = jnp.minimum(jnp.sum(blk[:, None] >= bend[None, :], axis=1), n_exp - 1).astype(jnp.int32)
    nv = jnp.where(blk < bend[-1], jnp.clip(counts[block_e] - (blk - bstart[block_e]) * EB, 0, EB), 0)
    start_row = jnp.sum(jnp.where(top_idx[..., None] == jnp.arange(n_exp, dtype=jnp.int32),
                                  (bstart * EB).astype(jnp.int32), 0), axis=-1)
    pos = (start_row + rank).transpose(1, 0, 2).reshape(K, T).astype(jnp.int32)
    return block_e, nv.astype(jnp.int32), bend[-1:].astype(jnp.int32), pos


def _lambda_init(layer_idx):
    return 0.8 - 0.6 * math.exp(-0.3 * layer_idx)


def _block_diag(w):
    n, c, d = w.shape
    eye = jnp.eye(n, dtype=w.dtype)
    return (eye[:, None, :, None] * w[:, :, None, :]).reshape(n * c, n * d)


def kernel(x, c, w_ada, b_ada, norm1_g, w_in, q_norm_g, k_norm_g, lambda_q1, lambda_k1, lambda_q2, lambda_k2, subln_g, conv_w, conv_b, w_rg_a, b_rg_a, w_rg_x, b_rg_x, lru_lambda, w_out, norm2_g, w_router, b_router, w_moe1, b_moe1, w_moe2, b_moe2):
    B, S, D = x.shape
    depth = w_ada.shape[0]
    lru_w = conv_w.shape[2]
    attn_w = (w_in.shape[2] - 2 * lru_w) // 3
    n_exp = w_router.shape[2]
    n_heads = attn_w // HEAD_DIM
    assert attn_w % LANES == 0 and S % CHUNK == 0

    half = HEAD_DIM // 2
    inv = ROPE_THETA ** (-jnp.arange(half, dtype=F32) / half)
    ang = jnp.arange(S, dtype=jnp.int32).astype(F32)[:, None] * inv[None, :]
    cos_t = jnp.tile(jnp.concatenate([jnp.cos(ang), jnp.cos(ang)], axis=1), (1, n_heads))
    sin_t = jnp.tile(jnp.concatenate([-jnp.sin(ang), jnp.sin(ang)], axis=1), (1, n_heads))
    group = jnp.arange(attn_w, dtype=jnp.int32) // HEAD_DIM
    gsum = (group[:, None] == group[None, :]).astype(BF16)

    for l in range(depth):
        mod = _ada(c, w_ada[l], b_ada[l])
        sh1, sc1, g1, sh2, sc2, g2 = [m.reshape(B, 1, D) for m in jnp.split(mod, 6, axis=-1)]

        qt, k, vt, xr, gr = _inproj(
            x, norm1_g[l].reshape(1, D), sc1, sh1, w_in[l].astype(BF16), gsum,
            jnp.tile(q_norm_g[l], n_heads).reshape(1, attn_w), jnp.tile(k_norm_g[l], n_heads).reshape(1, attn_w),
            cos_t, sin_t, attn_w, lru_w)

        score_bound = (HEAD_DIM ** 0.5 * LOG2_E * BF16_SLACK) * jnp.max(jnp.abs(q_norm_g[l])) * jnp.max(jnp.abs(k_norm_g[l]))
        bounded = (score_bound <= SCORE_BOUND).astype(jnp.int32).reshape(1)
        att = _attn(bounded, qt, k, vt, lambda_q1[l].reshape(1, -1), lambda_k1[l].reshape(1, -1),
                    lambda_q2[l].reshape(1, -1), lambda_k2[l].reshape(1, -1),
                    subln_g[l].reshape(-1, 1), _lambda_init(l))

        rec = _lru(xr, gr, conv_w[l], conv_b[l].reshape(1, lru_w),
                   _block_diag(w_rg_a[l]).astype(BF16), b_rg_a[l].reshape(1, lru_w),
                   _block_diag(w_rg_x[l]).astype(BF16), b_rg_x[l].reshape(1, lru_w),
                   lru_lambda[l].reshape(1, lru_w))

        n_groups = MOE_GROUPS if B % MOE_GROUPS == 0 else 1
        nb = B // n_groups
        T = nb * S
        rows = D // LANES
        wo_b = w_out[l].astype(BF16)
        x_in, x_out = x, None
        for grp in range(n_groups):
            b0 = grp * nb
            x1, hm2d, top_idx, gates, rank, counts = _outproj(
                att, rec, x_in, wo_b, g1, norm2_g[l].reshape(1, D), sc2, sh2,
                w_router[l].T, b_router[l].reshape(n_exp, 1), n_exp, b0, nb)
            block_e, nv, n_used, pos = _route(top_idx, rank, counts.reshape(n_exp))
            pos_chunks = pos.reshape(TOP_K, T // SC_CHUNK, SC_CHUNK).transpose(1, 0, 2)
            xs = _sc_scatter_rows(hm2d.reshape(T, rows, LANES), pos_chunks, block_e.shape[0] * EB)
            ys = _moe(block_e, nv, n_used, xs.reshape(-1, LANES),
                      w_moe1[l], b_moe1[l].reshape(n_exp, 1, -1), w_moe2[l], b_moe2[l].reshape(n_exp, 1, -1))
            y4 = _sc_gather_rows(ys.reshape(-1, rows, LANES), pos.reshape(TOP_K * T))
            x_out = _combine(y4, gates, x1, g2, b0, B, x_out)
        x = x_out
    return x
```

```python
import functools
import math

import jax
import jax.numpy as jnp
from jax import lax
from jax.experimental import pallas as pl
from jax.experimental.pallas import tpu as pltpu
from jax.experimental.pallas import tpu_sc as plsc

F32 = jnp.float32
BF16 = jnp.bfloat16

CHUNK = 64
HEAD_DIM = 64
ROPE_THETA = 10000.0
LRU_BLOCK = 64
CONV_W = 4
LRU_C = 8.0
TOP_K = 4
SWIGLU_ALPHA = 1.702
SWIGLU_LIMIT = 7.0
NORM_EPS = 1e-6
SUBLN_EPS = 1e-5
LOG2_E = math.log2(math.e)
SCORE_BOUND = 60.0
BF16_SLACK = 1.02

LANES = 128
SUBLANES = 8
VMEM_LIMIT = 56 * 1024 * 1024
SC_CORES = 2
SC_SUBCORES = 16
SC_CHUNK = 64

TM_PROJ = 512
TQ = 512
KV_WIDE = 4
TT = 256
EB = 512
TM_COMB = 512
MOE_GROUPS = 2


def _split_bf16(a):
    hi = a.astype(BF16)
    lo = (a - hi.astype(F32)).astype(BF16)
    return hi, lo


def _dot(a, b):
    return jnp.dot(a, b, preferred_element_type=F32)


def _dot3(a, b):
    ah, al = _split_bf16(a)
    bh, bl = _split_bf16(b)
    return _dot(ah, bh) + (_dot(ah, bl) + _dot(al, bh))


def _ada_kernel(c_ref, w_ref, b_ref, o_ref):
    c = c_ref[...]
    s = c * jax.nn.sigmoid(c)
    o_ref[...] = _dot3(s, w_ref[...]) + b_ref[...]


def _ada(c, w_ada, b_ada):
    B, D = c.shape
    N = w_ada.shape[1]
    tn = D
    return pl.pallas_call(
        _ada_kernel,
        grid=(N // tn,),
        in_specs=[pl.BlockSpec((B, D), lambda j: (0, 0)),
                  pl.BlockSpec((D, tn), lambda j: (0, j)),
                  pl.BlockSpec((1, tn), lambda j: (0, j))],
        out_specs=pl.BlockSpec((B, tn), lambda j: (0, j)),
        out_shape=jax.ShapeDtypeStruct((B, N), F32),
        compiler_params=pltpu.CompilerParams(dimension_semantics=("arbitrary",),
                                             vmem_limit_bytes=VMEM_LIMIT),
        name="ada",
    )(c, w_ada, b_ada.reshape(1, N))


def _inproj_kernel(x_ref, g_ref, sc_ref, sh_ref, w_ref, gsum_ref, qg_ref, kg_ref, cos_ref, sin_ref,
                   qt_ref, k_ref, vt_ref, xr_ref, gr_ref, *, attn_w, lru_w):
    x = x_ref[0]
    ms = jnp.mean(x * x, axis=-1, keepdims=True)
    hn = x * lax.rsqrt(ms + NORM_EPS) * g_ref[...]
    hn = (hn * (1.0 + sc_ref[0]) + sh_ref[0]).astype(BF16)

    cos = cos_ref[...]
    sin = sin_ref[...]
    gsum = gsum_ref[...]
    lane = lax.broadcasted_iota(jnp.int32, cos.shape, 1)
    first_half = (lane % HEAD_DIM) < (HEAD_DIM // 2)

    def head_norm_rope(t, gain):
        ssum = _dot((t * t).astype(BF16), gsum)
        t = t * lax.rsqrt(ssum * (1.0 / HEAD_DIM) + NORM_EPS) * gain
        half = HEAD_DIM // 2
        partner = jnp.where(first_half, pltpu.roll(t, attn_w - half, 1), pltpu.roll(t, half, 1))
        return t * cos + partner * sin

    q = head_norm_rope(_dot(hn, w_ref[:, 0:attn_w]), qg_ref[...]) * (HEAD_DIM ** -0.5 * LOG2_E)
    for h in range(attn_w // LANES):
        qt_ref[0, h] = q[:, h * LANES:(h + 1) * LANES].T.astype(BF16)

    k = head_norm_rope(_dot(hn, w_ref[:, attn_w:2 * attn_w]), kg_ref[...])
    k_ref[0] = k.astype(BF16)

    v = _dot(hn, w_ref[:, 2 * attn_w:3 * attn_w])
    for h in range(attn_w // LANES):
        vt_ref[0, h] = v[:, h * LANES:(h + 1) * LANES].T.astype(BF16)
    xr_ref[0] = _dot(hn, w_ref[:, 3 * attn_w:3 * attn_w + lru_w])
    gr_ref[0] = _dot(hn, w_ref[:, 3 * attn_w + lru_w:3 * attn_w + 2 * lru_w])


def _inproj(x, g1n, sc1, sh1, w_in_b, gsum, qg, kg, cos_t, sin_t, attn_w, lru_w):
    B, S, D = x.shape
    tm = min(TM_PROJ, S)
    in_w = w_in_b.shape[1]
    nh = attn_w // LANES
    kern = functools.partial(_inproj_kernel, attn_w=attn_w, lru_w=lru_w)
    return pl.pallas_call(
        kern,
        grid=(S // tm, B),
        in_specs=[pl.BlockSpec((1, tm, D), lambda i, b: (b, i, 0)),
                  pl.BlockSpec((1, D), lambda i, b: (0, 0)),
                  pl.BlockSpec((1, 1, D), lambda i, b: (b, 0, 0)),
                  pl.BlockSpec((1, 1, D), lambda i, b: (b, 0, 0)),
                  pl.BlockSpec((D, in_w), lambda i, b: (0, 0)),
                  pl.BlockSpec((attn_w, attn_w), lambda i, b: (0, 0)),
                  pl.BlockSpec((1, attn_w), lambda i, b: (0, 0)),
                  pl.BlockSpec((1, attn_w), lambda i, b: (0, 0)),
                  pl.BlockSpec((tm, attn_w), lambda i, b: (i, 0)),
                  pl.BlockSpec((tm, attn_w), lambda i, b: (i, 0))],
        out_specs=[pl.BlockSpec((1, nh, LANES, tm), lambda i, b: (b, 0, 0, i)),
                   pl.BlockSpec((1, tm, attn_w), lambda i, b: (b, i, 0)),
                   pl.BlockSpec((1, nh, LANES, tm), lambda i, b: (b, 0, 0, i)),
                   pl.BlockSpec((1, tm, lru_w), lambda i, b: (b, i, 0)),
                   pl.BlockSpec((1, tm, lru_w), lambda i, b: (b, i, 0))],
        out_shape=[jax.ShapeDtypeStruct((B, nh, LANES, S), BF16),
                   jax.ShapeDtypeStruct((B, S, attn_w), BF16),
                   jax.ShapeDtypeStruct((B, nh, LANES, S), BF16),
                   jax.ShapeDtypeStruct((B, S, lru_w), F32),
                   jax.ShapeDtypeStruct((B, S, lru_w), F32)],
        compiler_params=pltpu.CompilerParams(dimension_semantics=("arbitrary", "arbitrary"),
                                             vmem_limit_bytes=VMEM_LIMIT),
        name="inproj",
    )(x, g1n, sc1, sh1, w_in_b, gsum, qg, kg, cos_t, sin_t)


def _attn_kernel(bounded_ref, qt_ref, k_ref, vt_ref, vis_ref, lq1_ref, lk1_ref, lq2_ref, lk2_ref, sg_ref, o_ref,
                 *, tq, wide, lam_init):
    qi = pl.program_id(2)
    qt = qt_ref[0, 0]
    row = lax.broadcasted_iota(jnp.int32, qt.shape, 0)
    zero = jnp.zeros_like(qt)
    qs = jnp.concatenate([jnp.where(row < HEAD_DIM, qt, zero),
                          jnp.where(row >= HEAD_DIM, qt, zero)], axis=1)

    def scores(start, size):
        kk = k_ref[0, pl.ds(pl.multiple_of(start, tq), size), :]
        return _dot(kk, qs)

    def values(start, size, p):
        vt = vt_ref[0, 0, :, pl.ds(pl.multiple_of(start, tq), size)]
        return _dot(vt, p)

    zeros = (jnp.zeros((1, 2 * tq), F32), jnp.zeros((LANES, 2 * tq), F32))

    def bounded_path():
        def tile(start, size, carry, vis=None):
            l, acc = carry
            p = jnp.exp2(scores(start, size))
            if vis is not None:
                p = p * vis
            return l + jnp.sum(p, axis=0, keepdims=True), acc + values(start, size, p.astype(BF16))

        carry = lax.fori_loop(0, qi // wide, lambda j, c: tile(j * (wide * tq), wide * tq, c), zeros)
        start = (qi // wide) * (wide * tq)
        piece = wide // 2
        while piece >= 1:
            take = ((qi % wide) & piece) != 0
            carry = lax.cond(take, lambda c, s=start, n=piece * tq: tile(s, n, c), lambda c: c, carry)
            start = start + jnp.where(take, piece * tq, 0)
            piece //= 2
        return tile(qi * tq, tq, carry, vis_ref[...])

    def online_path():
        def update(j, carry, masked):
            m, l, acc = carry
            s = scores(j * tq, tq)
            if masked:
                s = jnp.where(vis_ref[...] > 0.0, s, -jnp.inf)
            m_new = jnp.maximum(m, jnp.max(s, axis=0, keepdims=True))
            alpha = jnp.exp2(m - m_new)
            p = jnp.exp2(s - m_new)
            l = alpha * l + jnp.sum(p, axis=0, keepdims=True)
            return m_new, l, alpha * acc + values(j * tq, tq, p.astype(BF16))

        carry = (jnp.full((1, 2 * tq), -jnp.inf, F32),) + zeros
        carry = lax.fori_loop(0, qi, lambda j, c: update(j, c, False), carry)
        _, l, acc = update(qi, carry, True)
        return l, acc

    l, acc = lax.cond(bounded_ref[0] != 0, bounded_path, online_path)

    lam = (jnp.exp(jnp.sum(lq1_ref[...] * lk1_ref[...], axis=-1, keepdims=True))
           - jnp.exp(jnp.sum(lq2_ref[...] * lk2_ref[...], axis=-1, keepdims=True)) + lam_init)
    o = acc * (1.0 / l)
    o = o[:, :tq] - lam * o[:, tq:]
    ms = jnp.mean(o * o, axis=0, keepdims=True)
    o = o * lax.rsqrt(ms + SUBLN_EPS) * (sg_ref[...] * (1.0 - lam_init))
    o_ref[0] = o.T.astype(BF16)


def _attn(bounded, qt, k, vt, lq1, lk1, lq2, lk2, sg, lam_init):
    B, S, attn_w = k.shape
    nh = attn_w // LANES
    tq = min(TQ, S)
    wide = max(min(KV_WIDE, S // tq), 1)
    assert tq % CHUNK == 0 and wide & (wide - 1) == 0
    kern = functools.partial(_attn_kernel, tq=tq, wide=wide, lam_init=lam_init)
    vec = lambda n: pl.BlockSpec((1, n), lambda b, h, i: (0, 0))
    key_chunk = jnp.arange(tq, dtype=jnp.int32)[:, None] // CHUNK
    qry_chunk = (jnp.arange(2 * tq, dtype=jnp.int32)[None, :] % tq) // CHUNK
    vis = (key_chunk <= qry_chunk).astype(F32)
    return pl.pallas_call(
        kern,
        grid=(B, nh, S // tq),
        in_specs=[pl.BlockSpec(memory_space=pltpu.SMEM),
                  pl.BlockSpec((1, 1, LANES, tq), lambda b, h, i: (b, h, 0, i)),
                  pl.BlockSpec((1, S, LANES), lambda b, h, i: (b, 0, h)),
                  pl.BlockSpec((1, 1, LANES, S), lambda b, h, i: (b, h, 0, 0)),
                  pl.BlockSpec((tq, 2 * tq), lambda b, h, i: (0, 0)),
                  vec(HEAD_DIM), vec(HEAD_DIM), vec(HEAD_DIM), vec(HEAD_DIM),
                  pl.BlockSpec((LANES, 1), lambda b, h, i: (0, 0))],
        out_specs=pl.BlockSpec((1, tq, LANES), lambda b, h, i: (b, i, h)),
        out_shape=jax.ShapeDtypeStruct((B, S, attn_w), BF16),
        compiler_params=pltpu.CompilerParams(
            dimension_semantics=("arbitrary", "arbitrary", "arbitrary"), vmem_limit_bytes=VMEM_LIMIT),
        name="attn",
    )(bounded, qt, k, vt, vis, lq1, lk1, lq2, lk2, sg)


def _lru_kernel(xr_ref, gr_ref, cw_ref, cb_ref, wa_ref, ba_ref, wx_ref, bx_ref, lam_ref, o_ref,
                tail_ref, h_ref, *, tt):
    @pl.when(pl.program_id(1) == 0)
    def _():
        tail_ref[...] = jnp.zeros_like(tail_ref)
        h_ref[...] = jnp.zeros_like(h_ref)

    x = xr_ref[0]
    ext = jnp.concatenate([tail_ref[...], x], axis=0)
    xc = cb_ref[...] + cw_ref[CONV_W - 1:CONV_W, :] * x
    for j in range(CONV_W - 1):
        back = CONV_W - 1 - j
        xc = xc + cw_ref[j:j + 1, :] * ext[SUBLANES - back:SUBLANES - back + tt]
    tail_ref[...] = x[tt - SUBLANES:]

    xcb = xc.astype(BF16)
    r = jax.nn.sigmoid(_dot(xcb, wa_ref[...]) + ba_ref[...])
    i = jax.nn.sigmoid(_dot(xcb, wx_ref[...]) + bx_ref[...])
    nl = -lam_ref[...]
    softplus = jnp.maximum(nl, 0.0) + jnp.log(1.0 + jnp.exp(-jnp.abs(nl)))
    log_a = (-LRU_C) * r * softplus
    a = jnp.exp(log_a)
    th = jnp.tanh(log_a)
    u = jnp.sqrt((-2.0) * th / (1.0 - th)) * (i * xc)

    n_groups = tt // SUBLANES
    a = a.reshape(n_groups, SUBLANES, a.shape[1])
    u = u.reshape(a.shape)
    in_group = lax.broadcasted_iota(jnp.int32, a.shape, 1)
    d = 1
    while d < SUBLANES:
        keep = in_group >= d
        u = u + a * jnp.where(keep, pltpu.roll(u, d, 1), 0.0)
        a = a * jnp.where(keep, pltpu.roll(a, d, 1), 1.0)
        d *= 2
    carry = h_ref[0:1, :]
    groups = []
    for g in range(n_groups):
        hg = a[g] * carry + u[g]
        groups.append(hg)
        carry = hg[SUBLANES - 1:SUBLANES, :]
    h = jnp.concatenate(groups, axis=0)
    h_ref[...] = jnp.broadcast_to(carry, h_ref.shape)

    g = gr_ref[0]
    gelu = 0.5 * g * (1.0 + jnp.tanh(math.sqrt(2.0 / math.pi) * (g + 0.044715 * (g * g * g))))
    o_ref[0] = (h * gelu).astype(BF16)


def _lru(xr, gr, conv_w, conv_b, wa_bd, b_a, wx_bd, b_x, lam):
    B, S, W = xr.shape
    tt = min(TT, S)
    kern = functools.partial(_lru_kernel, tt=tt)
    vec = pl.BlockSpec((1, W), lambda b, i: (0, 0))
    mat = pl.BlockSpec((W, W), lambda b, i: (0, 0))
    return pl.pallas_call(
        kern,
        grid=(B, S // tt),
        in_specs=[pl.BlockSpec((1, tt, W), lambda b, i: (b, i, 0)),
                  pl.BlockSpec((1, tt, W), lambda b, i: (b, i, 0)),
                  pl.BlockSpec((CONV_W, W), lambda b, i: (0, 0)),
                  vec, mat, vec, mat, vec, vec],
        out_specs=pl.BlockSpec((1, tt, W), lambda b, i: (b, i, 0)),
        out_shape=jax.ShapeDtypeStruct((B, S, W), BF16),
        scratch_shapes=[pltpu.VMEM((SUBLANES, W), F32), pltpu.VMEM((SUBLANES, W), F32)],
        compiler_params=pltpu.CompilerParams(dimension_semantics=("arbitrary", "arbitrary"),
                                             vmem_limit_bytes=VMEM_LIMIT),
        name="lru",
    )(xr, gr, conv_w, conv_b, wa_bd, b_a, wx_bd, b_x, lam)


def _outproj_kernel(att_ref, rec_ref, x_ref, wo_ref, g1_ref, n2_ref, sc_ref, sh_ref, wr_ref, br_ref,
                    x1_ref, hm_ref, idx_ref, gate_ref, rank_ref, count_ref, cnt_ref, *, attn_w, n_exp):
    tm = x_ref.shape[1]
    mix = _dot(att_ref[0], wo_ref[0:attn_w, :]) + _dot(rec_ref[0], wo_ref[attn_w:, :])
    x1 = x_ref[0] + g1_ref[0] * mix
    x1_ref[0] = x1
    ms = jnp.mean(x1 * x1, axis=-1, keepdims=True)
    hm = x1 * lax.rsqrt(ms + NORM_EPS) * n2_ref[...]
    hm = hm * (1.0 + sc_ref[0]) + sh_ref[0]
    for c in range(hm.shape[1] // LANES):
        hm_ref[pl.ds(c, tm, stride=SUBLANES), :] = hm[:, c * LANES:(c + 1) * LANES]

    hh, hl = _split_bf16(hm)
    wh, wl = _split_bf16(wr_ref[...])
    nt = (((1,), (1,)), ((), ()))
    logits = (lax.dot_general(wh, hh, nt, preferred_element_type=F32)
              + (lax.dot_general(wh, hl, nt, preferred_element_type=F32)
                 + lax.dot_general(wl, hh, nt, preferred_element_type=F32))) + br_ref[...]
    eidx = lax.broadcasted_iota(jnp.int32, logits.shape, 0)
    vals, idxs = [], []
    for _ in range(TOP_K):
        m = jnp.max(logits, axis=0, keepdims=True)
        am = jnp.min(jnp.where(logits == m, eidx, n_exp), axis=0, keepdims=True)
        vals.append(m)
        idxs.append(am)
        logits = jnp.where(eidx == am, -jnp.inf, logits)
    ex = [jnp.exp(v - vals[0]) for v in vals]
    den = ex[0] + ex[1] + ex[2] + ex[3]
    idx_ref[0] = jnp.concatenate(idxs, axis=0)
    gate_ref[0] = jnp.concatenate([e / den for e in ex], axis=0)

    @pl.when((pl.program_id(0) == 0) & (pl.program_id(1) == 0))
    def _():
        cnt_ref[...] = jnp.zeros_like(cnt_ref)

    hits = [eidx == am for am in idxs]
    sel = (hits[0] | hits[1] | hits[2] | hits[3]).astype(F32)
    earlier = (lax.broadcasted_iota(jnp.int32, (tm, tm), 0)
               < lax.broadcasted_iota(jnp.int32, (tm, tm), 1)).astype(BF16)
    before = _dot(sel.astype(BF16), earlier) + cnt_ref[...]
    rank_ref[0] = jnp.concatenate(
        [jnp.sum(jnp.where(h, before, 0.0), axis=0, keepdims=True) for h in hits], axis=0).astype(jnp.int32)
    cnt_ref[...] += jnp.sum(sel, axis=1, keepdims=True)
    count_ref[...] = cnt_ref[...].astype(jnp.int32)


def _outproj(att, rec, x, wo_b, g1, n2g, sc2, sh2, wr_t, br, n_exp, b0, B):
    _, S, D = x.shape
    attn_w = att.shape[2]
    lru_w = rec.shape[2]
    tm = min(TM_PROJ, S)
    ns = S // tm
    rows = D // LANES
    kern = functools.partial(_outproj_kernel, attn_w=attn_w, n_exp=n_exp)
    bvec = pl.BlockSpec((1, 1, D), lambda b, i: (b + b0, 0, 0))
    return pl.pallas_call(
        kern,
        grid=(B, ns),
        in_specs=[pl.BlockSpec((1, tm, attn_w), lambda b, i: (b + b0, i, 0)),
                  pl.BlockSpec((1, tm, lru_w), lambda b, i: (b + b0, i, 0)),
                  pl.BlockSpec((1, tm, D), lambda b, i: (b + b0, i, 0)),
                  pl.BlockSpec((attn_w + lru_w, D), lambda b, i: (0, 0)),
                  bvec,
                  pl.BlockSpec((1, D), lambda b, i: (0, 0)),
                  bvec, bvec,
                  pl.BlockSpec((n_exp, D), lambda b, i: (0, 0)),
                  pl.BlockSpec((n_exp, 1), lambda b, i: (0, 0))],
        out_specs=[pl.BlockSpec((1, tm, D), lambda b, i: (b, i, 0)),
                   pl.BlockSpec((tm * rows, LANES), lambda b, i: (b * ns + i, 0)),
                   pl.BlockSpec((1, TOP_K, tm), lambda b, i: (b, 0, i)),
                   pl.BlockSpec((1, TOP_K, tm), lambda b, i: (b, 0, i)),
                   pl.BlockSpec((1, TOP_K, tm), lambda b, i: (b, 0, i)),
                   pl.BlockSpec((n_exp, 1), lambda b, i: (0, 0))],
        out_shape=[jax.ShapeDtypeStruct((B, S, D), F32),
                   jax.ShapeDtypeStruct((B * S * rows, LANES), F32),
                   jax.ShapeDtypeStruct((B, TOP_K, S), jnp.int32),
                   jax.ShapeDtypeStruct((B, TOP_K, S), F32),
                   jax.ShapeDtypeStruct((B, TOP_K, S), jnp.int32),
                   jax.ShapeDtypeStruct((n_exp, 1), jnp.int32)],
        scratch_shapes=[pltpu.VMEM((n_exp, 1), F32)],
        compiler_params=pltpu.CompilerParams(dimension_semantics=("arbitrary", "arbitrary"),
                                             vmem_limit_bytes=VMEM_LIMIT),
        name="outproj",
    )(att, rec, x, wo_b, g1, n2g, sc2, sh2, wr_t, br)


def _sc_gather_rows(table, idx):
    n = idx.shape[0]
    workers = SC_CORES * SC_SUBCORES
    per_worker = n // workers
    assert n % (workers * SC_CHUNK) == 0, "index count must split evenly into per-subcore chunks"
    n_chunks = per_worker // SC_CHUNK
    mesh = plsc.VectorSubcoreMesh(core_axis_name="c", subcore_axis_name="s",
                                  num_cores=SC_CORES, num_subcores=SC_SUBCORES)

    @functools.partial(
        pl.kernel, mesh=mesh,
        out_type=jax.ShapeDtypeStruct((n,) + table.shape[1:], table.dtype),
        scratch_types=[pltpu.VMEM((SC_CHUNK,), jnp.int32),
                       pltpu.VMEM((SC_CHUNK,) + table.shape[1:], table.dtype),
                       pltpu.SemaphoreType.DMA],
        name="sc_gather_rows",
    )
    def gather(table_hbm, idx_hbm, out_hbm, idx_v, rows_v, sem):
        base = (lax.axis_index("s") * SC_CORES + lax.axis_index("c")) * per_worker

        @pl.loop(0, n_chunks)
        def _(ci):
            off = pl.multiple_of(base + ci * SC_CHUNK, SC_CHUNK)
            pltpu.sync_copy(idx_hbm.at[pl.ds(off, SC_CHUNK)], idx_v)
            pltpu.async_copy(table_hbm.at[idx_v], rows_v, sem).wait()
            pltpu.sync_copy(rows_v, out_hbm.at[pl.ds(off, SC_CHUNK)])

    return gather(table, idx)


def _sc_scatter_rows(src, pos, n_out):
    n_tok = src.shape[0]
    n_k = pos.shape[1]
    workers = SC_CORES * SC_SUBCORES
    per_worker = n_tok // workers
    assert n_tok % (workers * SC_CHUNK) == 0, "token count must split evenly into per-subcore chunks"
    n_chunks = per_worker // SC_CHUNK
    mesh = plsc.VectorSubcoreMesh(core_axis_name="c", subcore_axis_name="s",
                                  num_cores=SC_CORES, num_subcores=SC_SUBCORES)

    @functools.partial(
        pl.kernel, mesh=mesh,
        out_type=jax.ShapeDtypeStruct((n_out,) + src.shape[1:], src.dtype),
        scratch_types=[pltpu.VMEM((n_k, SC_CHUNK), jnp.int32),
                       pltpu.VMEM((SC_CHUNK,) + src.shape[1:], src.dtype),
                       pltpu.SemaphoreType.DMA],
        name="sc_scatter_rows",
    )
    def scatter(src_hbm, pos_hbm, out_hbm, idx_v, rows_v, sem):
        base = (lax.axis_index("s") * SC_CORES + lax.axis_index("c")) * n_chunks

        @pl.loop(0, n_chunks)
        def _(ci):
            chunk = base + ci
            pltpu.sync_copy(pos_hbm.at[chunk], idx_v)
            pltpu.sync_copy(src_hbm.at[pl.ds(pl.multiple_of(chunk * SC_CHUNK, SC_CHUNK), SC_CHUNK)], rows_v)
            copies = [pltpu.async_copy(rows_v, out_hbm.at[idx_v.at[k]], sem) for k in range(n_k)]
            for cp in copies:
                cp.wait()

    return scatter(src, pos)


def _moe_kernel(be_ref, nv_ref, nused_ref, x_ref, w1_ref, b1_ref, w2_ref, b2_ref, y_ref, w1b_ref, w2b_ref,
                *, d_ff):
    i = pl.program_id(0)
    rows = SUBLANES

    @pl.when(i < nused_ref[0])
    def _():
        @pl.when((i == 0) | (be_ref[i] != be_ref[jnp.maximum(i - 1, 0)]))
        def _():
            w1b_ref[...] = w1_ref[0].astype(BF16)
            w2b_ref[...] = w2_ref[0].astype(BF16)

        x = jnp.concatenate([x_ref[pl.ds(c, EB, stride=rows), :] for c in range(rows)], axis=1)
        valid = lax.broadcasted_iota(jnp.int32, (EB, 1), 0) < nv_ref[i]
        x = jnp.where(valid, x, 0.0).astype(BF16)
        hu = _dot(x, w1b_ref[...]) + b1_ref[0]
        glu = jnp.minimum(hu[:, :d_ff], SWIGLU_LIMIT)
        lin = jnp.clip(hu[:, d_ff:], -SWIGLU_LIMIT, SWIGLU_LIMIT)
        act = glu * jax.nn.sigmoid(SWIGLU_ALPHA * glu) * (lin + 1.0)
        y = _dot(act.astype(BF16), w2b_ref[...]) + b2_ref[0]
        for c in range(rows):
            y_ref[pl.ds(c, EB, stride=rows), :] = y[:, c * LANES:(c + 1) * LANES]

    @pl.when(i >= nused_ref[0])
    def _():
        y_ref[...] = jnp.zeros_like(y_ref)


def _moe(block_e, nv, n_used, xs2d, w1, b1, w2, b2):
    n_blocks = block_e.shape[0]
    n_exp, D, ff2 = w1.shape
    d_ff = ff2 // 2
    rows = D // LANES
    assert rows == SUBLANES, "a token row must fill exactly one (8,128) f32 tile"
    kern = functools.partial(_moe_kernel, d_ff=d_ff)
    grid_spec = pltpu.PrefetchScalarGridSpec(
        num_scalar_prefetch=3,
        grid=(n_blocks,),
        in_specs=[pl.BlockSpec((EB * rows, LANES), lambda i, be, nv, nu: (i, 0)),
                  pl.BlockSpec((1, D, ff2), lambda i, be, nv, nu: (be[i], 0, 0)),
                  pl.BlockSpec((1, 1, ff2), lambda i, be, nv, nu: (be[i], 0, 0)),
                  pl.BlockSpec((1, d_ff, D), lambda i, be, nv, nu: (be[i], 0, 0)),
                  pl.BlockSpec((1, 1, D), lambda i, be, nv, nu: (be[i], 0, 0))],
        out_specs=pl.BlockSpec((EB * rows, LANES), lambda i, be, nv, nu: (i, 0)),
        scratch_shapes=[pltpu.VMEM((D, ff2), BF16), pltpu.VMEM((d_ff, D), BF16)],
    )
    return pl.pallas_call(
        kern,
        grid_spec=grid_spec,
        out_shape=jax.ShapeDtypeStruct((n_blocks * EB * rows, LANES), F32),
        compiler_params=pltpu.CompilerParams(dimension_semantics=("arbitrary",),
                                             vmem_limit_bytes=VMEM_LIMIT),
        name="moe",
    )(block_e, nv, n_used, xs2d, w1, b1, w2, b2)


def _combine_kernel(y_ref, gate_ref, x1_ref, g2_ref, *rest):
    o_ref = rest[-1]
    tm = x1_ref.shape[1]
    diag = (lax.broadcasted_iota(jnp.int32, (tm, tm), 0) == lax.broadcasted_iota(jnp.int32, (tm, tm), 1))
    moe = jnp.zeros(x1_ref.shape[1:], F32)
    for k in range(TOP_K):
        yk = jnp.concatenate([y_ref[k, pl.ds(c, tm, stride=SUBLANES), :] for c in range(SUBLANES)], axis=1)
        gate_col = jnp.sum(jnp.where(diag, gate_ref[0, k:k + 1, :], 0.0), axis=1, keepdims=True)
        moe = moe + gate_col * yk
    o_ref[0] = x1_ref[0] + g2_ref[0] * moe


def _combine(y4, gates, x1, g2, b0, n_batch, out_prev):
    B, S, D = x1.shape
    rows = D // LANES
    tm = min(TM_COMB, S)
    ns = S // tm
    in_specs = [pl.BlockSpec((TOP_K, tm * rows, LANES), lambda b, i: (0, b * ns + i, 0)),
                pl.BlockSpec((1, TOP_K, tm), lambda b, i: (b, 0, i)),
                pl.BlockSpec((1, tm, D), lambda b, i: (b, i, 0)),
                pl.BlockSpec((1, 1, D), lambda b, i: (b + b0, 0, 0))]
    args = [y4.reshape(TOP_K, B * S * rows, LANES), gates, x1, g2]
    aliases = {}
    if out_prev is not None:
        in_specs.append(pl.BlockSpec(memory_space=pl.ANY))
        args.append(out_prev)
        aliases = {len(args) - 1: 0}
    return pl.pallas_call(
        _combine_kernel,
        grid=(B, ns),
        in_specs=in_specs,
        out_specs=pl.BlockSpec((1, tm, D), lambda b, i: (b + b0, i, 0)),
        out_shape=jax.ShapeDtypeStruct((n_batch, S, D), F32),
        input_output_aliases=aliases,
        compiler_params=pltpu.CompilerParams(dimension_semantics=("arbitrary", "arbitrary"),
                                             vmem_limit_bytes=VMEM_LIMIT),
        name="combine",
    )(*args)


def _route(top_idx, rank, counts):
    B, K, S = top_idx.shape
    T = B * S
    n_exp = counts.shape[0]
    nblk = (counts + EB - 1) // EB
    bend = jnp.cumsum(nblk)
    bstart = bend - nblk
    n_blocks = -(-(T * K) // EB) + n_exp
    blk = jnp.arange(n_blocks, dtype=jnp.int32)
    block_e = jnp.minimum(jnp.sum(blk[:, None] >= bend[None, :], axis=1), n_exp - 1).astype(jnp.int32)
    nv = jnp.where(blk < bend[-1], jnp.clip(counts[block_e] - (blk - bstart[block_e]) * EB, 0, EB), 0)
    start_row = jnp.sum(jnp.where(top_idx[..., None] == jnp.arange(n_exp, dtype=jnp.int32),
                                  (bstart * EB).astype(jnp.int32), 0), axis=-1)
    pos = (start_row + rank).transpose(1, 0, 2).reshape(K, T).astype(jnp.int32)
    return block_e, nv.astype(jnp.int32), bend[-1:].astype(jnp.int32), pos


def _lambda_init(layer_idx):
    return 0.8 - 0.6 * math.exp(-0.3 * layer_idx)


def _block_diag(w):
    n, c, d = w.shape
    eye = jnp.eye(n, dtype=w.dtype)
    return (eye[:, None, :, None] * w[:, :, None, :]).reshape(n * c, n * d)


def kernel(x, c, w_ada, b_ada, norm1_g, w_in, q_norm_g, k_norm_g, lambda_q1, lambda_k1, lambda_q2, lambda_k2, subln_g, conv_w, conv_b, w_rg_a, b_rg_a, w_rg_x, b_rg_x, lru_lambda, w_out, norm2_g, w_router, b_router, w_moe1, b_moe1, w_moe2, b_moe2):
    B, S, D = x.shape
    depth = w_ada.shape[0]
    lru_w = conv_w.shape[2]
    attn_w = (w_in.shape[2] - 2 * lru_w) // 3
    n_exp = w_router.shape[2]
    n_heads = attn_w // HEAD_DIM
    assert attn_w % LANES == 0 and S % CHUNK == 0

    half = HEAD_DIM // 2
    inv = ROPE_THETA ** (-jnp.arange(half, dtype=F32) / half)
    ang = jnp.arange(S, dtype=jnp.int32).astype(F32)[:, None] * inv[None, :]
    cos_t = jnp.tile(jnp.concatenate([jnp.cos(ang), jnp.cos(ang)], axis=1), (1, n_heads))
    sin_t = jnp.tile(jnp.concatenate([-jnp.sin(ang), jnp.sin(ang)], axis=1), (1, n_heads))
    group = jnp.arange(attn_w, dtype=jnp.int32) // HEAD_DIM
    gsum = (group[:, None] == group[None, :]).astype(BF16)

    for l in range(depth):
        mod = _ada(c, w_ada[l], b_ada[l])
        sh1, sc1, g1, sh2, sc2, g2 = [m.reshape(B, 1, D) for m in jnp.split(mod, 6, axis=-1)]

        qt, k, vt, xr, gr = _inproj(
            x, norm1_g[l].reshape(1, D), sc1, sh1, w_in[l].astype(BF16), gsum,
            jnp.tile(q_norm_g[l], n_heads).reshape(1, attn_w), jnp.tile(k_norm_g[l], n_heads).reshape(1, attn_w),
            cos_t, sin_t, attn_w, lru_w)

        score_bound = (HEAD_DIM ** 0.5 * LOG2_E * BF16_SLACK) * jnp.max(jnp.abs(q_norm_g[l])) * jnp.max(jnp.abs(k_norm_g[l]))
        bounded = (score_bound <= SCORE_BOUND).astype(jnp.int32).reshape(1)
        att = _attn(bounded, qt, k, vt, lambda_q1[l].reshape(1, -1), lambda_k1[l].reshape(1, -1),
                    lambda_q2[l].reshape(1, -1), lambda_k2[l].reshape(1, -1),
                    subln_g[l].reshape(-1, 1), _lambda_init(l))

        rec = _lru(xr, gr, conv_w[l], conv_b[l].reshape(1, lru_w),
                   _block_diag(w_rg_a[l]).astype(BF16), b_rg_a[l].reshape(1, lru_w),
                   _block_diag(w_rg_x[l]).astype(BF16), b_rg_x[l].reshape(1, lru_w),
                   lru_lambda[l].reshape(1, lru_w))

        n_groups = MOE_GROUPS if B % MOE_GROUPS == 0 else 1
        nb = B // n_groups
        T = nb * S
        rows = D // LANES
        wo_b = w_out[l].astype(BF16)
        x_in, x_out = x, None
        for grp in range(n_groups):
            b0 = grp * nb
            x1, hm2d, top_idx, gates, rank, counts = _outproj(
                att, rec, x_in, wo_b, g1, norm2_g[l].reshape(1, D), sc2, sh2,
                w_router[l].T, b_router[l].reshape(n_exp, 1), n_exp, b0, nb)
            block_e, nv, n_used, pos = _route(top_idx, rank, counts.reshape(n_exp))
            pos_chunks = pos.reshape(TOP_K, T // SC_CHUNK, SC_CHUNK).transpose(1, 0, 2)
            xs = _sc_scatter_rows(hm2d.reshape(T, rows, LANES), pos_chunks, block_e.shape[0] * EB)
            ys = _moe(block_e, nv, n_used, xs.reshape(-1, LANES),
                      w_moe1[l], b_moe1[l].reshape(n_exp, 1, -1), w_moe2[l], b_moe2[l].reshape(n_exp, 1, -1))
            y4 = _sc_gather_rows(ys.reshape(-1, rows, LANES), pos.reshape(TOP_K * T))
            x_out = _combine(y4, gates, x1, g2, b0, B, x_out)
        x = x_out
    return x
```

```python
import functools
import math

import jax
import jax.numpy as jnp
from jax import lax
from jax.experimental import pallas as pl
from jax.experimental.pallas import tpu as pltpu
from jax.experimental.pallas import tpu_sc as plsc

F32 = jnp.float32
BF16 = jnp.bfloat16

CHUNK = 64
HEAD_DIM = 64
ROPE_THETA = 10000.0
LRU_BLOCK = 64
CONV_W = 4
LRU_C = 8.0
TOP_K = 4
SWIGLU_ALPHA = 1.702
SWIGLU_LIMIT = 7.0
NORM_EPS = 1e-6
SUBLN_EPS = 1e-5
LOG2_E = math.log2(math.e)
SCORE_BOUND = 60.0
BF16_SLACK = 1.02

LANES = 128
SUBLANES = 8
VMEM_LIMIT = 56 * 1024 * 1024
SC_CORES = 2
SC_SUBCORES = 16
SC_CHUNK = 64

TM_PROJ = 512
TQ = 512
KV_WIDE = 4
TT = 256
EB = 512
TM_COMB = 512
MOE_GROUPS = 2
COMBINE_PARTS = 2


def _split_bf16(a):
    hi = a.astype(BF16)
    lo = (a - hi.astype(F32)).astype(BF16)
    return hi, lo


def _dot(a, b):
    return jnp.dot(a, b, preferred_element_type=F32)


def _dot3(a, b):
    ah, al = _split_bf16(a)
    bh, bl = _split_bf16(b)
    return _dot(ah, bh) + (_dot(ah, bl) + _dot(al, bh))


def _ada_kernel(c_ref, w_ref, b_ref, o_ref):
    c = c_ref[...]
    s = c * jax.nn.sigmoid(c)
    o_ref[...] = _dot3(s, w_ref[...]) + b_ref[...]


def _ada(c, w_ada, b_ada):
    B, D = c.shape
    N = w_ada.shape[1]
    tn = D
    return pl.pallas_call(
        _ada_kernel,
        grid=(N // tn,),
        in_specs=[pl.BlockSpec((B, D), lambda j: (0, 0)),
                  pl.BlockSpec((D, tn), lambda j: (0, j)),
                  pl.BlockSpec((1, tn), lambda j: (0, j))],
        out_specs=pl.BlockSpec((B, tn), lambda j: (0, j)),
        out_shape=jax.ShapeDtypeStruct((B, N), F32),
        compiler_params=pltpu.CompilerParams(dimension_semantics=("arbitrary",),
                                             vmem_limit_bytes=VMEM_LIMIT),
        name="ada",
    )(c, w_ada, b_ada.reshape(1, N))


def _inproj_kernel(x_ref, g_ref, sc_ref, sh_ref, w_ref, gsum_ref, qg_ref, kg_ref, cos_ref, sin_ref,
                   qt_ref, k_ref, vt_ref, xr_ref, gr_ref, *, attn_w, lru_w):
    x = x_ref[0]
    ms = jnp.mean(x * x, axis=-1, keepdims=True)
    hn = x * lax.rsqrt(ms + NORM_EPS) * g_ref[...]
    hn = (hn * (1.0 + sc_ref[0]) + sh_ref[0]).astype(BF16)

    cos = cos_ref[...]
    sin = sin_ref[...]
    gsum = gsum_ref[...]
    lane = lax.broadcasted_iota(jnp.int32, cos.shape, 1)
    first_half = (lane % HEAD_DIM) < (HEAD_DIM // 2)

    def head_norm_rope(t, gain):
        ssum = _dot((t * t).astype(BF16), gsum)
        t = t * lax.rsqrt(ssum * (1.0 / HEAD_DIM) + NORM_EPS) * gain
        half = HEAD_DIM // 2
        partner = jnp.where(first_half, pltpu.roll(t, attn_w - half, 1), pltpu.roll(t, half, 1))
        return t * cos + partner * sin

    q = head_norm_rope(_dot(hn, w_ref[:, 0:attn_w]), qg_ref[...]) * (HEAD_DIM ** -0.5 * LOG2_E)
    for h in range(attn_w // LANES):
        qt_ref[0, h] = q[:, h * LANES:(h + 1) * LANES].T.astype(BF16)

    k = head_norm_rope(_dot(hn, w_ref[:, attn_w:2 * attn_w]), kg_ref[...])
    k_ref[0] = k.astype(BF16)

    v = _dot(hn, w_ref[:, 2 * attn_w:3 * attn_w])
    for h in range(attn_w // LANES):
        vt_ref[0, h] = v[:, h * LANES:(h + 1) * LANES].T.astype(BF16)
    xr_ref[0] = _dot(hn, w_ref[:, 3 * attn_w:3 * attn_w + lru_w])
    gr_ref[0] = _dot(hn, w_ref[:, 3 * attn_w + lru_w:3 * attn_w + 2 * lru_w])


def _inproj(x, g1n, sc1, sh1, w_in_b, gsum, qg, kg, cos_t, sin_t, attn_w, lru_w):
    B, S, D = x.shape
    tm = min(TM_PROJ, S)
    in_w = w_in_b.shape[1]
    nh = attn_w // LANES
    kern = functools.partial(_inproj_kernel, attn_w=attn_w, lru_w=lru_w)
    return pl.pallas_call(
        kern,
        grid=(S // tm, B),
        in_specs=[pl.BlockSpec((1, tm, D), lambda i, b: (b, i, 0)),
                  pl.BlockSpec((1, D), lambda i, b: (0, 0)),
                  pl.BlockSpec((1, 1, D), lambda i, b: (b, 0, 0)),
                  pl.BlockSpec((1, 1, D), lambda i, b: (b, 0, 0)),
                  pl.BlockSpec((D, in_w), lambda i, b: (0, 0)),
                  pl.BlockSpec((attn_w, attn_w), lambda i, b: (0, 0)),
                  pl.BlockSpec((1, attn_w), lambda i, b: (0, 0)),
                  pl.BlockSpec((1, attn_w), lambda i, b: (0, 0)),
                  pl.BlockSpec((tm, attn_w), lambda i, b: (i, 0)),
                  pl.BlockSpec((tm, attn_w), lambda i, b: (i, 0))],
        out_specs=[pl.BlockSpec((1, nh, LANES, tm), lambda i, b: (b, 0, 0, i)),
                   pl.BlockSpec((1, tm, attn_w), lambda i, b: (b, i, 0)),
                   pl.BlockSpec((1, nh, LANES, tm), lambda i, b: (b, 0, 0, i)),
                   pl.BlockSpec((1, tm, lru_w), lambda i, b: (b, i, 0)),
                   pl.BlockSpec((1, tm, lru_w), lambda i, b: (b, i, 0))],
        out_shape=[jax.ShapeDtypeStruct((B, nh, LANES, S), BF16),
                   jax.ShapeDtypeStruct((B, S, attn_w), BF16),
                   jax.ShapeDtypeStruct((B, nh, LANES, S), BF16),
                   jax.ShapeDtypeStruct((B, S, lru_w), F32),
                   jax.ShapeDtypeStruct((B, S, lru_w), F32)],
        compiler_params=pltpu.CompilerParams(dimension_semantics=("arbitrary", "arbitrary"),
                                             vmem_limit_bytes=VMEM_LIMIT),
        name="inproj",
    )(x, g1n, sc1, sh1, w_in_b, gsum, qg, kg, cos_t, sin_t)


def _attn_kernel(bounded_ref, qt_ref, k_ref, vt_ref, vis_ref, lq1_ref, lk1_ref, lq2_ref, lk2_ref, sg_ref, o_ref,
                 *, tq, wide, lam_init):
    qi = pl.program_id(2)
    qt = qt_ref[0, 0]
    row = lax.broadcasted_iota(jnp.int32, qt.shape, 0)
    zero = jnp.zeros_like(qt)
    qs = jnp.concatenate([jnp.where(row < HEAD_DIM, qt, zero),
                          jnp.where(row >= HEAD_DIM, qt, zero)], axis=1)

    def scores(start, size):
        kk = k_ref[0, pl.ds(pl.multiple_of(start, tq), size), :]
        return _dot(kk, qs)

    def values(start, size, p):
        vt = vt_ref[0, 0, :, pl.ds(pl.multiple_of(start, tq), size)]
        return _dot(vt, p)

    zeros = (jnp.zeros((1, 2 * tq), F32), jnp.zeros((LANES, 2 * tq), F32))

    def bounded_path():
        def tile(start, size, carry, vis=None):
            l, acc = carry
            p = jnp.exp2(scores(start, size))
            if vis is not None:
                p = p * vis
            return l + jnp.sum(p, axis=0, keepdims=True), acc + values(start, size, p.astype(BF16))

        carry = lax.fori_loop(0, qi // wide, lambda j, c: tile(j * (wide * tq), wide * tq, c), zeros)
        start = (qi // wide) * (wide * tq)
        piece = wide // 2
        while piece >= 1:
            take = ((qi % wide) & piece) != 0
            carry = lax.cond(take, lambda c, s=start, n=piece * tq: tile(s, n, c), lambda c: c, carry)
            start = start + jnp.where(take, piece * tq, 0)
            piece //= 2
        return tile(qi * tq, tq, carry, vis_ref[...])

    def online_path():
        def update(j, carry, masked):
            m, l, acc = carry
            s = scores(j * tq, tq)
            if masked:
                s = jnp.where(vis_ref[...] > 0.0, s, -jnp.inf)
            m_new = jnp.maximum(m, jnp.max(s, axis=0, keepdims=True))
            alpha = jnp.exp2(m - m_new)
            p = jnp.exp2(s - m_new)
            l = alpha * l + jnp.sum(p, axis=0, keepdims=True)
            return m_new, l, alpha * acc + values(j * tq, tq, p.astype(BF16))

        carry = (jnp.full((1, 2 * tq), -jnp.inf, F32),) + zeros
        carry = lax.fori_loop(0, qi, lambda j, c: update(j, c, False), carry)
        _, l, acc = update(qi, carry, True)
        return l, acc

    l, acc = lax.cond(bounded_ref[0] != 0, bounded_path, online_path)

    lam = (jnp.exp(jnp.sum(lq1_ref[...] * lk1_ref[...], axis=-1, keepdims=True))
           - jnp.exp(jnp.sum(lq2_ref[...] * lk2_ref[...], axis=-1, keepdims=True)) + lam_init)
    o = acc * (1.0 / l)
    o = o[:, :tq] - lam * o[:, tq:]
    ms = jnp.mean(o * o, axis=0, keepdims=True)
    o = o * lax.rsqrt(ms + SUBLN_EPS) * (sg_ref[...] * (1.0 - lam_init))
    o_ref[0] = o.T.astype(BF16)


def _attn(bounded, qt, k, vt, lq1, lk1, lq2, lk2, sg, lam_init):
    B, S, attn_w = k.shape
    nh = attn_w // LANES
    tq = min(TQ, S)
    wide = max(min(KV_WIDE, S // tq), 1)
    assert tq % CHUNK == 0 and wide & (wide - 1) == 0
    kern = functools.partial(_attn_kernel, tq=tq, wide=wide, lam_init=lam_init)
    vec = lambda n: pl.BlockSpec((1, n), lambda b, h, i: (0, 0))
    key_chunk = jnp.arange(tq, dtype=jnp.int32)[:, None] // CHUNK
    qry_chunk = (jnp.arange(2 * tq, dtype=jnp.int32)[None, :] % tq) // CHUNK
    vis = (key_chunk <= qry_chunk).astype(F32)
    return pl.pallas_call(
        kern,
        grid=(B, nh, S // tq),
        in_specs=[pl.BlockSpec(memory_space=pltpu.SMEM),
                  pl.BlockSpec((1, 1, LANES, tq), lambda b, h, i: (b, h, 0, i)),
                  pl.BlockSpec((1, S, LANES), lambda b, h, i: (b, 0, h)),
                  pl.BlockSpec((1, 1, LANES, S), lambda b, h, i: (b, h, 0, 0)),
                  pl.BlockSpec((tq, 2 * tq), lambda b, h, i: (0, 0)),
                  vec(HEAD_DIM), vec(HEAD_DIM), vec(HEAD_DIM), vec(HEAD_DIM),
                  pl.BlockSpec((LANES, 1), lambda b, h, i: (0, 0))],
        out_specs=pl.BlockSpec((1, tq, LANES), lambda b, h, i: (b, i, h)),
        out_shape=jax.ShapeDtypeStruct((B, S, attn_w), BF16),
        compiler_params=pltpu.CompilerParams(
            dimension_semantics=("arbitrary", "arbitrary", "arbitrary"), vmem_limit_bytes=VMEM_LIMIT),
        name="attn",
    )(bounded, qt, k, vt, vis, lq1, lk1, lq2, lk2, sg)


def _lru_kernel(xr_ref, gr_ref, cw_ref, cb_ref, wa_ref, ba_ref, wx_ref, bx_ref, lam_ref, o_ref,
                tail_ref, h_ref, *, tt):
    @pl.when(pl.program_id(1) == 0)
    def _():
        tail_ref[...] = jnp.zeros_like(tail_ref)
        h_ref[...] = jnp.zeros_like(h_ref)

    x = xr_ref[0]
    ext = jnp.concatenate([tail_ref[...], x], axis=0)
    xc = cb_ref[...] + cw_ref[CONV_W - 1:CONV_W, :] * x
    for j in range(CONV_W - 1):
        back = CONV_W - 1 - j
        xc = xc + cw_ref[j:j + 1, :] * ext[SUBLANES - back:SUBLANES - back + tt]
    tail_ref[...] = x[tt - SUBLANES:]

    xcb = xc.astype(BF16)
    r = jax.nn.sigmoid(_dot(xcb, wa_ref[...]) + ba_ref[...])
    i = jax.nn.sigmoid(_dot(xcb, wx_ref[...]) + bx_ref[...])
    nl = -lam_ref[...]
    softplus = jnp.maximum(nl, 0.0) + jnp.log(1.0 + jnp.exp(-jnp.abs(nl)))
    log_a = (-LRU_C) * r * softplus
    a = jnp.exp(log_a)
    th = jnp.tanh(log_a)
    u = jnp.sqrt((-2.0) * th / (1.0 - th)) * (i * xc)

    n_groups = tt // SUBLANES
    a = a.reshape(n_groups, SUBLANES, a.shape[1])
    u = u.reshape(a.shape)
    in_group = lax.broadcasted_iota(jnp.int32, a.shape, 1)
    d = 1
    while d < SUBLANES:
        keep = in_group >= d
        u = u + a * jnp.where(keep, pltpu.roll(u, d, 1), 0.0)
        a = a * jnp.where(keep, pltpu.roll(a, d, 1), 1.0)
        d *= 2
    carry = h_ref[0:1, :]
    groups = []
    for g in range(n_groups):
        hg = a[g] * carry + u[g]
        groups.append(hg)
        carry = hg[SUBLANES - 1:SUBLANES, :]
    h = jnp.concatenate(groups, axis=0)
    h_ref[...] = jnp.broadcast_to(carry, h_ref.shape)

    g = gr_ref[0]
    gelu = 0.5 * g * (1.0 + jnp.tanh(math.sqrt(2.0 / math.pi) * (g + 0.044715 * (g * g * g))))
    o_ref[0] = (h * gelu).astype(BF16)


def _lru(xr, gr, conv_w, conv_b, wa_bd, b_a, wx_bd, b_x, lam):
    B, S, W = xr.shape
    tt = min(TT, S)
    kern = functools.partial(_lru_kernel, tt=tt)
    vec = pl.BlockSpec((1, W), lambda b, i: (0, 0))
    mat = pl.BlockSpec((W, W), lambda b, i: (0, 0))
    return pl.pallas_call(
        kern,
        grid=(B, S // tt),
        in_specs=[pl.BlockSpec((1, tt, W), lambda b, i: (b, i, 0)),
                  pl.BlockSpec((1, tt, W), lambda b, i: (b, i, 0)),
                  pl.BlockSpec((CONV_W, W), lambda b, i: (0, 0)),
                  vec, mat, vec, mat, vec, vec],
        out_specs=pl.BlockSpec((1, tt, W), lambda b, i: (b, i, 0)),
        out_shape=jax.ShapeDtypeStruct((B, S, W), BF16),
        scratch_shapes=[pltpu.VMEM((SUBLANES, W), F32), pltpu.VMEM((SUBLANES, W), F32)],
        compiler_params=pltpu.CompilerParams(dimension_semantics=("arbitrary", "arbitrary"),
                                             vmem_limit_bytes=VMEM_LIMIT),
        name="lru",
    )(xr, gr, conv_w, conv_b, wa_bd, b_a, wx_bd, b_x, lam)


def _outproj_kernel(att_ref, rec_ref, x_ref, wo_ref, g1_ref, n2_ref, sc_ref, sh_ref, wr_ref, br_ref,
                    x1_ref, hm_ref, idx_ref, gate_ref, rank_ref, count_ref, cnt_ref, *, attn_w, n_exp):
    tm = x_ref.shape[1]
    mix = _dot(att_ref[0], wo_ref[0:attn_w, :]) + _dot(rec_ref[0], wo_ref[attn_w:, :])
    x1 = x_ref[0] + g1_ref[0] * mix
    x1_ref[0] = x1
    ms = jnp.mean(x1 * x1, axis=-1, keepdims=True)
    hm = x1 * lax.rsqrt(ms + NORM_EPS) * n2_ref[...]
    hm = hm * (1.0 + sc_ref[0]) + sh_ref[0]
    for c in range(hm.shape[1] // LANES):
        hm_ref[pl.ds(c, tm, stride=SUBLANES), :] = hm[:, c * LANES:(c + 1) * LANES]

    hh, hl = _split_bf16(hm)
    wh, wl = _split_bf16(wr_ref[...])
    nt = (((1,), (1,)), ((), ()))
    logits = (lax.dot_general(wh, hh, nt, preferred_element_type=F32)
              + (lax.dot_general(wh, hl, nt, preferred_element_type=F32)
                 + lax.dot_general(wl, hh, nt, preferred_element_type=F32))) + br_ref[...]
    eidx = lax.broadcasted_iota(jnp.int32, logits.shape, 0)
    vals, idxs = [], []
    for _ in range(TOP_K):
        m = jnp.max(logits, axis=0, keepdims=True)
        am = jnp.min(jnp.where(logits == m, eidx, n_exp), axis=0, keepdims=True)
        vals.append(m)
        idxs.append(am)
        logits = jnp.where(eidx == am, -jnp.inf, logits)
    ex = [jnp.exp(v - vals[0]) for v in vals]
    den = ex[0] + ex[1] + ex[2] + ex[3]
    idx_ref[0] = jnp.concatenate(idxs, axis=0)
    gate_ref[0] = jnp.concatenate([e / den for e in ex], axis=0)

    @pl.when((pl.program_id(0) == 0) & (pl.program_id(1) == 0))
    def _():
        cnt_ref[...] = jnp.zeros_like(cnt_ref)

    hits = [eidx == am for am in idxs]
    sel = (hits[0] | hits[1] | hits[2] | hits[3]).astype(F32)
    earlier = (lax.broadcasted_iota(jnp.int32, (tm, tm), 0)
               < lax.broadcasted_iota(jnp.int32, (tm, tm), 1)).astype(BF16)
    before = _dot(sel.astype(BF16), earlier) + cnt_ref[...]
    rank_ref[0] = jnp.concatenate(
        [jnp.sum(jnp.where(h, before, 0.0), axis=0, keepdims=True) for h in hits], axis=0).astype(jnp.int32)
    cnt_ref[...] += jnp.sum(sel, axis=1, keepdims=True)
    count_ref[...] = cnt_ref[...].astype(jnp.int32)


def _outproj(att, rec, x, wo_b, g1, n2g, sc2, sh2, wr_t, br, n_exp, b0, B):
    _, S, D = x.shape
    attn_w = att.shape[2]
    lru_w = rec.shape[2]
    tm = min(TM_PROJ, S)
    ns = S // tm
    rows = D // LANES
    kern = functools.partial(_outproj_kernel, attn_w=attn_w, n_exp=n_exp)
    bvec = pl.BlockSpec((1, 1, D), lambda b, i: (b + b0, 0, 0))
    return pl.pallas_call(
        kern,
        grid=(B, ns),
        in_specs=[pl.BlockSpec((1, tm, attn_w), lambda b, i: (b + b0, i, 0)),
                  pl.BlockSpec((1, tm, lru_w), lambda b, i: (b + b0, i, 0)),
                  pl.BlockSpec((1, tm, D), lambda b, i: (b + b0, i, 0)),
                  pl.BlockSpec((attn_w + lru_w, D), lambda b, i: (0, 0)),
                  bvec,
                  pl.BlockSpec((1, D), lambda b, i: (0, 0)),
                  bvec, bvec,
                  pl.BlockSpec((n_exp, D), lambda b, i: (0, 0)),
                  pl.BlockSpec((n_exp, 1), lambda b, i: (0, 0))],
        out_specs=[pl.BlockSpec((1, tm, D), lambda b, i: (b, i, 0)),
                   pl.BlockSpec((tm * rows, LANES), lambda b, i: (b * ns + i, 0)),
                   pl.BlockSpec((1, TOP_K, tm), lambda b, i: (b, 0, i)),
                   pl.BlockSpec((1, TOP_K, tm), lambda b, i: (b, 0, i)),
                   pl.BlockSpec((1, TOP_K, tm), lambda b, i: (b, 0, i)),
                   pl.BlockSpec((n_exp, 1), lambda b, i: (0, 0))],
        out_shape=[jax.ShapeDtypeStruct((B, S, D), F32),
                   jax.ShapeDtypeStruct((B * S * rows, LANES), F32),
                   jax.ShapeDtypeStruct((B, TOP_K, S), jnp.int32),
                   jax.ShapeDtypeStruct((B, TOP_K, S), F32),
                   jax.ShapeDtypeStruct((B, TOP_K, S), jnp.int32),
                   jax.ShapeDtypeStruct((n_exp, 1), jnp.int32)],
        scratch_shapes=[pltpu.VMEM((n_exp, 1), F32)],
        compiler_params=pltpu.CompilerParams(dimension_semantics=("arbitrary", "arbitrary"),
                                             vmem_limit_bytes=VMEM_LIMIT),
        name="outproj",
    )(att, rec, x, wo_b, g1, n2g, sc2, sh2, wr_t, br)


def _sc_gather_rows(table, idx):
    n = idx.shape[0]
    workers = SC_CORES * SC_SUBCORES
    per_worker = n // workers
    assert n % (workers * SC_CHUNK) == 0, "index count must split evenly into per-subcore chunks"
    n_chunks = per_worker // SC_CHUNK
    mesh = plsc.VectorSubcoreMesh(core_axis_name="c", subcore_axis_name="s",
                                  num_cores=SC_CORES, num_subcores=SC_SUBCORES)

    @functools.partial(
        pl.kernel, mesh=mesh,
        out_type=jax.ShapeDtypeStruct((n,) + table.shape[1:], table.dtype),
        scratch_types=[pltpu.VMEM((SC_CHUNK,), jnp.int32),
                       pltpu.VMEM((SC_CHUNK,) + table.shape[1:], table.dtype),
                       pltpu.SemaphoreType.DMA],
        name="sc_gather_rows",
    )
    def gather(table_hbm, idx_hbm, out_hbm, idx_v, rows_v, sem):
        base = (lax.axis_index("s") * SC_CORES + lax.axis_index("c")) * per_worker

        @pl.loop(0, n_chunks)
        def _(ci):
            off = pl.multiple_of(base + ci * SC_CHUNK, SC_CHUNK)
            pltpu.sync_copy(idx_hbm.at[pl.ds(off, SC_CHUNK)], idx_v)
            pltpu.async_copy(table_hbm.at[idx_v], rows_v, sem).wait()
            pltpu.sync_copy(rows_v, out_hbm.at[pl.ds(off, SC_CHUNK)])

    return gather(table, idx)


def _sc_scatter_rows(src, pos, n_out):
    n_tok = src.shape[0]
    n_k = pos.shape[1]
    workers = SC_CORES * SC_SUBCORES
    per_worker = n_tok // workers
    assert n_tok % (workers * SC_CHUNK) == 0, "token count must split evenly into per-subcore chunks"
    n_chunks = per_worker // SC_CHUNK
    mesh = plsc.VectorSubcoreMesh(core_axis_name="c", subcore_axis_name="s",
                                  num_cores=SC_CORES, num_subcores=SC_SUBCORES)

    @functools.partial(
        pl.kernel, mesh=mesh,
        out_type=jax.ShapeDtypeStruct((n_out,) + src.shape[1:], src.dtype),
        scratch_types=[pltpu.VMEM((n_k, SC_CHUNK), jnp.int32),
                       pltpu.VMEM((SC_CHUNK,) + src.shape[1:], src.dtype),
                       pltpu.SemaphoreType.DMA],
        name="sc_scatter_rows",
    )
    def scatter(src_hbm, pos_hbm, out_hbm, idx_v, rows_v, sem):
        base = (lax.axis_index("s") * SC_CORES + lax.axis_index("c")) * n_chunks

        @pl.loop(0, n_chunks)
        def _(ci):
            chunk = base + ci
            pltpu.sync_copy(pos_hbm.at[chunk], idx_v)
            pltpu.sync_copy(src_hbm.at[pl.ds(pl.multiple_of(chunk * SC_CHUNK, SC_CHUNK), SC_CHUNK)], rows_v)
            copies = [pltpu.async_copy(rows_v, out_hbm.at[idx_v.at[k]], sem) for k in range(n_k)]
            for cp in copies:
                cp.wait()

    return scatter(src, pos)


def _moe_kernel(be_ref, nv_ref, nused_ref, x_ref, w1_ref, b1_ref, w2_ref, b2_ref, y_ref, w1b_ref, w2b_ref,
                *, d_ff):
    i = pl.program_id(0)
    rows = SUBLANES

    @pl.when(i < nused_ref[0])
    def _():
        @pl.when((i == 0) | (be_ref[i] != be_ref[jnp.maximum(i - 1, 0)]))
        def _():
            w1b_ref[...] = w1_ref[0].astype(BF16)
            w2b_ref[...] = w2_ref[0].astype(BF16)

        x = jnp.concatenate([x_ref[pl.ds(c, EB, stride=rows), :] for c in range(rows)], axis=1)
        valid = lax.broadcasted_iota(jnp.int32, (EB, 1), 0) < nv_ref[i]
        x = jnp.where(valid, x, 0.0).astype(BF16)
        hu = _dot(x, w1b_ref[...]) + b1_ref[0]
        glu = jnp.minimum(hu[:, :d_ff], SWIGLU_LIMIT)
        lin = jnp.clip(hu[:, d_ff:], -SWIGLU_LIMIT, SWIGLU_LIMIT)
        act = glu * jax.nn.sigmoid(SWIGLU_ALPHA * glu) * (lin + 1.0)
        y = _dot(act.astype(BF16), w2b_ref[...]) + b2_ref[0]
        for c in range(rows):
            y_ref[pl.ds(c, EB, stride=rows), :] = y[:, c * LANES:(c + 1) * LANES]

    @pl.when(i >= nused_ref[0])
    def _():
        y_ref[...] = jnp.zeros_like(y_ref)


def _moe(block_e, nv, n_used, xs2d, w1, b1, w2, b2):
    n_blocks = block_e.shape[0]
    n_exp, D, ff2 = w1.shape
    d_ff = ff2 // 2
    rows = D // LANES
    assert rows == SUBLANES, "a token row must fill exactly one (8,128) f32 tile"
    kern = functools.partial(_moe_kernel, d_ff=d_ff)
    grid_spec = pltpu.PrefetchScalarGridSpec(
        num_scalar_prefetch=3,
        grid=(n_blocks,),
        in_specs=[pl.BlockSpec((EB * rows, LANES), lambda i, be, nv, nu: (i, 0)),
                  pl.BlockSpec((1, D, ff2), lambda i, be, nv, nu: (be[i], 0, 0)),
                  pl.BlockSpec((1, 1, ff2), lambda i, be, nv, nu: (be[i], 0, 0)),
                  pl.BlockSpec((1, d_ff, D), lambda i, be, nv, nu: (be[i], 0, 0)),
                  pl.BlockSpec((1, 1, D), lambda i, be, nv, nu: (be[i], 0, 0))],
        out_specs=pl.BlockSpec((EB * rows, LANES), lambda i, be, nv, nu: (i, 0)),
        scratch_shapes=[pltpu.VMEM((D, ff2), BF16), pltpu.VMEM((d_ff, D), BF16)],
    )
    return pl.pallas_call(
        kern,
        grid_spec=grid_spec,
        out_shape=jax.ShapeDtypeStruct((n_blocks * EB * rows, LANES), F32),
        compiler_params=pltpu.CompilerParams(dimension_semantics=("arbitrary",),
                                             vmem_limit_bytes=VMEM_LIMIT),
        name="moe",
    )(block_e, nv, n_used, xs2d, w1, b1, w2, b2)


def _combine_kernel(y_ref, gate_ref, x1_ref, g2_ref, *rest):
    o_ref = rest[-1]
    tm = x1_ref.shape[1]
    diag = (lax.broadcasted_iota(jnp.int32, (tm, tm), 0) == lax.broadcasted_iota(jnp.int32, (tm, tm), 1))
    moe = jnp.zeros(x1_ref.shape[1:], F32)
    for k in range(TOP_K):
        yk = jnp.concatenate([y_ref[k, pl.ds(c, tm, stride=SUBLANES), :] for c in range(SUBLANES)], axis=1)
        gate_col = jnp.sum(jnp.where(diag, gate_ref[0, k:k + 1, :], 0.0), axis=1, keepdims=True)
        moe = moe + gate_col * yk
    o_ref[0] = x1_ref[0] + g2_ref[0] * moe


def _combine(y4, gates, x1, g2, g0, B, b0, n_batch, out_prev):
    _, S, D = x1.shape
    rows = D // LANES
    tm = min(TM_COMB, S)
    ns = S // tm
    in_specs = [pl.BlockSpec((TOP_K, tm * rows, LANES), lambda b, i: (0, b * ns + i, 0)),
                pl.BlockSpec((1, TOP_K, tm), lambda b, i: (b + g0, 0, i)),
                pl.BlockSpec((1, tm, D), lambda b, i: (b + g0, i, 0)),
                pl.BlockSpec((1, 1, D), lambda b, i: (b + g0 + b0, 0, 0))]
    args = [y4.reshape(TOP_K, B * S * rows, LANES), gates, x1, g2]
    aliases = {}
    if out_prev is not None:
        in_specs.append(pl.BlockSpec(memory_space=pl.ANY))
        args.append(out_prev)
        aliases = {len(args) - 1: 0}
    return pl.pallas_call(
        _combine_kernel,
        grid=(B, ns),
        in_specs=in_specs,
        out_specs=pl.BlockSpec((1, tm, D), lambda b, i: (b + g0 + b0, i, 0)),
        out_shape=jax.ShapeDtypeStruct((n_batch, S, D), F32),
        input_output_aliases=aliases,
        compiler_params=pltpu.CompilerParams(dimension_semantics=("arbitrary", "arbitrary"),
                                             vmem_limit_bytes=VMEM_LIMIT),
        name="combine",
    )(*args)


def _route(top_idx, rank, counts):
    B, K, S = top_idx.shape
    T = B * S
    n_exp = counts.shape[0]
    nblk = (counts + EB - 1) // EB
    bend = jnp.cumsum(nblk)
    bstart = bend - nblk
    n_blocks = -(-(T * K) // EB) + n_exp
    blk = jnp.arange(n_blocks, dtype=jnp.int32)
    block_e = jnp.minimum(jnp.sum(blk[:, None] >= bend[None, :], axis=1), n_exp - 1).astype(jnp.int32)
    nv = jnp.where(blk < bend[-1], jnp.clip(counts[block_e] - (blk - bstart[block_e]) * EB, 0, EB), 0)
    start_row = jnp.sum(jnp.where(top_idx[..., None] == jnp.arange(n_exp, dtype=jnp.int32),
                                  (bstart * EB).astype(jnp.int32), 0), axis=-1)
    pos = (start_row + rank).transpose(1, 0, 2).reshape(K, T).astype(jnp.int32)
    return block_e, nv.astype(jnp.int32), bend[-1:].astype(jnp.int32), pos


def _lambda_init(layer_idx):
    return 0.8 - 0.6 * math.exp(-0.3 * layer_idx)


def _block_diag(w):
    n, c, d = w.shape
    eye = jnp.eye(n, dtype=w.dtype)
    return (eye[:, None, :, None] * w[:, :, None, :]).reshape(n * c, n * d)


def kernel(x, c, w_ada, b_ada, norm1_g, w_in, q_norm_g, k_norm_g, lambda_q1, lambda_k1, lambda_q2, lambda_k2, subln_g, conv_w, conv_b, w_rg_a, b_rg_a, w_rg_x, b_rg_x, lru_lambda, w_out, norm2_g, w_router, b_router, w_moe1, b_moe1, w_moe2, b_moe2):
    B, S, D = x.shape
    depth = w_ada.shape[0]
    lru_w = conv_w.shape[2]
    attn_w = (w_in.shape[2] - 2 * lru_w) // 3
    n_exp = w_router.shape[2]
    n_heads = attn_w // HEAD_DIM
    assert attn_w % LANES == 0 and S % CHUNK == 0

    half = HEAD_DIM // 2
    inv = ROPE_THETA ** (-jnp.arange(half, dtype=F32) / half)
    ang = jnp.arange(S, dtype=jnp.int32).astype(F32)[:, None] * inv[None, :]
    cos_t = jnp.tile(jnp.concatenate([jnp.cos(ang), jnp.cos(ang)], axis=1), (1, n_heads))
    sin_t = jnp.tile(jnp.concatenate([-jnp.sin(ang), jnp.sin(ang)], axis=1), (1, n_heads))
    group = jnp.arange(attn_w, dtype=jnp.int32) // HEAD_DIM
    gsum = (group[:, None] == group[None, :]).astype(BF16)

    for l in range(depth):
        mod = _ada(c, w_ada[l], b_ada[l])
        sh1, sc1, g1, sh2, sc2, g2 = [m.reshape(B, 1, D) for m in jnp.split(mod, 6, axis=-1)]

        qt, k, vt, xr, gr = _inproj(
            x, norm1_g[l].reshape(1, D), sc1, sh1, w_in[l].astype(BF16), gsum,
            jnp.tile(q_norm_g[l], n_heads).reshape(1, attn_w), jnp.tile(k_norm_g[l], n_heads).reshape(1, attn_w),
            cos_t, sin_t, attn_w, lru_w)

        score_bound = (HEAD_DIM ** 0.5 * LOG2_E * BF16_SLACK) * jnp.max(jnp.abs(q_norm_g[l])) * jnp.max(jnp.abs(k_norm_g[l]))
        bounded = (score_bound <= SCORE_BOUND).astype(jnp.int32).reshape(1)
        att = _attn(bounded, qt, k, vt, lambda_q1[l].reshape(1, -1), lambda_k1[l].reshape(1, -1),
                    lambda_q2[l].reshape(1, -1), lambda_k2[l].reshape(1, -1),
                    subln_g[l].reshape(-1, 1), _lambda_init(l))

        rec = _lru(xr, gr, conv_w[l], conv_b[l].reshape(1, lru_w),
                   _block_diag(w_rg_a[l]).astype(BF16), b_rg_a[l].reshape(1, lru_w),
                   _block_diag(w_rg_x[l]).astype(BF16), b_rg_x[l].reshape(1, lru_w),
                   lru_lambda[l].reshape(1, lru_w))

        n_groups = MOE_GROUPS if B % MOE_GROUPS == 0 else 1
        nb = B // n_groups
        T = nb * S
        rows = D // LANES
        wo_b = w_out[l].astype(BF16)
        x_in, x_out = x, None
        for grp in range(n_groups):
            b0 = grp * nb
            x1, hm2d, top_idx, gates, rank, counts = _outproj(
                att, rec, x_in, wo_b, g1, norm2_g[l].reshape(1, D), sc2, sh2,
                w_router[l].T, b_router[l].reshape(n_exp, 1), n_exp, b0, nb)
            block_e, nv, n_used, pos = _route(top_idx, rank, counts.reshape(n_exp))
            pos_chunks = pos.reshape(TOP_K, T // SC_CHUNK, SC_CHUNK).transpose(1, 0, 2)
            xs = _sc_scatter_rows(hm2d.reshape(T, rows, LANES), pos_chunks, block_e.shape[0] * EB)
            ys = _moe(block_e, nv, n_used, xs.reshape(-1, LANES),
                      w_moe1[l], b_moe1[l].reshape(n_exp, 1, -1), w_moe2[l], b_moe2[l].reshape(n_exp, 1, -1))
            n_part = COMBINE_PARTS if nb % COMBINE_PARTS == 0 else 1
            npb = nb // n_part
            pos_b = pos.reshape(TOP_K, nb, S)
            for part in range(n_part):
                g0 = part * npb
                y4 = _sc_gather_rows(ys.reshape(-1, rows, LANES), pos_b[:, g0:g0 + npb].reshape(TOP_K * npb * S))
                x_out = _combine(y4, gates, x1, g2, g0, npb, b0, B, x_out)
        x = x_out
    return x
```

```python
import functools
import math

import jax
import jax.numpy as jnp
from jax import lax
from jax.experimental import pallas as pl
from jax.experimental.pallas import tpu as pltpu
from jax.experimental.pallas import tpu_sc as plsc

F32 = jnp.float32
BF16 = jnp.bfloat16

CHUNK = 64
HEAD_DIM = 64
ROPE_THETA = 10000.0
LRU_BLOCK = 64
CONV_W = 4
LRU_C = 8.0
TOP_K = 4
SWIGLU_ALPHA = 1.702
SWIGLU_LIMIT = 7.0
NORM_EPS = 1e-6
SUBLN_EPS = 1e-5
LOG2_E = math.log2(math.e)
SCORE_BOUND = 60.0
BF16_SLACK = 1.02

LANES = 128
SUBLANES = 8
YROWS = 4
VMEM_LIMIT = 56 * 1024 * 1024
SC_CORES = 2
SC_SUBCORES = 16
SC_CHUNK = 64

TM_PROJ = 512
TQ = 512
KV_WIDE = 4
TT = 256
EB = 512
TM_COMB = 512
MOE_GROUPS = 2


def _split_bf16(a):
    hi = a.astype(BF16)
    lo = (a - hi.astype(F32)).astype(BF16)
    return hi, lo


def _dot(a, b):
    return jnp.dot(a, b, preferred_element_type=F32)


def _dot3(a, b):
    ah, al = _split_bf16(a)
    bh, bl = _split_bf16(b)
    return _dot(ah, bh) + (_dot(ah, bl) + _dot(al, bh))


def _ada_kernel(c_ref, w_ref, b_ref, o_ref):
    c = c_ref[...]
    s = c * jax.nn.sigmoid(c)
    o_ref[...] = _dot3(s, w_ref[...]) + b_ref[...]


def _ada(c, w_ada, b_ada):
    B, D = c.shape
    N = w_ada.shape[1]
    tn = D
    return pl.pallas_call(
        _ada_kernel,
        grid=(N // tn,),
        in_specs=[pl.BlockSpec((B, D), lambda j: (0, 0)),
                  pl.BlockSpec((D, tn), lambda j: (0, j)),
                  pl.BlockSpec((1, tn), lambda j: (0, j))],
        out_specs=pl.BlockSpec((B, tn), lambda j: (0, j)),
        out_shape=jax.ShapeDtypeStruct((B, N), F32),
        compiler_params=pltpu.CompilerParams(dimension_semantics=("arbitrary",),
                                             vmem_limit_bytes=VMEM_LIMIT),
        name="ada",
    )(c, w_ada, b_ada.reshape(1, N))


def _inproj_kernel(x_ref, g_ref, sc_ref, sh_ref, w_ref, gsum_ref, qg_ref, kg_ref, cos_ref, sin_ref,
                   qt_ref, k_ref, vt_ref, xr_ref, gr_ref, *, attn_w, lru_w):
    x = x_ref[0]
    ms = jnp.mean(x * x, axis=-1, keepdims=True)
    hn = x * lax.rsqrt(ms + NORM_EPS) * g_ref[...]
    hn = (hn * (1.0 + sc_ref[0]) + sh_ref[0]).astype(BF16)

    cos = cos_ref[...]
    sin = sin_ref[...]
    gsum = gsum_ref[...]
    lane = lax.broadcasted_iota(jnp.int32, cos.shape, 1)
    first_half = (lane % HEAD_DIM) < (HEAD_DIM // 2)

    def head_norm_rope(t, gain):
        ssum = _dot((t * t).astype(BF16), gsum)
        t = t * lax.rsqrt(ssum * (1.0 / HEAD_DIM) + NORM_EPS) * gain
        half = HEAD_DIM // 2
        partner = jnp.where(first_half, pltpu.roll(t, attn_w - half, 1), pltpu.roll(t, half, 1))
        return t * cos + partner * sin

    q = head_norm_rope(_dot(hn, w_ref[:, 0:attn_w]), qg_ref[...]) * (HEAD_DIM ** -0.5 * LOG2_E)
    for h in range(attn_w // LANES):
        qt_ref[0, h] = q[:, h * LANES:(h + 1) * LANES].T.astype(BF16)

    k = head_norm_rope(_dot(hn, w_ref[:, attn_w:2 * attn_w]), kg_ref[...])
    k_ref[0] = k.astype(BF16)

    v = _dot(hn, w_ref[:, 2 * attn_w:3 * attn_w])
    for h in range(attn_w // LANES):
        vt_ref[0, h] = v[:, h * LANES:(h + 1) * LANES].T.astype(BF16)
    xr_ref[0] = _dot(hn, w_ref[:, 3 * attn_w:3 * attn_w + lru_w])
    gr_ref[0] = _dot(hn, w_ref[:, 3 * attn_w + lru_w:3 * attn_w + 2 * lru_w])


def _inproj(x, g1n, sc1, sh1, w_in_b, gsum, qg, kg, cos_t, sin_t, attn_w, lru_w):
    B, S, D = x.shape
    tm = min(TM_PROJ, S)
    in_w = w_in_b.shape[1]
    nh = attn_w // LANES
    kern = functools.partial(_inproj_kernel, attn_w=attn_w, lru_w=lru_w)
    return pl.pallas_call(
        kern,
        grid=(S // tm, B),
        in_specs=[pl.BlockSpec((1, tm, D), lambda i, b: (b, i, 0)),
                  pl.BlockSpec((1, D), lambda i, b: (0, 0)),
                  pl.BlockSpec((1, 1, D), lambda i, b: (b, 0, 0)),
                  pl.BlockSpec((1, 1, D), lambda i, b: (b, 0, 0)),
                  pl.BlockSpec((D, in_w), lambda i, b: (0, 0)),
                  pl.BlockSpec((attn_w, attn_w), lambda i, b: (0, 0)),
                  pl.BlockSpec((1, attn_w), lambda i, b: (0, 0)),
                  pl.BlockSpec((1, attn_w), lambda i, b: (0, 0)),
                  pl.BlockSpec((tm, attn_w), lambda i, b: (i, 0)),
                  pl.BlockSpec((tm, attn_w), lambda i, b: (i, 0))],
        out_specs=[pl.BlockSpec((1, nh, LANES, tm), lambda i, b: (b, 0, 0, i)),
                   pl.BlockSpec((1, tm, attn_w), lambda i, b: (b, i, 0)),
                   pl.BlockSpec((1, nh, LANES, tm), lambda i, b: (b, 0, 0, i)),
                   pl.BlockSpec((1, tm, lru_w), lambda i, b: (b, i, 0)),
                   pl.BlockSpec((1, tm, lru_w), lambda i, b: (b, i, 0))],
        out_shape=[jax.ShapeDtypeStruct((B, nh, LANES, S), BF16),
                   jax.ShapeDtypeStruct((B, S, attn_w), BF16),
                   jax.ShapeDtypeStruct((B, nh, LANES, S), BF16),
                   jax.ShapeDtypeStruct((B, S, lru_w), F32),
                   jax.ShapeDtypeStruct((B, S, lru_w), F32)],
        compiler_params=pltpu.CompilerParams(dimension_semantics=("arbitrary", "arbitrary"),
                                             vmem_limit_bytes=VMEM_LIMIT),
        name="inproj",
    )(x, g1n, sc1, sh1, w_in_b, gsum, qg, kg, cos_t, sin_t)


def _attn_kernel(bounded_ref, qt_ref, k_ref, vt_ref, vis_ref, lq1_ref, lk1_ref, lq2_ref, lk2_ref, sg_ref, o_ref,
                 *, tq, wide, lam_init):
    qi = pl.program_id(2)
    qt = qt_ref[0, 0]
    row = lax.broadcasted_iota(jnp.int32, qt.shape, 0)
    zero = jnp.zeros_like(qt)
    qs = jnp.concatenate([jnp.where(row < HEAD_DIM, qt, zero),
                          jnp.where(row >= HEAD_DIM, qt, zero)], axis=1)

    def scores(start, size):
        kk = k_ref[0, pl.ds(pl.multiple_of(start, tq), size), :]
        return _dot(kk, qs)

    def values(start, size, p):
        vt = vt_ref[0, 0, :, pl.ds(pl.multiple_of(start, tq), size)]
        return _dot(vt, p)

    zeros = (jnp.zeros((1, 2 * tq), F32), jnp.zeros((LANES, 2 * tq), F32))

    def bounded_path():
        def tile(start, size, carry, vis=None):
            l, acc = carry
            p = jnp.exp2(scores(start, size))
            if vis is not None:
                p = p * vis
            return l + jnp.sum(p, axis=0, keepdims=True), acc + values(start, size, p.astype(BF16))

        carry = lax.fori_loop(0, qi // wide, lambda j, c: tile(j * (wide * tq), wide * tq, c), zeros)
        start = (qi // wide) * (wide * tq)
        piece = wide // 2
        while piece >= 1:
            take = ((qi % wide) & piece) != 0
            carry = lax.cond(take, lambda c, s=start, n=piece * tq: tile(s, n, c), lambda c: c, carry)
            start = start + jnp.where(take, piece * tq, 0)
            piece //= 2
        return tile(qi * tq, tq, carry, vis_ref[...])

    def online_path():
        def update(j, carry, masked):
            m, l, acc = carry
            s = scores(j * tq, tq)
            if masked:
                s = jnp.where(vis_ref[...] > 0.0, s, -jnp.inf)
            m_new = jnp.maximum(m, jnp.max(s, axis=0, keepdims=True))
            alpha = jnp.exp2(m - m_new)
            p = jnp.exp2(s - m_new)
            l = alpha * l + jnp.sum(p, axis=0, keepdims=True)
            return m_new, l, alpha * acc + values(j * tq, tq, p.astype(BF16))

        carry = (jnp.full((1, 2 * tq), -jnp.inf, F32),) + zeros
        carry = lax.fori_loop(0, qi, lambda j, c: update(j, c, False), carry)
        _, l, acc = update(qi, carry, True)
        return l, acc

    l, acc = lax.cond(bounded_ref[0] != 0, bounded_path, online_path)

    lam = (jnp.exp(jnp.sum(lq1_ref[...] * lk1_ref[...], axis=-1, keepdims=True))
           - jnp.exp(jnp.sum(lq2_ref[...] * lk2_ref[...], axis=-1, keepdims=True)) + lam_init)
    o = acc * (1.0 / l)
    o = o[:, :tq] - lam * o[:, tq:]
    ms = jnp.mean(o * o, axis=0, keepdims=True)
    o = o * lax.rsqrt(ms + SUBLN_EPS) * (sg_ref[...] * (1.0 - lam_init))
    o_ref[0] = o.T.astype(BF16)


def _attn(bounded, qt, k, vt, lq1, lk1, lq2, lk2, sg, lam_init):
    B, S, attn_w = k.shape
    nh = attn_w // LANES
    tq = min(TQ, S)
    wide = max(min(KV_WIDE, S // tq), 1)
    assert tq % CHUNK == 0 and wide & (wide - 1) == 0
    kern = functools.partial(_attn_kernel, tq=tq, wide=wide, lam_init=lam_init)
    vec = lambda n: pl.BlockSpec((1, n), lambda b, h, i: (0, 0))
    key_chunk = jnp.arange(tq, dtype=jnp.int32)[:, None] // CHUNK
    qry_chunk = (jnp.arange(2 * tq, dtype=jnp.int32)[None, :] % tq) // CHUNK
    vis = (key_chunk <= qry_chunk).astype(F32)
    return pl.pallas_call(
        kern,
        grid=(B, nh, S // tq),
        in_specs=[pl.BlockSpec(memory_space=pltpu.SMEM),
                  pl.BlockSpec((1, 1, LANES, tq), lambda b, h, i: (b, h, 0, i)),
                  pl.BlockSpec((1, S, LANES), lambda b, h, i: (b, 0, h)),
                  pl.BlockSpec((1, 1, LANES, S), lambda b, h, i: (b, h, 0, 0)),
                  pl.BlockSpec((tq, 2 * tq), lambda b, h, i: (0, 0)),
                  vec(HEAD_DIM), vec(HEAD_DIM), vec(HEAD_DIM), vec(HEAD_DIM),
                  pl.BlockSpec((LANES, 1), lambda b, h, i: (0, 0))],
        out_specs=pl.BlockSpec((1, tq, LANES), lambda b, h, i: (b, i, h)),
        out_shape=jax.ShapeDtypeStruct((B, S, attn_w), BF16),
        compiler_params=pltpu.CompilerParams(
            dimension_semantics=("arbitrary", "arbitrary", "arbitrary"), vmem_limit_bytes=VMEM_LIMIT),
        name="attn",
    )(bounded, qt, k, vt, vis, lq1, lk1, lq2, lk2, sg)


def _lru_kernel(xr_ref, gr_ref, cw_ref, cb_ref, wa_ref, ba_ref, wx_ref, bx_ref, lam_ref, o_ref,
                tail_ref, h_ref, *, tt):
    @pl.when(pl.program_id(1) == 0)
    def _():
        tail_ref[...] = jnp.zeros_like(tail_ref)
        h_ref[...] = jnp.zeros_like(h_ref)

    x = xr_ref[0]
    ext = jnp.concatenate([tail_ref[...], x], axis=0)
    xc = cb_ref[...] + cw_ref[CONV_W - 1:CONV_W, :] * x
    for j in range(CONV_W - 1):
        back = CONV_W - 1 - j
        xc = xc + cw_ref[j:j + 1, :] * ext[SUBLANES - back:SUBLANES - back + tt]
    tail_ref[...] = x[tt - SUBLANES:]

    xcb = xc.astype(BF16)
    r = jax.nn.sigmoid(_dot(xcb, wa_ref[...]) + ba_ref[...])
    i = jax.nn.sigmoid(_dot(xcb, wx_ref[...]) + bx_ref[...])
    nl = -lam_ref[...]
    softplus = jnp.maximum(nl, 0.0) + jnp.log(1.0 + jnp.exp(-jnp.abs(nl)))
    log_a = (-LRU_C) * r * softplus
    a = jnp.exp(log_a)
    th = jnp.tanh(log_a)
    u = jnp.sqrt((-2.0) * th / (1.0 - th)) * (i * xc)

    n_groups = tt // SUBLANES
    a = a.reshape(n_groups, SUBLANES, a.shape[1])
    u = u.reshape(a.shape)
    in_group = lax.broadcasted_iota(jnp.int32, a.shape, 1)
    d = 1
    while d < SUBLANES:
        keep = in_group >= d
        u = u + a * jnp.where(keep, pltpu.roll(u, d, 1), 0.0)
        a = a * jnp.where(keep, pltpu.roll(a, d, 1), 1.0)
        d *= 2
    carry = h_ref[0:1, :]
    groups = []
    for g in range(n_groups):
        hg = a[g] * carry + u[g]
        groups.append(hg)
        carry = hg[SUBLANES - 1:SUBLANES, :]
    h = jnp.concatenate(groups, axis=0)
    h_ref[...] = jnp.broadcast_to(carry, h_ref.shape)

    g = gr_ref[0]
    gelu = 0.5 * g * (1.0 + jnp.tanh(math.sqrt(2.0 / math.pi) * (g + 0.044715 * (g * g * g))))
    o_ref[0] = (h * gelu).astype(BF16)


def _lru(xr, gr, conv_w, conv_b, wa_bd, b_a, wx_bd, b_x, lam):
    B, S, W = xr.shape
    tt = min(TT, S)
    kern = functools.partial(_lru_kernel, tt=tt)
    vec = pl.BlockSpec((1, W), lambda b, i: (0, 0))
    mat = pl.BlockSpec((W, W), lambda b, i: (0, 0))
    return pl.pallas_call(
        kern,
        grid=(B, S // tt),
        in_specs=[pl.BlockSpec((1, tt, W), lambda b, i: (b, i, 0)),
                  pl.BlockSpec((1, tt, W), lambda b, i: (b, i, 0)),
                  pl.BlockSpec((CONV_W, W), lambda b, i: (0, 0)),
                  vec, mat, vec, mat, vec, vec],
        out_specs=pl.BlockSpec((1, tt, W), lambda b, i: (b, i, 0)),
        out_shape=jax.ShapeDtypeStruct((B, S, W), BF16),
        scratch_shapes=[pltpu.VMEM((SUBLANES, W), F32), pltpu.VMEM((SUBLANES, W), F32)],
        compiler_params=pltpu.CompilerParams(dimension_semantics=("arbitrary", "arbitrary"),
                                             vmem_limit_bytes=VMEM_LIMIT),
        name="lru",
    )(xr, gr, conv_w, conv_b, wa_bd, b_a, wx_bd, b_x, lam)


def _outproj_kernel(att_ref, rec_ref, x_ref, wo_ref, g1_ref, n2_ref, sc_ref, sh_ref, wr_ref, br_ref,
                    x1_ref, hm_ref, idx_ref, gate_ref, rank_ref, count_ref, cnt_ref, *, attn_w, n_exp):
    tm = x_ref.shape[1]
    mix = _dot(att_ref[0], wo_ref[0:attn_w, :]) + _dot(rec_ref[0], wo_ref[attn_w:, :])
    x1 = x_ref[0] + g1_ref[0] * mix
    x1_ref[0] = x1
    ms = jnp.mean(x1 * x1, axis=-1, keepdims=True)
    hm = x1 * lax.rsqrt(ms + NORM_EPS) * n2_ref[...]
    hm = hm * (1.0 + sc_ref[0]) + sh_ref[0]
    for c in range(hm.shape[1] // LANES):
        hm_ref[pl.ds(c, tm, stride=SUBLANES), :] = hm[:, c * LANES:(c + 1) * LANES]

    hh, hl = _split_bf16(hm)
    wh, wl = _split_bf16(wr_ref[...])
    nt = (((1,), (1,)), ((), ()))
    logits = (lax.dot_general(wh, hh, nt, preferred_element_type=F32)
              + (lax.dot_general(wh, hl, nt, preferred_element_type=F32)
                 + lax.dot_general(wl, hh, nt, preferred_element_type=F32))) + br_ref[...]
    eidx = lax.broadcasted_iota(jnp.int32, logits.shape, 0)
    vals, idxs = [], []
    for _ in range(TOP_K):
        m = jnp.max(logits, axis=0, keepdims=True)
        am = jnp.min(jnp.where(logits == m, eidx, n_exp), axis=0, keepdims=True)
        vals.append(m)
        idxs.append(am)
        logits = jnp.where(eidx == am, -jnp.inf, logits)
    ex = [jnp.exp(v - vals[0]) for v in vals]
    den = ex[0] + ex[1] + ex[2] + ex[3]
    idx_ref[0] = jnp.concatenate(idxs, axis=0)
    gate_ref[0] = jnp.concatenate([e / den for e in ex], axis=0)

    @pl.when((pl.program_id(0) == 0) & (pl.program_id(1) == 0))
    def _():
        cnt_ref[...] = jnp.zeros_like(cnt_ref)

    hits = [eidx == am for am in idxs]
    sel = (hits[0] | hits[1] | hits[2] | hits[3]).astype(F32)
    earlier = (lax.broadcasted_iota(jnp.int32, (tm, tm), 0)
               < lax.broadcasted_iota(jnp.int32, (tm, tm), 1)).astype(BF16)
    before = _dot(sel.astype(BF16), earlier) + cnt_ref[...]
    rank_ref[0] = jnp.concatenate(
        [jnp.sum(jnp.where(h, before, 0.0), axis=0, keepdims=True) for h in hits], axis=0).astype(jnp.int32)
    cnt_ref[...] += jnp.sum(sel, axis=1, keepdims=True)
    count_ref[...] = cnt_ref[...].astype(jnp.int32)


def _outproj(att, rec, x, wo_b, g1, n2g, sc2, sh2, wr_t, br, n_exp, b0, B):
    _, S, D = x.shape
    attn_w = att.shape[2]
    lru_w = rec.shape[2]
    tm = min(TM_PROJ, S)
    ns = S // tm
    rows = D // LANES
    kern = functools.partial(_outproj_kernel, attn_w=attn_w, n_exp=n_exp)
    bvec = pl.BlockSpec((1, 1, D), lambda b, i: (b + b0, 0, 0))
    return pl.pallas_call(
        kern,
        grid=(B, ns),
        in_specs=[pl.BlockSpec((1, tm, attn_w), lambda b, i: (b + b0, i, 0)),
                  pl.BlockSpec((1, tm, lru_w), lambda b, i: (b + b0, i, 0)),
                  pl.BlockSpec((1, tm, D), lambda b, i: (b + b0, i, 0)),
                  pl.BlockSpec((attn_w + lru_w, D), lambda b, i: (0, 0)),
                  bvec,
                  pl.BlockSpec((1, D), lambda b, i: (0, 0)),
                  bvec, bvec,
                  pl.BlockSpec((n_exp, D), lambda b, i: (0, 0)),
                  pl.BlockSpec((n_exp, 1), lambda b, i: (0, 0))],
        out_specs=[pl.BlockSpec((1, tm, D), lambda b, i: (b, i, 0)),
                   pl.BlockSpec((tm * rows, LANES), lambda b, i: (b * ns + i, 0)),
                   pl.BlockSpec((1, TOP_K, tm), lambda b, i: (b, 0, i)),
                   pl.BlockSpec((1, TOP_K, tm), lambda b, i: (b, 0, i)),
                   pl.BlockSpec((1, TOP_K, tm), lambda b, i: (b, 0, i)),
                   pl.BlockSpec((n_exp, 1), lambda b, i: (0, 0))],
        out_shape=[jax.ShapeDtypeStruct((B, S, D), F32),
                   jax.ShapeDtypeStruct((B * S * rows, LANES), F32),
                   jax.ShapeDtypeStruct((B, TOP_K, S), jnp.int32),
                   jax.ShapeDtypeStruct((B, TOP_K, S), F32),
                   jax.ShapeDtypeStruct((B, TOP_K, S), jnp.int32),
                   jax.ShapeDtypeStruct((n_exp, 1), jnp.int32)],
        scratch_shapes=[pltpu.VMEM((n_exp, 1), F32)],
        compiler_params=pltpu.CompilerParams(dimension_semantics=("arbitrary", "arbitrary"),
                                             vmem_limit_bytes=VMEM_LIMIT),
        name="outproj",
    )(att, rec, x, wo_b, g1, n2g, sc2, sh2, wr_t, br)


def _sc_gather_rows(table, idx):
    n = idx.shape[0]
    workers = SC_CORES * SC_SUBCORES
    per_worker = n // workers
    assert n % (workers * SC_CHUNK) == 0, "index count must split evenly into per-subcore chunks"
    n_chunks = per_worker // SC_CHUNK
    mesh = plsc.VectorSubcoreMesh(core_axis_name="c", subcore_axis_name="s",
                                  num_cores=SC_CORES, num_subcores=SC_SUBCORES)

    @functools.partial(
        pl.kernel, mesh=mesh,
        out_type=jax.ShapeDtypeStruct((n,) + table.shape[1:], table.dtype),
        scratch_types=[pltpu.VMEM((SC_CHUNK,), jnp.int32),
                       pltpu.VMEM((SC_CHUNK,) + table.shape[1:], table.dtype),
                       pltpu.SemaphoreType.DMA],
        name="sc_gather_rows",
    )
    def gather(table_hbm, idx_hbm, out_hbm, idx_v, rows_v, sem):
        base = (lax.axis_index("s") * SC_CORES + lax.axis_index("c")) * per_worker

        @pl.loop(0, n_chunks)
        def _(ci):
            off = pl.multiple_of(base + ci * SC_CHUNK, SC_CHUNK)
            pltpu.sync_copy(idx_hbm.at[pl.ds(off, SC_CHUNK)], idx_v)
            pltpu.async_copy(table_hbm.at[idx_v], rows_v, sem).wait()
            pltpu.sync_copy(rows_v, out_hbm.at[pl.ds(off, SC_CHUNK)])

    return gather(table, idx)


def _sc_scatter_rows(src, pos, n_out):
    n_tok = src.shape[0]
    n_k = pos.shape[1]
    workers = SC_CORES * SC_SUBCORES
    per_worker = n_tok // workers
    assert n_tok % (workers * SC_CHUNK) == 0, "token count must split evenly into per-subcore chunks"
    n_chunks = per_worker // SC_CHUNK
    mesh = plsc.VectorSubcoreMesh(core_axis_name="c", subcore_axis_name="s",
                                  num_cores=SC_CORES, num_subcores=SC_SUBCORES)

    @functools.partial(
        pl.kernel, mesh=mesh,
        out_type=jax.ShapeDtypeStruct((n_out,) + src.shape[1:], src.dtype),
        scratch_types=[pltpu.VMEM((n_k, SC_CHUNK), jnp.int32),
                       pltpu.VMEM((SC_CHUNK,) + src.shape[1:], src.dtype),
                       pltpu.SemaphoreType.DMA],
        name="sc_scatter_rows",
    )
    def scatter(src_hbm, pos_hbm, out_hbm, idx_v, rows_v, sem):
        base = (lax.axis_index("s") * SC_CORES + lax.axis_index("c")) * n_chunks

        @pl.loop(0, n_chunks)
        def _(ci):
            chunk = base + ci
            pltpu.sync_copy(pos_hbm.at[chunk], idx_v)
            pltpu.sync_copy(src_hbm.at[pl.ds(pl.multiple_of(chunk * SC_CHUNK, SC_CHUNK), SC_CHUNK)], rows_v)
            copies = [pltpu.async_copy(rows_v, out_hbm.at[idx_v.at[k]], sem) for k in range(n_k)]
            for cp in copies:
                cp.wait()

    return scatter(src, pos)


def _moe_kernel(be_ref, nv_ref, nused_ref, x_ref, w1_ref, b1_ref, w2_ref, b2_ref, y_ref, w1b_ref, w2b_ref,
                *, d_ff):
    i = pl.program_id(0)
    rows = SUBLANES

    @pl.when(i < nused_ref[0])
    def _():
        @pl.when((i == 0) | (be_ref[i] != be_ref[jnp.maximum(i - 1, 0)]))
        def _():
            w1b_ref[...] = w1_ref[0].astype(BF16)
            w2b_ref[...] = w2_ref[0].astype(BF16)

        x = jnp.concatenate([x_ref[pl.ds(c, EB, stride=rows), :] for c in range(rows)], axis=1)
        valid = lax.broadcasted_iota(jnp.int32, (EB, 1), 0) < nv_ref[i]
        x = jnp.where(valid, x, 0.0).astype(BF16)
        hu = _dot(x, w1b_ref[...]) + b1_ref[0]
        glu = jnp.minimum(hu[:, :d_ff], SWIGLU_LIMIT)
        lin = jnp.clip(hu[:, d_ff:], -SWIGLU_LIMIT, SWIGLU_LIMIT)
        act = glu * jax.nn.sigmoid(SWIGLU_ALPHA * glu) * (lin + 1.0)
        y = _dot(act.astype(BF16), w2b_ref[...]) + b2_ref[0]
        bits = lax.bitcast_convert_type(y.astype(BF16).astype(F32), jnp.uint32)
        for c in range(YROWS):
            even = bits[:, (2 * c) * LANES:(2 * c + 1) * LANES] >> 16
            odd = bits[:, (2 * c + 1) * LANES:(2 * c + 2) * LANES] & jnp.uint32(0xFFFF0000)
            y_ref[pl.ds(c, EB, stride=YROWS), :] = lax.bitcast_convert_type(even | odd, jnp.int32)

    @pl.when(i >= nused_ref[0])
    def _():
        y_ref[...] = jnp.zeros_like(y_ref)


def _moe(block_e, nv, n_used, xs2d, w1, b1, w2, b2):
    n_blocks = block_e.shape[0]
    n_exp, D, ff2 = w1.shape
    d_ff = ff2 // 2
    rows = D // LANES
    assert rows == SUBLANES, "a token row must fill exactly one (8,128) f32 tile"
    kern = functools.partial(_moe_kernel, d_ff=d_ff)
    grid_spec = pltpu.PrefetchScalarGridSpec(
        num_scalar_prefetch=3,
        grid=(n_blocks,),
        in_specs=[pl.BlockSpec((EB * rows, LANES), lambda i, be, nv, nu: (i, 0)),
                  pl.BlockSpec((1, D, ff2), lambda i, be, nv, nu: (be[i], 0, 0)),
                  pl.BlockSpec((1, 1, ff2), lambda i, be, nv, nu: (be[i], 0, 0)),
                  pl.BlockSpec((1, d_ff, D), lambda i, be, nv, nu: (be[i], 0, 0)),
                  pl.BlockSpec((1, 1, D), lambda i, be, nv, nu: (be[i], 0, 0))],
        out_specs=pl.BlockSpec((EB * YROWS, LANES), lambda i, be, nv, nu: (i, 0)),
        scratch_shapes=[pltpu.VMEM((D, ff2), BF16), pltpu.VMEM((d_ff, D), BF16)],
    )
    return pl.pallas_call(
        kern,
        grid_spec=grid_spec,
        out_shape=jax.ShapeDtypeStruct((n_blocks * EB * YROWS, LANES), jnp.int32),
        compiler_params=pltpu.CompilerParams(dimension_semantics=("arbitrary",),
                                             vmem_limit_bytes=VMEM_LIMIT),
        name="moe",
    )(block_e, nv, n_used, xs2d, w1, b1, w2, b2)


def _combine_kernel(y_ref, gate_ref, x1_ref, g2_ref, *rest):
    o_ref = rest[-1]
    tm = x1_ref.shape[1]
    diag = (lax.broadcasted_iota(jnp.int32, (tm, tm), 0) == lax.broadcasted_iota(jnp.int32, (tm, tm), 1))
    moe = jnp.zeros(x1_ref.shape[1:], F32)
    for k in range(TOP_K):
        chunks = []
        for c in range(YROWS):
            word = lax.bitcast_convert_type(y_ref[k, pl.ds(c, tm, stride=YROWS), :], jnp.uint32)
            chunks.append(lax.bitcast_convert_type(word << 16, F32))
            chunks.append(lax.bitcast_convert_type(word & jnp.uint32(0xFFFF0000), F32))
        yk = jnp.concatenate(chunks, axis=1)
        gate_col = jnp.sum(jnp.where(diag, gate_ref[0, k:k + 1, :], 0.0), axis=1, keepdims=True)
        moe = moe + gate_col * yk
    o_ref[0] = x1_ref[0] + g2_ref[0] * moe


def _combine(y4, gates, x1, g2, b0, n_batch, out_prev):
    B, S, D = x1.shape
    rows = YROWS
    assert D == 2 * YROWS * LANES
    tm = min(TM_COMB, S)
    ns = S // tm
    in_specs = [pl.BlockSpec((TOP_K, tm * rows, LANES), lambda b, i: (0, b * ns + i, 0)),
                pl.BlockSpec((1, TOP_K, tm), lambda b, i: (b, 0, i)),
                pl.BlockSpec((1, tm, D), lambda b, i: (b, i, 0)),
                pl.BlockSpec((1, 1, D), lambda b, i: (b + b0, 0, 0))]
    args = [y4.reshape(TOP_K, B * S * rows, LANES), gates, x1, g2]
    aliases = {}
    if out_prev is not None:
        in_specs.append(pl.BlockSpec(memory_space=pl.ANY))
        args.append(out_prev)
        aliases = {len(args) - 1: 0}
    return pl.pallas_call(
        _combine_kernel,
        grid=(B, ns),
        in_specs=in_specs,
        out_specs=pl.BlockSpec((1, tm, D), lambda b, i: (b + b0, i, 0)),
        out_shape=jax.ShapeDtypeStruct((n_batch, S, D), F32),
        input_output_aliases=aliases,
        compiler_params=pltpu.CompilerParams(dimension_semantics=("arbitrary", "arbitrary"),
                                             vmem_limit_bytes=VMEM_LIMIT),
        name="combine",
    )(*args)


def _route(top_idx, rank, counts):
    B, K, S = top_idx.shape
    T = B * S
    n_exp = counts.shape[0]
    nblk = (counts + EB - 1) // EB
    bend = jnp.cumsum(nblk)
    bstart = bend - nblk
    n_blocks = -(-(T * K) // EB) + n_exp
    blk = jnp.arange(n_blocks, dtype=jnp.int32)
    block_e = jnp.minimum(jnp.sum(blk[:, None] >= bend[None, :], axis=1), n_exp - 1).astype(jnp.int32)
    nv = jnp.where(blk < bend[-1], jnp.clip(counts[block_e] - (blk - bstart[block_e]) * EB, 0, EB), 0)
    start_row = jnp.sum(jnp.where(top_idx[..., None] == jnp.arange(n_exp, dtype=jnp.int32),
                                  (bstart * EB).astype(jnp.int32), 0), axis=-1)
    pos = (start_row + rank).transpose(1, 0, 2).reshape(K, T).astype(jnp.int32)
    return block_e, nv.astype(jnp.int32), bend[-1:].astype(jnp.int32), pos


def _lambda_init(layer_idx):
    return 0.8 - 0.6 * math.exp(-0.3 * layer_idx)


def _block_diag(w):
    n, c, d = w.shape
    eye = jnp.eye(n, dtype=w.dtype)
    return (eye[:, None, :, None] * w[:, :, None, :]).reshape(n * c, n * d)


def kernel(x, c, w_ada, b_ada, norm1_g, w_in, q_norm_g, k_norm_g, lambda_q1, lambda_k1, lambda_q2, lambda_k2, subln_g, conv_w, conv_b, w_rg_a, b_rg_a, w_rg_x, b_rg_x, lru_lambda, w_out, norm2_g, w_router, b_router, w_moe1, b_moe1, w_moe2, b_moe2):
    B, S, D = x.shape
    depth = w_ada.shape[0]
    lru_w = conv_w.shape[2]
    attn_w = (w_in.shape[2] - 2 * lru_w) // 3
    n_exp = w_router.shape[2]
    n_heads = attn_w // HEAD_DIM
    assert attn_w % LANES == 0 and S % CHUNK == 0

    half = HEAD_DIM // 2
    inv = ROPE_THETA ** (-jnp.arange(half, dtype=F32) / half)
    ang = jnp.arange(S, dtype=jnp.int32).astype(F32)[:, None] * inv[None, :]
    cos_t = jnp.tile(jnp.concatenate([jnp.cos(ang), jnp.cos(ang)], axis=1), (1, n_heads))
    sin_t = jnp.tile(jnp.concatenate([-jnp.sin(ang), jnp.sin(ang)], axis=1), (1, n_heads))
    group = jnp.arange(attn_w, dtype=jnp.int32) // HEAD_DIM
    gsum = (group[:, None] == group[None, :]).astype(BF16)

    for l in range(depth):
        mod = _ada(c, w_ada[l], b_ada[l])
        sh1, sc1, g1, sh2, sc2, g2 = [m.reshape(B, 1, D) for m in jnp.split(mod, 6, axis=-1)]

        qt, k, vt, xr, gr = _inproj(
            x, norm1_g[l].reshape(1, D), sc1, sh1, w_in[l].astype(BF16), gsum,
            jnp.tile(q_norm_g[l], n_heads).reshape(1, attn_w), jnp.tile(k_norm_g[l], n_heads).reshape(1, attn_w),
            cos_t, sin_t, attn_w, lru_w)

        score_bound = (HEAD_DIM ** 0.5 * LOG2_E * BF16_SLACK) * jnp.max(jnp.abs(q_norm_g[l])) * jnp.max(jnp.abs(k_norm_g[l]))
        bounded = (score_bound <= SCORE_BOUND).astype(jnp.int32).reshape(1)
        att = _attn(bounded, qt, k, vt, lambda_q1[l].reshape(1, -1), lambda_k1[l].reshape(1, -1),
                    lambda_q2[l].reshape(1, -1), lambda_k2[l].reshape(1, -1),
                    subln_g[l].reshape(-1, 1), _lambda_init(l))

        rec = _lru(xr, gr, conv_w[l], conv_b[l].reshape(1, lru_w),
                   _block_diag(w_rg_a[l]).astype(BF16), b_rg_a[l].reshape(1, lru_w),
                   _block_diag(w_rg_x[l]).astype(BF16), b_rg_x[l].reshape(1, lru_w),
                   lru_lambda[l].reshape(1, lru_w))

        n_groups = MOE_GROUPS if B % MOE_GROUPS == 0 else 1
        nb = B // n_groups
        T = nb * S
        rows = D // LANES
        wo_b = w_out[l].astype(BF16)
        x_in, x_out = x, None
        for grp in range(n_groups):
            b0 = grp * nb
            x1, hm2d, top_idx, gates, rank, counts = _outproj(
                att, rec, x_in, wo_b, g1, norm2_g[l].reshape(1, D), sc2, sh2,
                w_router[l].T, b_router[l].reshape(n_exp, 1), n_exp, b0, nb)
            block_e, nv, n_used, pos = _route(top_idx, rank, counts.reshape(n_exp))
            pos_chunks = pos.reshape(TOP_K, T // SC_CHUNK, SC_CHUNK).transpose(1, 0, 2)
            xs = _sc_scatter_rows(hm2d.reshape(T, rows, LANES), pos_chunks, block_e.shape[0] * EB)
            ys = _moe(block_e, nv, n_used, xs.reshape(-1, LANES),
                      w_moe1[l], b_moe1[l].reshape(n_exp, 1, -1), w_moe2[l], b_moe2[l].reshape(n_exp, 1, -1))
            y4 = _sc_gather_rows(ys.reshape(-1, YROWS, LANES), pos.reshape(TOP_K * T))
            x_out = _combine(y4, gates, x1, g2, b0, B, x_out)
        x = x_out
    return x
```

```python
import functools
import math

import jax
import jax.numpy as jnp
from jax import lax
from jax.experimental import pallas as pl
from jax.experimental.pallas import tpu as pltpu
from jax.experimental.pallas import tpu_sc as plsc

F32 = jnp.float32
BF16 = jnp.bfloat16

CHUNK = 64
HEAD_DIM = 64
ROPE_THETA = 10000.0
LRU_BLOCK = 64
CONV_W = 4
LRU_C = 8.0
TOP_K = 4
SWIGLU_ALPHA = 1.702
SWIGLU_LIMIT = 7.0
NORM_EPS = 1e-6
SUBLN_EPS = 1e-5
LOG2_E = math.log2(math.e)
SCORE_BOUND = 60.0
BF16_SLACK = 1.02

LANES = 128
SUBLANES = 8
YROWS = 4
VMEM_LIMIT = 56 * 1024 * 1024
SC_CORES = 2
SC_SUBCORES = 16
SC_CHUNK = 64

TM_PROJ = 512
TQ = 512
KV_WIDE = 4
TT = 256
EB = 512
TM_COMB = 512
MOE_GROUPS = 2


def _split_bf16(a):
    hi = a.astype(BF16)
    lo = (a - hi.astype(F32)).astype(BF16)
    return hi, lo


def _dot(a, b):
    return jnp.dot(a, b, preferred_element_type=F32)


def _store_packed_rows(ref, val, n):
    bits = lax.bitcast_convert_type(val.astype(BF16).astype(F32), jnp.uint32)
    for c in range(YROWS):
        even = bits[:, (2 * c) * LANES:(2 * c + 1) * LANES] >> 16
        odd = bits[:, (2 * c + 1) * LANES:(2 * c + 2) * LANES] & jnp.uint32(0xFFFF0000)
        ref[pl.ds(c, n, stride=YROWS), :] = lax.bitcast_convert_type(even | odd, jnp.int32)


def _load_packed_rows(ref, lead, n):
    chunks = []
    for c in range(YROWS):
        word = lax.bitcast_convert_type(ref[lead + (pl.ds(c, n, stride=YROWS), slice(None))], jnp.uint32)
        chunks.append(lax.bitcast_convert_type(word << 16, F32))
        chunks.append(lax.bitcast_convert_type(word & jnp.uint32(0xFFFF0000), F32))
    return jnp.concatenate(chunks, axis=1)


def _dot3(a, b):
    ah, al = _split_bf16(a)
    bh, bl = _split_bf16(b)
    return _dot(ah, bh) + (_dot(ah, bl) + _dot(al, bh))


def _ada_kernel(c_ref, w_ref, b_ref, o_ref):
    c = c_ref[...]
    s = c * jax.nn.sigmoid(c)
    o_ref[...] = _dot3(s, w_ref[...]) + b_ref[...]


def _ada(c, w_ada, b_ada):
    B, D = c.shape
    N = w_ada.shape[1]
    tn = D
    return pl.pallas_call(
        _ada_kernel,
        grid=(N // tn,),
        in_specs=[pl.BlockSpec((B, D), lambda j: (0, 0)),
                  pl.BlockSpec((D, tn), lambda j: (0, j)),
                  pl.BlockSpec((1, tn), lambda j: (0, j))],
        out_specs=pl.BlockSpec((B, tn), lambda j: (0, j)),
        out_shape=jax.ShapeDtypeStruct((B, N), F32),
        compiler_params=pltpu.CompilerParams(dimension_semantics=("arbitrary",),
                                             vmem_limit_bytes=VMEM_LIMIT),
        name="ada",
    )(c, w_ada, b_ada.reshape(1, N))


def _inproj_kernel(x_ref, g_ref, sc_ref, sh_ref, w_ref, gsum_ref, qg_ref, kg_ref, cos_ref, sin_ref,
                   qt_ref, k_ref, vt_ref, xr_ref, gr_ref, *, attn_w, lru_w):
    x = x_ref[0]
    ms = jnp.mean(x * x, axis=-1, keepdims=True)
    hn = x * lax.rsqrt(ms + NORM_EPS) * g_ref[...]
    hn = (hn * (1.0 + sc_ref[0]) + sh_ref[0]).astype(BF16)

    cos = cos_ref[...]
    sin = sin_ref[...]
    gsum = gsum_ref[...]
    lane = lax.broadcasted_iota(jnp.int32, cos.shape, 1)
    first_half = (lane % HEAD_DIM) < (HEAD_DIM // 2)

    def head_norm_rope(t, gain):
        ssum = _dot((t * t).astype(BF16), gsum)
        t = t * lax.rsqrt(ssum * (1.0 / HEAD_DIM) + NORM_EPS) * gain
        half = HEAD_DIM // 2
        partner = jnp.where(first_half, pltpu.roll(t, attn_w - half, 1), pltpu.roll(t, half, 1))
        return t * cos + partner * sin

    q = head_norm_rope(_dot(hn, w_ref[:, 0:attn_w]), qg_ref[...]) * (HEAD_DIM ** -0.5 * LOG2_E)
    for h in range(attn_w // LANES):
        qt_ref[0, h] = q[:, h * LANES:(h + 1) * LANES].T.astype(BF16)

    k = head_norm_rope(_dot(hn, w_ref[:, attn_w:2 * attn_w]), kg_ref[...])
    k_ref[0] = k.astype(BF16)

    v = _dot(hn, w_ref[:, 2 * attn_w:3 * attn_w])
    for h in range(attn_w // LANES):
        vt_ref[0, h] = v[:, h * LANES:(h + 1) * LANES].T.astype(BF16)
    xr_ref[0] = _dot(hn, w_ref[:, 3 * attn_w:3 * attn_w + lru_w])
    gr_ref[0] = _dot(hn, w_ref[:, 3 * attn_w + lru_w:3 * attn_w + 2 * lru_w])


def _inproj(x, g1n, sc1, sh1, w_in_b, gsum, qg, kg, cos_t, sin_t, attn_w, lru_w):
    B, S, D = x.shape
    tm = min(TM_PROJ, S)
    in_w = w_in_b.shape[1]
    nh = attn_w // LANES
    kern = functools.partial(_inproj_kernel, attn_w=attn_w, lru_w=lru_w)
    return pl.pallas_call(
        kern,
        grid=(S // tm, B),
        in_specs=[pl.BlockSpec((1, tm, D), lambda i, b: (b, i, 0)),
                  pl.BlockSpec((1, D), lambda i, b: (0, 0)),
                  pl.BlockSpec((1, 1, D), lambda i, b: (b, 0, 0)),
                  pl.BlockSpec((1, 1, D), lambda i, b: (b, 0, 0)),
                  pl.BlockSpec((D, in_w), lambda i, b: (0, 0)),
                  pl.BlockSpec((attn_w, attn_w), lambda i, b: (0, 0)),
                  pl.BlockSpec((1, attn_w), lambda i, b: (0, 0)),
                  pl.BlockSpec((1, attn_w), lambda i, b: (0, 0)),
                  pl.BlockSpec((tm, attn_w), lambda i, b: (i, 0)),
                  pl.BlockSpec((tm, attn_w), lambda i, b: (i, 0))],
        out_specs=[pl.BlockSpec((1, nh, LANES, tm), lambda i, b: (b, 0, 0, i)),
                   pl.BlockSpec((1, tm, attn_w), lambda i, b: (b, i, 0)),
                   pl.BlockSpec((1, nh, LANES, tm), lambda i, b: (b, 0, 0, i)),
                   pl.BlockSpec((1, tm, lru_w), lambda i, b: (b, i, 0)),
                   pl.BlockSpec((1, tm, lru_w), lambda i, b: (b, i, 0))],
        out_shape=[jax.ShapeDtypeStruct((B, nh, LANES, S), BF16),
                   jax.ShapeDtypeStruct((B, S, attn_w), BF16),
                   jax.ShapeDtypeStruct((B, nh, LANES, S), BF16),
                   jax.ShapeDtypeStruct((B, S, lru_w), F32),
                   jax.ShapeDtypeStruct((B, S, lru_w), F32)],
        compiler_params=pltpu.CompilerParams(dimension_semantics=("arbitrary", "arbitrary"),
                                             vmem_limit_bytes=VMEM_LIMIT),
        name="inproj",
    )(x, g1n, sc1, sh1, w_in_b, gsum, qg, kg, cos_t, sin_t)


def _attn_kernel(bounded_ref, qt_ref, k_ref, vt_ref, vis_ref, lq1_ref, lk1_ref, lq2_ref, lk2_ref, sg_ref, o_ref,
                 *, tq, wide, lam_init):
    qi = pl.program_id(2)
    qt = qt_ref[0, 0]
    row = lax.broadcasted_iota(jnp.int32, qt.shape, 0)
    zero = jnp.zeros_like(qt)
    qs = jnp.concatenate([jnp.where(row < HEAD_DIM, qt, zero),
                          jnp.where(row >= HEAD_DIM, qt, zero)], axis=1)

    def scores(start, size):
        kk = k_ref[0, pl.ds(pl.multiple_of(start, tq), size), :]
        return _dot(kk, qs)

    def values(start, size, p):
        vt = vt_ref[0, 0, :, pl.ds(pl.multiple_of(start, tq), size)]
        return _dot(vt, p)

    zeros = (jnp.zeros((1, 2 * tq), F32), jnp.zeros((LANES, 2 * tq), F32))

    def bounded_path():
        def tile(start, size, carry, vis=None):
            l, acc = carry
            p = jnp.exp2(scores(start, size))
            if vis is not None:
                p = p * vis
            return l + jnp.sum(p, axis=0, keepdims=True), acc + values(start, size, p.astype(BF16))

        carry = lax.fori_loop(0, qi // wide, lambda j, c: tile(j * (wide * tq), wide * tq, c), zeros)
        start = (qi // wide) * (wide * tq)
        piece = wide // 2
        while piece >= 1:
            take = ((qi % wide) & piece) != 0
            carry = lax.cond(take, lambda c, s=start, n=piece * tq: tile(s, n, c), lambda c: c, carry)
            start = start + jnp.where(take, piece * tq, 0)
            piece //= 2
        return tile(qi * tq, tq, carry, vis_ref[...])

    def online_path():
        def update(j, carry, masked):
            m, l, acc = carry
            s = scores(j * tq, tq)
            if masked:
                s = jnp.where(vis_ref[...] > 0.0, s, -jnp.inf)
            m_new = jnp.maximum(m, jnp.max(s, axis=0, keepdims=True))
            alpha = jnp.exp2(m - m_new)
            p = jnp.exp2(s - m_new)
            l = alpha * l + jnp.sum(p, axis=0, keepdims=True)
            return m_new, l, alpha * acc + values(j * tq, tq, p.astype(BF16))

        carry = (jnp.full((1, 2 * tq), -jnp.inf, F32),) + zeros
        carry = lax.fori_loop(0, qi, lambda j, c: update(j, c, False), carry)
        _, l, acc = update(qi, carry, True)
        return l, acc

    l, acc = lax.cond(bounded_ref[0] != 0, bounded_path, online_path)

    lam = (jnp.exp(jnp.sum(lq1_ref[...] * lk1_ref[...], axis=-1, keepdims=True))
           - jnp.exp(jnp.sum(lq2_ref[...] * lk2_ref[...], axis=-1, keepdims=True)) + lam_init)
    o = acc * (1.0 / l)
    o = o[:, :tq] - lam * o[:, tq:]
    ms = jnp.mean(o * o, axis=0, keepdims=True)
    o = o * lax.rsqrt(ms + SUBLN_EPS) * (sg_ref[...] * (1.0 - lam_init))
    o_ref[0] = o.T.astype(BF16)


def _attn(bounded, qt, k, vt, lq1, lk1, lq2, lk2, sg, lam_init):
    B, S, attn_w = k.shape
    nh = attn_w // LANES
    tq = min(TQ, S)
    wide = max(min(KV_WIDE, S // tq), 1)
    assert tq % CHUNK == 0 and wide & (wide - 1) == 0
    kern = functools.partial(_attn_kernel, tq=tq, wide=wide, lam_init=lam_init)
    vec = lambda n: pl.BlockSpec((1, n), lambda b, h, i: (0, 0))
    key_chunk = jnp.arange(tq, dtype=jnp.int32)[:, None] // CHUNK
    qry_chunk = (jnp.arange(2 * tq, dtype=jnp.int32)[None, :] % tq) // CHUNK
    vis = (key_chunk <= qry_chunk).astype(F32)
    return pl.pallas_call(
        kern,
        grid=(B, nh, S // tq),
        in_specs=[pl.BlockSpec(memory_space=pltpu.SMEM),
                  pl.BlockSpec((1, 1, LANES, tq), lambda b, h, i: (b, h, 0, i)),
                  pl.BlockSpec((1, S, LANES), lambda b, h, i: (b, 0, h)),
                  pl.BlockSpec((1, 1, LANES, S), lambda b, h, i: (b, h, 0, 0)),
                  pl.BlockSpec((tq, 2 * tq), lambda b, h, i: (0, 0)),
                  vec(HEAD_DIM), vec(HEAD_DIM), vec(HEAD_DIM), vec(HEAD_DIM),
                  pl.BlockSpec((LANES, 1), lambda b, h, i: (0, 0))],
        out_specs=pl.BlockSpec((1, tq, LANES), lambda b, h, i: (b, i, h)),
        out_shape=jax.ShapeDtypeStruct((B, S, attn_w), BF16),
        compiler_params=pltpu.CompilerParams(
            dimension_semantics=("arbitrary", "arbitrary", "arbitrary"), vmem_limit_bytes=VMEM_LIMIT),
        name="attn",
    )(bounded, qt, k, vt, vis, lq1, lk1, lq2, lk2, sg)


def _lru_kernel(xr_ref, gr_ref, cw_ref, cb_ref, wa_ref, ba_ref, wx_ref, bx_ref, lam_ref, o_ref,
                tail_ref, h_ref, *, tt):
    @pl.when(pl.program_id(1) == 0)
    def _():
        tail_ref[...] = jnp.zeros_like(tail_ref)
        h_ref[...] = jnp.zeros_like(h_ref)

    x = xr_ref[0]
    ext = jnp.concatenate([tail_ref[...], x], axis=0)
    xc = cb_ref[...] + cw_ref[CONV_W - 1:CONV_W, :] * x
    for j in range(CONV_W - 1):
        back = CONV_W - 1 - j
        xc = xc + cw_ref[j:j + 1, :] * ext[SUBLANES - back:SUBLANES - back + tt]
    tail_ref[...] = x[tt - SUBLANES:]

    xcb = xc.astype(BF16)
    r = jax.nn.sigmoid(_dot(xcb, wa_ref[...]) + ba_ref[...])
    i = jax.nn.sigmoid(_dot(xcb, wx_ref[...]) + bx_ref[...])
    nl = -lam_ref[...]
    softplus = jnp.maximum(nl, 0.0) + jnp.log(1.0 + jnp.exp(-jnp.abs(nl)))
    log_a = (-LRU_C) * r * softplus
    a = jnp.exp(log_a)
    th = jnp.tanh(log_a)
    u = jnp.sqrt((-2.0) * th / (1.0 - th)) * (i * xc)

    n_groups = tt // SUBLANES
    a = a.reshape(n_groups, SUBLANES, a.shape[1])
    u = u.reshape(a.shape)
    in_group = lax.broadcasted_iota(jnp.int32, a.shape, 1)
    d = 1
    while d < SUBLANES:
        keep = in_group >= d
        u = u + a * jnp.where(keep, pltpu.roll(u, d, 1), 0.0)
        a = a * jnp.where(keep, pltpu.roll(a, d, 1), 1.0)
        d *= 2
    carry = h_ref[0:1, :]
    groups = []
    for g in range(n_groups):
        hg = a[g] * carry + u[g]
        groups.append(hg)
        carry = hg[SUBLANES - 1:SUBLANES, :]
    h = jnp.concatenate(groups, axis=0)
    h_ref[...] = jnp.broadcast_to(carry, h_ref.shape)

    g = gr_ref[0]
    gelu = 0.5 * g * (1.0 + jnp.tanh(math.sqrt(2.0 / math.pi) * (g + 0.044715 * (g * g * g))))
    o_ref[0] = (h * gelu).astype(BF16)


def _lru(xr, gr, conv_w, conv_b, wa_bd, b_a, wx_bd, b_x, lam):
    B, S, W = xr.shape
    tt = min(TT, S)
    kern = functools.partial(_lru_kernel, tt=tt)
    vec = pl.BlockSpec((1, W), lambda b, i: (0, 0))
    mat = pl.BlockSpec((W, W), lambda b, i: (0, 0))
    return pl.pallas_call(
        kern,
        grid=(B, S // tt),
        in_specs=[pl.BlockSpec((1, tt, W), lambda b, i: (b, i, 0)),
                  pl.BlockSpec((1, tt, W), lambda b, i: (b, i, 0)),
                  pl.BlockSpec((CONV_W, W), lambda b, i: (0, 0)),
                  vec, mat, vec, mat, vec, vec],
        out_specs=pl.BlockSpec((1, tt, W), lambda b, i: (b, i, 0)),
        out_shape=jax.ShapeDtypeStruct((B, S, W), BF16),
        scratch_shapes=[pltpu.VMEM((SUBLANES, W), F32), pltpu.VMEM((SUBLANES, W), F32)],
        compiler_params=pltpu.CompilerParams(dimension_semantics=("arbitrary", "arbitrary"),
                                             vmem_limit_bytes=VMEM_LIMIT),
        name="lru",
    )(xr, gr, conv_w, conv_b, wa_bd, b_a, wx_bd, b_x, lam)


def _outproj_kernel(att_ref, rec_ref, x_ref, wo_ref, g1_ref, n2_ref, sc_ref, sh_ref, wr_ref, br_ref,
                    x1_ref, hm_ref, idx_ref, gate_ref, rank_ref, count_ref, cnt_ref, *, attn_w, n_exp):
    tm = x_ref.shape[1]
    mix = _dot(att_ref[0], wo_ref[0:attn_w, :]) + _dot(rec_ref[0], wo_ref[attn_w:, :])
    x1 = x_ref[0] + g1_ref[0] * mix
    x1_ref[0] = x1
    ms = jnp.mean(x1 * x1, axis=-1, keepdims=True)
    hm = x1 * lax.rsqrt(ms + NORM_EPS) * n2_ref[...]
    hm = hm * (1.0 + sc_ref[0]) + sh_ref[0]
    _store_packed_rows(hm_ref, hm, tm)

    hh, hl = _split_bf16(hm)
    wh, wl = _split_bf16(wr_ref[...])
    nt = (((1,), (1,)), ((), ()))
    logits = (lax.dot_general(wh, hh, nt, preferred_element_type=F32)
              + (lax.dot_general(wh, hl, nt, preferred_element_type=F32)
                 + lax.dot_general(wl, hh, nt, preferred_element_type=F32))) + br_ref[...]
    eidx = lax.broadcasted_iota(jnp.int32, logits.shape, 0)
    vals, idxs = [], []
    for _ in range(TOP_K):
        m = jnp.max(logits, axis=0, keepdims=True)
        am = jnp.min(jnp.where(logits == m, eidx, n_exp), axis=0, keepdims=True)
        vals.append(m)
        idxs.append(am)
        logits = jnp.where(eidx == am, -jnp.inf, logits)
    ex = [jnp.exp(v - vals[0]) for v in vals]
    den = ex[0] + ex[1] + ex[2] + ex[3]
    idx_ref[0] = jnp.concatenate(idxs, axis=0)
    gate_ref[0] = jnp.concatenate([e / den for e in ex], axis=0)

    @pl.when((pl.program_id(0) == 0) & (pl.program_id(1) == 0))
    def _():
        cnt_ref[...] = jnp.zeros_like(cnt_ref)

    hits = [eidx == am for am in idxs]
    sel = (hits[0] | hits[1] | hits[2] | hits[3]).astype(F32)
    earlier = (lax.broadcasted_iota(jnp.int32, (tm, tm), 0)
               < lax.broadcasted_iota(jnp.int32, (tm, tm), 1)).astype(BF16)
    before = _dot(sel.astype(BF16), earlier) + cnt_ref[...]
    rank_ref[0] = jnp.concatenate(
        [jnp.sum(jnp.where(h, before, 0.0), axis=0, keepdims=True) for h in hits], axis=0).astype(jnp.int32)
    cnt_ref[...] += jnp.sum(sel, axis=1, keepdims=True)
    count_ref[...] = cnt_ref[...].astype(jnp.int32)


def _outproj(att, rec, x, wo_b, g1, n2g, sc2, sh2, wr_t, br, n_exp, b0, B):
    _, S, D = x.shape
    attn_w = att.shape[2]
    lru_w = rec.shape[2]
    tm = min(TM_PROJ, S)
    ns = S // tm
    rows = YROWS
    assert D == 2 * YROWS * LANES
    kern = functools.partial(_outproj_kernel, attn_w=attn_w, n_exp=n_exp)
    bvec = pl.BlockSpec((1, 1, D), lambda b, i: (b + b0, 0, 0))
    return pl.pallas_call(
        kern,
        grid=(B, ns),
        in_specs=[pl.BlockSpec((1, tm, attn_w), lambda b, i: (b + b0, i, 0)),
                  pl.BlockSpec((1, tm, lru_w), lambda b, i: (b + b0, i, 0)),
                  pl.BlockSpec((1, tm, D), lambda b, i: (b + b0, i, 0)),
                  pl.BlockSpec((attn_w + lru_w, D), lambda b, i: (0, 0)),
                  bvec,
                  pl.BlockSpec((1, D), lambda b, i: (0, 0)),
                  bvec, bvec,
                  pl.BlockSpec((n_exp, D), lambda b, i: (0, 0)),
                  pl.BlockSpec((n_exp, 1), lambda b, i: (0, 0))],
        out_specs=[pl.BlockSpec((1, tm, D), lambda b, i: (b, i, 0)),
                   pl.BlockSpec((tm * rows, LANES), lambda b, i: (b * ns + i, 0)),
                   pl.BlockSpec((1, TOP_K, tm), lambda b, i: (b, 0, i)),
                   pl.BlockSpec((1, TOP_K, tm), lambda b, i: (b, 0, i)),
                   pl.BlockSpec((1, TOP_K, tm), lambda b, i: (b, 0, i)),
                   pl.BlockSpec((n_exp, 1), lambda b, i: (0, 0))],
        out_shape=[jax.ShapeDtypeStruct((B, S, D), F32),
                   jax.ShapeDtypeStruct((B * S * rows, LANES), jnp.int32),
                   jax.ShapeDtypeStruct((B, TOP_K, S), jnp.int32),
                   jax.ShapeDtypeStruct((B, TOP_K, S), F32),
                   jax.ShapeDtypeStruct((B, TOP_K, S), jnp.int32),
                   jax.ShapeDtypeStruct((n_exp, 1), jnp.int32)],
        scratch_shapes=[pltpu.VMEM((n_exp, 1), F32)],
        compiler_params=pltpu.CompilerParams(dimension_semantics=("arbitrary", "arbitrary"),
                                             vmem_limit_bytes=VMEM_LIMIT),
        name="outproj",
    )(att, rec, x, wo_b, g1, n2g, sc2, sh2, wr_t, br)


def _sc_gather_rows(table, idx):
    n = idx.shape[0]
    workers = SC_CORES * SC_SUBCORES
    per_worker = n // workers
    assert n % (workers * SC_CHUNK) == 0, "index count must split evenly into per-subcore chunks"
    n_chunks = per_worker // SC_CHUNK
    mesh = plsc.VectorSubcoreMesh(core_axis_name="c", subcore_axis_name="s",
                                  num_cores=SC_CORES, num_subcores=SC_SUBCORES)

    @functools.partial(
        pl.kernel, mesh=mesh,
        out_type=jax.ShapeDtypeStruct((n,) + table.shape[1:], table.dtype),
        scratch_types=[pltpu.VMEM((SC_CHUNK,), jnp.int32),
                       pltpu.VMEM((SC_CHUNK,) + table.shape[1:], table.dtype),
                       pltpu.SemaphoreType.DMA],
        name="sc_gather_rows",
    )
    def gather(table_hbm, idx_hbm, out_hbm, idx_v, rows_v, sem):
        base = (lax.axis_index("s") * SC_CORES + lax.axis_index("c")) * per_worker

        @pl.loop(0, n_chunks)
        def _(ci):
            off = pl.multiple_of(base + ci * SC_CHUNK, SC_CHUNK)
            pltpu.sync_copy(idx_hbm.at[pl.ds(off, SC_CHUNK)], idx_v)
            pltpu.async_copy(table_hbm.at[idx_v], rows_v, sem).wait()
            pltpu.sync_copy(rows_v, out_hbm.at[pl.ds(off, SC_CHUNK)])

    return gather(table, idx)


def _sc_scatter_rows(src, pos, n_out):
    n_tok = src.shape[0]
    n_k = pos.shape[1]
    workers = SC_CORES * SC_SUBCORES
    per_worker = n_tok // workers
    assert n_tok % (workers * SC_CHUNK) == 0, "token count must split evenly into per-subcore chunks"
    n_chunks = per_worker // SC_CHUNK
    mesh = plsc.VectorSubcoreMesh(core_axis_name="c", subcore_axis_name="s",
                                  num_cores=SC_CORES, num_subcores=SC_SUBCORES)

    @functools.partial(
        pl.kernel, mesh=mesh,
        out_type=jax.ShapeDtypeStruct((n_out,) + src.shape[1:], src.dtype),
        scratch_types=[pltpu.VMEM((n_k, SC_CHUNK), jnp.int32),
                       pltpu.VMEM((SC_CHUNK,) + src.shape[1:], src.dtype),
                       pltpu.SemaphoreType.DMA],
        name="sc_scatter_rows",
    )
    def scatter(src_hbm, pos_hbm, out_hbm, idx_v, rows_v, sem):
        base = (lax.axis_index("s") * SC_CORES + lax.axis_index("c")) * n_chunks

        @pl.loop(0, n_chunks)
        def _(ci):
            chunk = base + ci
            pltpu.sync_copy(pos_hbm.at[chunk], idx_v)
            pltpu.sync_copy(src_hbm.at[pl.ds(pl.multiple_of(chunk * SC_CHUNK, SC_CHUNK), SC_CHUNK)], rows_v)
            copies = [pltpu.async_copy(rows_v, out_hbm.at[idx_v.at[k]], sem) for k in range(n_k)]
            for cp in copies:
                cp.wait()

    return scatter(src, pos)


def _moe_kernel(be_ref, nv_ref, nused_ref, x_ref, w1_ref, b1_ref, w2_ref, b2_ref, y_ref, w1b_ref, w2b_ref,
                *, d_ff):
    i = pl.program_id(0)

    @pl.when(i < nused_ref[0])
    def _():
        @pl.when((i == 0) | (be_ref[i] != be_ref[jnp.maximum(i - 1, 0)]))
        def _():
            w1b_ref[...] = w1_ref[0].astype(BF16)
            w2b_ref[...] = w2_ref[0].astype(BF16)

        valid = lax.broadcasted_iota(jnp.int32, (EB, 1), 0) < nv_ref[i]
        x = jnp.where(valid, _load_packed_rows(x_ref, (), EB), 0.0).astype(BF16)
        hu = _dot(x, w1b_ref[...]) + b1_ref[0]
        glu = jnp.minimum(hu[:, :d_ff], SWIGLU_LIMIT)
        lin = jnp.clip(hu[:, d_ff:], -SWIGLU_LIMIT, SWIGLU_LIMIT)
        act = glu * jax.nn.sigmoid(SWIGLU_ALPHA * glu) * (lin + 1.0)
        y = _dot(act.astype(BF16), w2b_ref[...]) + b2_ref[0]
        _store_packed_rows(y_ref, y, EB)

    @pl.when(i >= nused_ref[0])
    def _():
        y_ref[...] = jnp.zeros_like(y_ref)


def _moe(block_e, nv, n_used, xs2d, w1, b1, w2, b2):
    n_blocks = block_e.shape[0]
    n_exp, D, ff2 = w1.shape
    d_ff = ff2 // 2
    assert D == 2 * YROWS * LANES, "a token row must fill exactly one packed (YROWS,128) word tile"
    kern = functools.partial(_moe_kernel, d_ff=d_ff)
    grid_spec = pltpu.PrefetchScalarGridSpec(
        num_scalar_prefetch=3,
        grid=(n_blocks,),
        in_specs=[pl.BlockSpec((EB * YROWS, LANES), lambda i, be, nv, nu: (i, 0)),
                  pl.BlockSpec((1, D, ff2), lambda i, be, nv, nu: (be[i], 0, 0)),
                  pl.BlockSpec((1, 1, ff2), lambda i, be, nv, nu: (be[i], 0, 0)),
                  pl.BlockSpec((1, d_ff, D), lambda i, be, nv, nu: (be[i], 0, 0)),
                  pl.BlockSpec((1, 1, D), lambda i, be, nv, nu: (be[i], 0, 0))],
        out_specs=pl.BlockSpec((EB * YROWS, LANES), lambda i, be, nv, nu: (i, 0)),
        scratch_shapes=[pltpu.VMEM((D, ff2), BF16), pltpu.VMEM((d_ff, D), BF16)],
    )
    return pl.pallas_call(
        kern,
        grid_spec=grid_spec,
        out_shape=jax.ShapeDtypeStruct((n_blocks * EB * YROWS, LANES), jnp.int32),
        compiler_params=pltpu.CompilerParams(dimension_semantics=("arbitrary",),
                                             vmem_limit_bytes=VMEM_LIMIT),
        name="moe",
    )(block_e, nv, n_used, xs2d, w1, b1, w2, b2)


def _combine_kernel(y_ref, gate_ref, x1_ref, g2_ref, *rest):
    o_ref = rest[-1]
    tm = x1_ref.shape[1]
    diag = (lax.broadcasted_iota(jnp.int32, (tm, tm), 0) == lax.broadcasted_iota(jnp.int32, (tm, tm), 1))
    moe = jnp.zeros(x1_ref.shape[1:], F32)
    for k in range(TOP_K):
        yk = _load_packed_rows(y_ref, (k,), tm)
        gate_col = jnp.sum(jnp.where(diag, gate_ref[0, k:k + 1, :], 0.0), axis=1, keepdims=True)
        moe = moe + gate_col * yk
    o_ref[0] = x1_ref[0] + g2_ref[0] * moe


def _combine(y4, gates, x1, g2, b0, n_batch, out_prev):
    B, S, D = x1.shape
    rows = YROWS
    assert D == 2 * YROWS * LANES
    tm = min(TM_COMB, S)
    ns = S // tm
    in_specs = [pl.BlockSpec((TOP_K, tm * rows, LANES), lambda b, i: (0, b * ns + i, 0)),
                pl.BlockSpec((1, TOP_K, tm), lambda b, i: (b, 0, i)),
                pl.BlockSpec((1, tm, D), lambda b, i: (b, i, 0)),
                pl.BlockSpec((1, 1, D), lambda b, i: (b + b0, 0, 0))]
    args = [y4.reshape(TOP_K, B * S * rows, LANES), gates, x1, g2]
    aliases = {}
    if out_prev is not None:
        in_specs.append(pl.BlockSpec(memory_space=pl.ANY))
        args.append(out_prev)
        aliases = {len(args) - 1: 0}
    return pl.pallas_call(
        _combine_kernel,
        grid=(B, ns),
        in_specs=in_specs,
        out_specs=pl.BlockSpec((1, tm, D), lambda b, i: (b + b0, i, 0)),
        out_shape=jax.ShapeDtypeStruct((n_batch, S, D), F32),
        input_output_aliases=aliases,
        compiler_params=pltpu.CompilerParams(dimension_semantics=("arbitrary", "arbitrary"),
                                             vmem_limit_bytes=VMEM_LIMIT),
        name="combine",
    )(*args)


def _route(top_idx, rank, counts):
    B, K, S = top_idx.shape
    T = B * S
    n_exp = counts.shape[0]
    nblk = (counts + EB - 1) // EB
    bend = jnp.cumsum(nblk)
    bstart = bend - nblk
    n_blocks = -(-(T * K) // EB) + n_exp
    blk = jnp.arange(n_blocks, dtype=jnp.int32)
    block_e = jnp.minimum(jnp.sum(blk[:, None] >= bend[None, :], axis=1), n_exp - 1).astype(jnp.int32)
    nv = jnp.where(blk < bend[-1], jnp.clip(counts[block_e] - (blk - bstart[block_e]) * EB, 0, EB), 0)
    start_row = jnp.sum(jnp.where(top_idx[..., None] == jnp.arange(n_exp, dtype=jnp.int32),
                                  (bstart * EB).astype(jnp.int32), 0), axis=-1)
    pos = (start_row + rank).transpose(1, 0, 2).reshape(K, T).astype(jnp.int32)
    return block_e, nv.astype(jnp.int32), bend[-1:].astype(jnp.int32), pos


def _lambda_init(layer_idx):
    return 0.8 - 0.6 * math.exp(-0.3 * layer_idx)


def _block_diag(w):
    n, c, d = w.shape
    eye = jnp.eye(n, dtype=w.dtype)
    return (eye[:, None, :, None] * w[:, :, None, :]).reshape(n * c, n * d)


def kernel(x, c, w_ada, b_ada, norm1_g, w_in, q_norm_g, k_norm_g, lambda_q1, lambda_k1, lambda_q2, lambda_k2, subln_g, conv_w, conv_b, w_rg_a, b_rg_a, w_rg_x, b_rg_x, lru_lambda, w_out, norm2_g, w_router, b_router, w_moe1, b_moe1, w_moe2, b_moe2):
    B, S, D = x.shape
    depth = w_ada.shape[0]
    lru_w = conv_w.shape[2]
    attn_w = (w_in.shape[2] - 2 * lru_w) // 3
    n_exp = w_router.shape[2]
    n_heads = attn_w // HEAD_DIM
    assert attn_w % LANES == 0 and S % CHUNK == 0

    half = HEAD_DIM // 2
    inv = ROPE_THETA ** (-jnp.arange(half, dtype=F32) / half)
    ang = jnp.arange(S, dtype=jnp.int32).astype(F32)[:, None] * inv[None, :]
    cos_t = jnp.tile(jnp.concatenate([jnp.cos(ang), jnp.cos(ang)], axis=1), (1, n_heads))
    sin_t = jnp.tile(jnp.concatenate([-jnp.sin(ang), jnp.sin(ang)], axis=1), (1, n_heads))
    group = jnp.arange(attn_w, dtype=jnp.int32) // HEAD_DIM
    gsum = (group[:, None] == group[None, :]).astype(BF16)

    for l in range(depth):
        mod = _ada(c, w_ada[l], b_ada[l])
        sh1, sc1, g1, sh2, sc2, g2 = [m.reshape(B, 1, D) for m in jnp.split(mod, 6, axis=-1)]

        qt, k, vt, xr, gr = _inproj(
            x, norm1_g[l].reshape(1, D), sc1, sh1, w_in[l].astype(BF16), gsum,
            jnp.tile(q_norm_g[l], n_heads).reshape(1, attn_w), jnp.tile(k_norm_g[l], n_heads).reshape(1, attn_w),
            cos_t, sin_t, attn_w, lru_w)

        score_bound = (HEAD_DIM ** 0.5 * LOG2_E * BF16_SLACK) * jnp.max(jnp.abs(q_norm_g[l])) * jnp.max(jnp.abs(k_norm_g[l]))
        bounded = (score_bound <= SCORE_BOUND).astype(jnp.int32).reshape(1)
        att = _attn(bounded, qt, k, vt, lambda_q1[l].reshape(1, -1), lambda_k1[l].reshape(1, -1),
                    lambda_q2[l].reshape(1, -1), lambda_k2[l].reshape(1, -1),
                    subln_g[l].reshape(-1, 1), _lambda_init(l))

        rec = _lru(xr, gr, conv_w[l], conv_b[l].reshape(1, lru_w),
                   _block_diag(w_rg_a[l]).astype(BF16), b_rg_a[l].reshape(1, lru_w),
                   _block_diag(w_rg_x[l]).astype(BF16), b_rg_x[l].reshape(1, lru_w),
                   lru_lambda[l].reshape(1, lru_w))

        n_groups = MOE_GROUPS if B % MOE_GROUPS == 0 else 1
        nb = B // n_groups
        T = nb * S
        wo_b = w_out[l].astype(BF16)
        x_in, x_out = x, None
        for grp in range(n_groups):
            b0 = grp * nb
            x1, hm2d, top_idx, gates, rank, counts = _outproj(
                att, rec, x_in, wo_b, g1, norm2_g[l].reshape(1, D), sc2, sh2,
                w_router[l].T, b_router[l].reshape(n_exp, 1), n_exp, b0, nb)
            block_e, nv, n_used, pos = _route(top_idx, rank, counts.reshape(n_exp))
            pos_chunks = pos.reshape(TOP_K, T // SC_CHUNK, SC_CHUNK).transpose(1, 0, 2)
            xs = _sc_scatter_rows(hm2d.reshape(T, YROWS, LANES), pos_chunks, block_e.shape[0] * EB)
            ys = _moe(block_e, nv, n_used, xs.reshape(-1, LANES),
                      w_moe1[l], b_moe1[l].reshape(n_exp, 1, -1), w_moe2[l], b_moe2[l].reshape(n_exp, 1, -1))
            y4 = _sc_gather_rows(ys.reshape(-1, YROWS, LANES), pos.reshape(TOP_K * T))
            x_out = _combine(y4, gates, x1, g2, b0, B, x_out)
        x = x_out
    return x
```

```python
import functools
import math

import jax
import jax.numpy as jnp
from jax import lax
from jax.experimental import pallas as pl
from jax.experimental.pallas import tpu as pltpu
from jax.experimental.pallas import tpu_sc as plsc

F32 = jnp.float32
BF16 = jnp.bfloat16

CHUNK = 64
HEAD_DIM = 64
ROPE_THETA = 10000.0
LRU_BLOCK = 64
CONV_W = 4
LRU_C = 8.0
TOP_K = 4
SWIGLU_ALPHA = 1.702
SWIGLU_LIMIT = 7.0
NORM_EPS = 1e-6
SUBLN_EPS = 1e-5
LOG2_E = math.log2(math.e)
SCORE_BOUND = 60.0
BF16_SLACK = 1.02

LANES = 128
SUBLANES = 8
YROWS = 4
VMEM_LIMIT = 56 * 1024 * 1024
SC_CORES = 2
SC_SUBCORES = 16
SC_CHUNK = 64

TM_PROJ = 512
TQ = 512
KV_WIDE = 4
TT = 256
EB = 512
TM_COMB = 512
MOE_GROUPS = 2


def _split_bf16(a):
    hi = a.astype(BF16)
    lo = (a - hi.astype(F32)).astype(BF16)
    return hi, lo


def _dot(a, b):
    return jnp.dot(a, b, preferred_element_type=F32)


def _store_packed_rows(ref, val, n):
    bits = lax.bitcast_convert_type(val.astype(BF16).astype(F32), jnp.uint32)
    for c in range(YROWS):
        even = bits[:, (2 * c) * LANES:(2 * c + 1) * LANES] >> 16
        odd = bits[:, (2 * c + 1) * LANES:(2 * c + 2) * LANES] & jnp.uint32(0xFFFF0000)
        ref[pl.ds(c, n, stride=YROWS), :] = lax.bitcast_convert_type(even | odd, jnp.int32)


def _load_packed_rows(ref, lead, n):
    chunks = []
    for c in range(YROWS):
        word = lax.bitcast_convert_type(ref[lead + (pl.ds(c, n, stride=YROWS), slice(None))], jnp.uint32)
        chunks.append(lax.bitcast_convert_type(word << 16, F32))
        chunks.append(lax.bitcast_convert_type(word & jnp.uint32(0xFFFF0000), F32))
    return jnp.concatenate(chunks, axis=1)


def _dot3(a, b):
    ah, al = _split_bf16(a)
    bh, bl = _split_bf16(b)
    return _dot(ah, bh) + (_dot(ah, bl) + _dot(al, bh))


def _ada_kernel(c_ref, w_ref, b_ref, o_ref):
    c = c_ref[...]
    s = c * jax.nn.sigmoid(c)
    o_ref[...] = _dot3(s, w_ref[...]) + b_ref[...]


def _ada(c, w_ada, b_ada):
    B, D = c.shape
    N = w_ada.shape[1]
    tn = D
    return pl.pallas_call(
        _ada_kernel,
        grid=(N // tn,),
        in_specs=[pl.BlockSpec((B, D), lambda j: (0, 0)),
                  pl.BlockSpec((D, tn), lambda j: (0, j)),
                  pl.BlockSpec((1, tn), lambda j: (0, j))],
        out_specs=pl.BlockSpec((B, tn), lambda j: (0, j)),
        out_shape=jax.ShapeDtypeStruct((B, N), F32),
        compiler_params=pltpu.CompilerParams(dimension_semantics=("arbitrary",),
                                             vmem_limit_bytes=VMEM_LIMIT),
        name="ada",
    )(c, w_ada, b_ada.reshape(1, N))


def _inproj_kernel(x_ref, g_ref, sc_ref, sh_ref, w_ref, gsum_ref, qg_ref, kg_ref, cos_ref, sin_ref,
                   qt_ref, k_ref, vt_ref, xr_ref, gr_ref, *, attn_w, lru_w):
    x = x_ref[0]
    ms = jnp.mean(x * x, axis=-1, keepdims=True)
    hn = x * lax.rsqrt(ms + NORM_EPS) * g_ref[...]
    hn = (hn * (1.0 + sc_ref[0]) + sh_ref[0]).astype(BF16)

    cos = cos_ref[...]
    sin = sin_ref[...]
    gsum = gsum_ref[...]
    lane = lax.broadcasted_iota(jnp.int32, cos.shape, 1)
    first_half = (lane % HEAD_DIM) < (HEAD_DIM // 2)

    def head_norm_rope(t, gain):
        ssum = _dot((t * t).astype(BF16), gsum)
        t = t * lax.rsqrt(ssum * (1.0 / HEAD_DIM) + NORM_EPS) * gain
        half = HEAD_DIM // 2
        partner = jnp.where(first_half, pltpu.roll(t, attn_w - half, 1), pltpu.roll(t, half, 1))
        return t * cos + partner * sin

    q = head_norm_rope(_dot(hn, w_ref[:, 0:attn_w]), qg_ref[...]) * (HEAD_DIM ** -0.5 * LOG2_E)
    for h in range(attn_w // LANES):
        qt_ref[0, h] = q[:, h * LANES:(h + 1) * LANES].T.astype(BF16)

    k = head_norm_rope(_dot(hn, w_ref[:, attn_w:2 * attn_w]), kg_ref[...])
    k_ref[0] = k.astype(BF16)

    v = _dot(hn, w_ref[:, 2 * attn_w:3 * attn_w])
    for h in range(attn_w // LANES):
        vt_ref[0, h] = v[:, h * LANES:(h + 1) * LANES].T.astype(BF16)
    xr_ref[0] = _dot(hn, w_ref[:, 3 * attn_w:3 * attn_w + lru_w])
    gr_ref[0] = _dot(hn, w_ref[:, 3 * attn_w + lru_w:3 * attn_w + 2 * lru_w])


def _inproj(x, g1n, sc1, sh1, w_in_b, gsum, qg, kg, cos_t, sin_t, attn_w, lru_w):
    B, S, D = x.shape
    tm = min(TM_PROJ, S)
    in_w = w_in_b.shape[1]
    nh = attn_w // LANES
    kern = functools.partial(_inproj_kernel, attn_w=attn_w, lru_w=lru_w)
    return pl.pallas_call(
        kern,
        grid=(S // tm, B),
        in_specs=[pl.BlockSpec((1, tm, D), lambda i, b: (b, i, 0)),
                  pl.BlockSpec((1, D), lambda i, b: (0, 0)),
                  pl.BlockSpec((1, 1, D), lambda i, b: (b, 0, 0)),
                  pl.BlockSpec((1, 1, D), lambda i, b: (b, 0, 0)),
                  pl.BlockSpec((D, in_w), lambda i, b: (0, 0)),
                  pl.BlockSpec((attn_w, attn_w), lambda i, b: (0, 0)),
                  pl.BlockSpec((1, attn_w), lambda i, b: (0, 0)),
                  pl.BlockSpec((1, attn_w), lambda i, b: (0, 0)),
                  pl.BlockSpec((tm, attn_w), lambda i, b: (i, 0)),
                  pl.BlockSpec((tm, attn_w), lambda i, b: (i, 0))],
        out_specs=[pl.BlockSpec((1, nh, LANES, tm), lambda i, b: (b, 0, 0, i)),
                   pl.BlockSpec((1, tm, attn_w), lambda i, b: (b, i, 0)),
                   pl.BlockSpec((1, nh, LANES, tm), lambda i, b: (b, 0, 0, i)),
                   pl.BlockSpec((1, tm, lru_w), lambda i, b: (b, i, 0)),
                   pl.BlockSpec((1, tm, lru_w), lambda i, b: (b, i, 0))],
        out_shape=[jax.ShapeDtypeStruct((B, nh, LANES, S), BF16),
                   jax.ShapeDtypeStruct((B, S, attn_w), BF16),
                   jax.ShapeDtypeStruct((B, nh, LANES, S), BF16),
                   jax.ShapeDtypeStruct((B, S, lru_w), F32),
                   jax.ShapeDtypeStruct((B, S, lru_w), F32)],
        compiler_params=pltpu.CompilerParams(dimension_semantics=("arbitrary", "arbitrary"),
                                             vmem_limit_bytes=VMEM_LIMIT),
        name="inproj",
    )(x, g1n, sc1, sh1, w_in_b, gsum, qg, kg, cos_t, sin_t)


def _attn_kernel(bounded_ref, qt_ref, k_ref, vt_ref, vis_ref, lq1_ref, lk1_ref, lq2_ref, lk2_ref, sg_ref, o_ref,
                 *, tq, wide, lam_init):
    qi = pl.program_id(2)
    qt = qt_ref[0, 0]
    row = lax.broadcasted_iota(jnp.int32, qt.shape, 0)
    zero = jnp.zeros_like(qt)
    qs = jnp.concatenate([jnp.where(row < HEAD_DIM, qt, zero),
                          jnp.where(row >= HEAD_DIM, qt, zero)], axis=1)

    def scores(start, size):
        kk = k_ref[0, pl.ds(pl.multiple_of(start, tq), size), :]
        return _dot(kk, qs)

    def values(start, size, p):
        vt = vt_ref[0, 0, :, pl.ds(pl.multiple_of(start, tq), size)]
        return _dot(vt, p)

    zeros = (jnp.zeros((1, 2 * tq), F32), jnp.zeros((LANES, 2 * tq), F32))

    def bounded_path():
        def tile(start, size, carry, vis=None):
            l, acc = carry
            p = jnp.exp2(scores(start, size))
            if vis is not None:
                p = p * vis
            return l + jnp.sum(p, axis=0, keepdims=True), acc + values(start, size, p.astype(BF16))

        carry = lax.fori_loop(0, qi // wide, lambda j, c: tile(j * (wide * tq), wide * tq, c), zeros)
        start = (qi // wide) * (wide * tq)
        piece = wide // 2
        while piece >= 1:
            take = ((qi % wide) & piece) != 0
            carry = lax.cond(take, lambda c, s=start, n=piece * tq: tile(s, n, c), lambda c: c, carry)
            start = start + jnp.where(take, piece * tq, 0)
            piece //= 2
        return tile(qi * tq, tq, carry, vis_ref[...])

    def online_path():
        def update(j, carry, masked):
            m, l, acc = carry
            s = scores(j * tq, tq)
            if masked:
                s = jnp.where(vis_ref[...] > 0.0, s, -jnp.inf)
            m_new = jnp.maximum(m, jnp.max(s, axis=0, keepdims=True))
            alpha = jnp.exp2(m - m_new)
            p = jnp.exp2(s - m_new)
            l = alpha * l + jnp.sum(p, axis=0, keepdims=True)
            return m_new, l, alpha * acc + values(j * tq, tq, p.astype(BF16))

        carry = (jnp.full((1, 2 * tq), -jnp.inf, F32),) + zeros
        carry = lax.fori_loop(0, qi, lambda j, c: update(j, c, False), carry)
        _, l, acc = update(qi, carry, True)
        return l, acc

    l, acc = lax.cond(bounded_ref[0] != 0, bounded_path, online_path)

    lam = (jnp.exp(jnp.sum(lq1_ref[...] * lk1_ref[...], axis=-1, keepdims=True))
           - jnp.exp(jnp.sum(lq2_ref[...] * lk2_ref[...], axis=-1, keepdims=True)) + lam_init)
    o = acc * (1.0 / l)
    o = o[:, :tq] - lam * o[:, tq:]
    ms = jnp.mean(o * o, axis=0, keepdims=True)
    o = o * lax.rsqrt(ms + SUBLN_EPS) * (sg_ref[...] * (1.0 - lam_init))
    o_ref[0] = o.T.astype(BF16)


def _attn(bounded, qt, k, vt, lq1, lk1, lq2, lk2, sg, lam_init):
    B, S, attn_w = k.shape
    nh = attn_w // LANES
    tq = min(TQ, S)
    wide = max(min(KV_WIDE, S // tq), 1)
    assert tq % CHUNK == 0 and wide & (wide - 1) == 0
    kern = functools.partial(_attn_kernel, tq=tq, wide=wide, lam_init=lam_init)
    vec = lambda n: pl.BlockSpec((1, n), lambda b, h, i: (0, 0))
    key_chunk = jnp.arange(tq, dtype=jnp.int32)[:, None] // CHUNK
    qry_chunk = (jnp.arange(2 * tq, dtype=jnp.int32)[None, :] % tq) // CHUNK
    vis = (key_chunk <= qry_chunk).astype(F32)
    return pl.pallas_call(
        kern,
        grid=(B, nh, S // tq),
        in_specs=[pl.BlockSpec(memory_space=pltpu.SMEM),
                  pl.BlockSpec((1, 1, LANES, tq), lambda b, h, i: (b, h, 0, i)),
                  pl.BlockSpec((1, S, LANES), lambda b, h, i: (b, 0, h)),
                  pl.BlockSpec((1, 1, LANES, S), lambda b, h, i: (b, h, 0, 0)),
                  pl.BlockSpec((tq, 2 * tq), lambda b, h, i: (0, 0)),
                  vec(HEAD_DIM), vec(HEAD_DIM), vec(HEAD_DIM), vec(HEAD_DIM),
                  pl.BlockSpec((LANES, 1), lambda b, h, i: (0, 0))],
        out_specs=pl.BlockSpec((1, tq, LANES), lambda b, h, i: (b, i, h)),
        out_shape=jax.ShapeDtypeStruct((B, S, attn_w), BF16),
        compiler_params=pltpu.CompilerParams(
            dimension_semantics=("arbitrary", "arbitrary", "arbitrary"), vmem_limit_bytes=VMEM_LIMIT),
        name="attn",
    )(bounded, qt, k, vt, vis, lq1, lk1, lq2, lk2, sg)


def _lru_kernel(xr_ref, gr_ref, cw_ref, cb_ref, wa_ref, ba_ref, wx_ref, bx_ref, lam_ref, o_ref,
                tail_ref, h_ref, *, tt):
    @pl.when(pl.program_id(1) == 0)
    def _():
        tail_ref[...] = jnp.zeros_like(tail_ref)
        h_ref[...] = jnp.zeros_like(h_ref)

    x = xr_ref[0]
    ext = jnp.concatenate([tail_ref[...], x], axis=0)
    xc = cb_ref[...] + cw_ref[CONV_W - 1:CONV_W, :] * x
    for j in range(CONV_W - 1):
        back = CONV_W - 1 - j
        xc = xc + cw_ref[j:j + 1, :] * ext[SUBLANES - back:SUBLANES - back + tt]
    tail_ref[...] = x[tt - SUBLANES:]

    xcb = xc.astype(BF16)
    r = jax.nn.sigmoid(_dot(xcb, wa_ref[...]) + ba_ref[...])
    i = jax.nn.sigmoid(_dot(xcb, wx_ref[...]) + bx_ref[...])
    nl = -lam_ref[...]
    softplus = jnp.maximum(nl, 0.0) + jnp.log(1.0 + jnp.exp(-jnp.abs(nl)))
    log_a = (-LRU_C) * r * softplus
    a = jnp.exp(log_a)
    th = jnp.tanh(log_a)
    u = jnp.sqrt((-2.0) * th / (1.0 - th)) * (i * xc)

    n_groups = tt // SUBLANES
    a = a.reshape(n_groups, SUBLANES, a.shape[1])
    u = u.reshape(a.shape)
    in_group = lax.broadcasted_iota(jnp.int32, a.shape, 1)
    d = 1
    while d < SUBLANES:
        keep = in_group >= d
        u = u + a * jnp.where(keep, pltpu.roll(u, d, 1), 0.0)
        a = a * jnp.where(keep, pltpu.roll(a, d, 1), 1.0)
        d *= 2
    carry = h_ref[0:1, :]
    groups = []
    for g in range(n_groups):
        hg = a[g] * carry + u[g]
        groups.append(hg)
        carry = hg[SUBLANES - 1:SUBLANES, :]
    h = jnp.concatenate(groups, axis=0)
    h_ref[...] = jnp.broadcast_to(carry, h_ref.shape)

    g = gr_ref[0]
    gelu = 0.5 * g * (1.0 + jnp.tanh(math.sqrt(2.0 / math.pi) * (g + 0.044715 * (g * g * g))))
    o_ref[0] = (h * gelu).astype(BF16)


def _lru(xr, gr, conv_w, conv_b, wa_bd, b_a, wx_bd, b_x, lam):
    B, S, W = xr.shape
    tt = min(TT, S)
    kern = functools.partial(_lru_kernel, tt=tt)
    vec = pl.BlockSpec((1, W), lambda b, i: (0, 0))
    mat = pl.BlockSpec((W, W), lambda b, i: (0, 0))
    return pl.pallas_call(
        kern,
        grid=(B, S // tt),
        in_specs=[pl.BlockSpec((1, tt, W), lambda b, i: (b, i, 0)),
                  pl.BlockSpec((1, tt, W), lambda b, i: (b, i, 0)),
                  pl.BlockSpec((CONV_W, W), lambda b, i: (0, 0)),
                  vec, mat, vec, mat, vec, vec],
        out_specs=pl.BlockSpec((1, tt, W), lambda b, i: (b, i, 0)),
        out_shape=jax.ShapeDtypeStruct((B, S, W), BF16),
        scratch_shapes=[pltpu.VMEM((SUBLANES, W), F32), pltpu.VMEM((SUBLANES, W), F32)],
        compiler_params=pltpu.CompilerParams(dimension_semantics=("arbitrary", "arbitrary"),
                                             vmem_limit_bytes=VMEM_LIMIT),
        name="lru",
    )(xr, gr, conv_w, conv_b, wa_bd, b_a, wx_bd, b_x, lam)


def _outproj_kernel(att_ref, rec_ref, x_ref, wo_ref, g1_ref, n2_ref, sc_ref, sh_ref, wr_ref, br_ref,
                    x1_ref, hm_ref, idx_ref, gate_ref, rank_ref, count_ref, cnt_ref, *, attn_w, n_exp):
    tm = x_ref.shape[1]
    mix = _dot(att_ref[0], wo_ref[0:attn_w, :]) + _dot(rec_ref[0], wo_ref[attn_w:, :])
    x1 = x_ref[0] + g1_ref[0] * mix
    x1_ref[0] = x1
    ms = jnp.mean(x1 * x1, axis=-1, keepdims=True)
    hm = x1 * lax.rsqrt(ms + NORM_EPS) * n2_ref[...]
    hm = hm * (1.0 + sc_ref[0]) + sh_ref[0]
    _store_packed_rows(hm_ref, hm, tm)

    hh, hl = _split_bf16(hm)
    wh, wl = _split_bf16(wr_ref[...])
    nt = (((1,), (1,)), ((), ()))
    logits = (lax.dot_general(wh, hh, nt, preferred_element_type=F32)
              + (lax.dot_general(wh, hl, nt, preferred_element_type=F32)
                 + lax.dot_general(wl, hh, nt, preferred_element_type=F32))) + br_ref[...]
    eidx = lax.broadcasted_iota(jnp.int32, logits.shape, 0)
    vals, idxs = [], []
    for _ in range(TOP_K):
        m = jnp.max(logits, axis=0, keepdims=True)
        am = jnp.min(jnp.where(logits == m, eidx, n_exp), axis=0, keepdims=True)
        vals.append(m)
        idxs.append(am)
        logits = jnp.where(eidx == am, -jnp.inf, logits)
    ex = [jnp.exp(v - vals[0]) for v in vals]
    den = ex[0] + ex[1] + ex[2] + ex[3]
    idx_ref[0] = jnp.concatenate(idxs, axis=0)
    gate_ref[0] = jnp.concatenate([e / den for e in ex], axis=0)

    @pl.when((pl.program_id(0) == 0) & (pl.program_id(1) == 0))
    def _():
        cnt_ref[...] = jnp.zeros_like(cnt_ref)

    hits = [eidx == am for am in idxs]
    sel = (hits[0] | hits[1] | hits[2] | hits[3]).astype(F32)
    earlier = (lax.broadcasted_iota(jnp.int32, (tm, tm), 0)
               < lax.broadcasted_iota(jnp.int32, (tm, tm), 1)).astype(BF16)
    before = _dot(sel.astype(BF16), earlier) + cnt_ref[...]
    rank_ref[0] = jnp.concatenate(
        [jnp.sum(jnp.where(h, before, 0.0), axis=0, keepdims=True) for h in hits], axis=0).astype(jnp.int32)
    cnt_ref[...] += jnp.sum(sel, axis=1, keepdims=True)
    count_ref[...] = cnt_ref[...].astype(jnp.int32)


def _outproj(att, rec, x, wo_b, g1, n2g, sc2, sh2, wr_t, br, n_exp, b0, B):
    _, S, D = x.shape
    attn_w = att.shape[2]
    lru_w = rec.shape[2]
    tm = min(TM_PROJ, S)
    ns = S // tm
    rows = YROWS
    assert D == 2 * YROWS * LANES
    kern = functools.partial(_outproj_kernel, attn_w=attn_w, n_exp=n_exp)
    bvec = pl.BlockSpec((1, 1, D), lambda b, i: (b + b0, 0, 0))
    return pl.pallas_call(
        kern,
        grid=(B, ns),
        in_specs=[pl.BlockSpec((1, tm, attn_w), lambda b, i: (b + b0, i, 0)),
                  pl.BlockSpec((1, tm, lru_w), lambda b, i: (b + b0, i, 0)),
                  pl.BlockSpec((1, tm, D), lambda b, i: (b + b0, i, 0)),
                  pl.BlockSpec((attn_w + lru_w, D), lambda b, i: (0, 0)),
                  bvec,
                  pl.BlockSpec((1, D), lambda b, i: (0, 0)),
                  bvec, bvec,
                  pl.BlockSpec((n_exp, D), lambda b, i: (0, 0)),
                  pl.BlockSpec((n_exp, 1), lambda b, i: (0, 0))],
        out_specs=[pl.BlockSpec((1, tm, D), lambda b, i: (b, i, 0)),
                   pl.BlockSpec((tm * rows, LANES), lambda b, i: (b * ns + i, 0)),
                   pl.BlockSpec((1, TOP_K, tm), lambda b, i: (b, 0, i)),
                   pl.BlockSpec((1, TOP_K, tm), lambda b, i: (b, 0, i)),
                   pl.BlockSpec((1, TOP_K, tm), lambda b, i: (b, 0, i)),
                   pl.BlockSpec((n_exp, 1), lambda b, i: (0, 0))],
        out_shape=[jax.ShapeDtypeStruct((B, S, D), F32),
                   jax.ShapeDtypeStruct((B * S * rows, LANES), jnp.int32),
                   jax.ShapeDtypeStruct((B, TOP_K, S), jnp.int32),
                   jax.ShapeDtypeStruct((B, TOP_K, S), F32),
                   jax.ShapeDtypeStruct((B, TOP_K, S), jnp.int32),
                   jax.ShapeDtypeStruct((n_exp, 1), jnp.int32)],
        scratch_shapes=[pltpu.VMEM((n_exp, 1), F32)],
        compiler_params=pltpu.CompilerParams(dimension_semantics=("arbitrary", "arbitrary"),
                                             vmem_limit_bytes=VMEM_LIMIT),
        name="outproj",
    )(att, rec, x, wo_b, g1, n2g, sc2, sh2, wr_t, br)


def _sc_gather_rows(table, idx):
    n = idx.shape[0]
    workers = SC_CORES * SC_SUBCORES
    per_worker = n // workers
    assert n % (workers * SC_CHUNK) == 0, "index count must split evenly into per-subcore chunks"
    n_chunks = per_worker // SC_CHUNK
    mesh = plsc.VectorSubcoreMesh(core_axis_name="c", subcore_axis_name="s",
                                  num_cores=SC_CORES, num_subcores=SC_SUBCORES)

    @functools.partial(
        pl.kernel, mesh=mesh,
        out_type=jax.ShapeDtypeStruct((n,) + table.shape[1:], table.dtype),
        scratch_types=[pltpu.VMEM((SC_CHUNK,), jnp.int32),
                       pltpu.VMEM((SC_CHUNK,) + table.shape[1:], table.dtype),
                       pltpu.SemaphoreType.DMA],
        name="sc_gather_rows",
    )
    def gather(table_hbm, idx_hbm, out_hbm, idx_v, rows_v, sem):
        base = (lax.axis_index("s") * SC_CORES + lax.axis_index("c")) * per_worker

        @pl.loop(0, n_chunks)
        def _(ci):
            off = pl.multiple_of(base + ci * SC_CHUNK, SC_CHUNK)
            pltpu.sync_copy(idx_hbm.at[pl.ds(off, SC_CHUNK)], idx_v)
            pltpu.async_copy(table_hbm.at[idx_v], rows_v, sem).wait()
            pltpu.sync_copy(rows_v, out_hbm.at[pl.ds(off, SC_CHUNK)])

    return gather(table, idx)


def _sc_scatter_rows(src, pos, n_out):
    n_tok = src.shape[0]
    n_k = pos.shape[1]
    workers = SC_CORES * SC_SUBCORES
    per_worker = n_tok // workers
    assert n_tok % (workers * SC_CHUNK) == 0, "token count must split evenly into per-subcore chunks"
    n_chunks = per_worker // SC_CHUNK
    mesh = plsc.VectorSubcoreMesh(core_axis_name="c", subcore_axis_name="s",
                                  num_cores=SC_CORES, num_subcores=SC_SUBCORES)

    @functools.partial(
        pl.kernel, mesh=mesh,
        out_type=jax.ShapeDtypeStruct((n_out,) + src.shape[1:], src.dtype),
        scratch_types=[pltpu.VMEM((n_k, SC_CHUNK), jnp.int32),
                       pltpu.VMEM((SC_CHUNK,) + src.shape[1:], src.dtype),
                       pltpu.SemaphoreType.DMA],
        name="sc_scatter_rows",
    )
    def scatter(src_hbm, pos_hbm, out_hbm, idx_v, rows_v, sem):
        base = (lax.axis_index("s") * SC_CORES + lax.axis_index("c")) * n_chunks

        @pl.loop(0, n_chunks)
        def _(ci):
            chunk = base + ci
            pltpu.sync_copy(pos_hbm.at[chunk], idx_v)
            pltpu.sync_copy(src_hbm.at[pl.ds(pl.multiple_of(chunk * SC_CHUNK, SC_CHUNK), SC_CHUNK)], rows_v)
            copies = [pltpu.async_copy(rows_v, out_hbm.at[idx_v.at[k]], sem) for k in range(n_k)]
            for cp in copies:
                cp.wait()

    return scatter(src, pos)


def _moe_kernel(be_ref, nv_ref, nused_ref, x_ref, w1_ref, b1_ref, w2_ref, b2_ref, y_ref, w1b_ref, w2b_ref,
                *, d_ff):
    i = pl.program_id(0)

    @pl.when(i < nused_ref[0])
    def _():
        @pl.when((i == 0) | (be_ref[i] != be_ref[jnp.maximum(i - 1, 0)]))
        def _():
            w1b_ref[...] = w1_ref[0].astype(BF16)
            w2b_ref[...] = w2_ref[0].astype(BF16)

        valid = lax.broadcasted_iota(jnp.int32, (EB, 1), 0) < nv_ref[i]
        x = jnp.where(valid, _load_packed_rows(x_ref, (), EB), 0.0).astype(BF16)
        hu = _dot(x, w1b_ref[...]) + b1_ref[0]
        glu = jnp.minimum(hu[:, :d_ff], SWIGLU_LIMIT)
        lin = jnp.clip(hu[:, d_ff:], -SWIGLU_LIMIT, SWIGLU_LIMIT)
        act = glu * jax.nn.sigmoid(SWIGLU_ALPHA * glu) * (lin + 1.0)
        y = _dot(act.astype(BF16), w2b_ref[...]) + b2_ref[0]
        _store_packed_rows(y_ref, y, EB)

    @pl.when(i >= nused_ref[0])
    def _():
        y_ref[...] = jnp.zeros_like(y_ref)


def _moe(block_e, nv, n_used, xs2d, w1, b1, w2, b2):
    n_blocks = block_e.shape[0]
    n_exp, D, ff2 = w1.shape
    d_ff = ff2 // 2
    assert D == 2 * YROWS * LANES, "a token row must fill exactly one packed (YROWS,128) word tile"
    kern = functools.partial(_moe_kernel, d_ff=d_ff)
    grid_spec = pltpu.PrefetchScalarGridSpec(
        num_scalar_prefetch=3,
        grid=(n_blocks,),
        in_specs=[pl.BlockSpec((EB * YROWS, LANES), lambda i, be, nv, nu: (i, 0)),
                  pl.BlockSpec((1, D, ff2), lambda i, be, nv, nu: (be[i], 0, 0)),
                  pl.BlockSpec((1, 1, ff2), lambda i, be, nv, nu: (be[i], 0, 0)),
                  pl.BlockSpec((1, d_ff, D), lambda i, be, nv, nu: (be[i], 0, 0)),
                  pl.BlockSpec((1, 1, D), lambda i, be, nv, nu: (be[i], 0, 0))],
        out_specs=pl.BlockSpec((EB * YROWS, LANES), lambda i, be, nv, nu: (i, 0)),
        scratch_shapes=[pltpu.VMEM((D, ff2), BF16), pltpu.VMEM((d_ff, D), BF16)],
    )
    return pl.pallas_call(
        kern,
        grid_spec=grid_spec,
        out_shape=jax.ShapeDtypeStruct((n_blocks * EB * YROWS, LANES), jnp.int32),
        compiler_params=pltpu.CompilerParams(dimension_semantics=("arbitrary",),
                                             vmem_limit_bytes=VMEM_LIMIT),
        name="moe",
    )(block_e, nv, n_used, xs2d, w1, b1, w2, b2)


def _combine_kernel(y_ref, gate_ref, x1_ref, g2_ref, *rest):
    o_ref = rest[-1]
    tm = x1_ref.shape[1]
    diag = (lax.broadcasted_iota(jnp.int32, (tm, tm), 0) == lax.broadcasted_iota(jnp.int32, (tm, tm), 1))
    moe = jnp.zeros(x1_ref.shape[1:], F32)
    for k in range(TOP_K):
        yk = _load_packed_rows(y_ref, (k,), tm)
        gate_col = jnp.sum(jnp.where(diag, gate_ref[0, k:k + 1, :], 0.0), axis=1, keepdims=True)
        moe = moe + gate_col * yk
    o_ref[0] = x1_ref[0] + g2_ref[0] * moe


def _combine(y4, gates, x1, g2, b0, n_batch, out_prev):
    B, S, D = x1.shape
    rows = YROWS
    assert D == 2 * YROWS * LANES
    tm = min(TM_COMB, S)
    ns = S // tm
    in_specs = [pl.BlockSpec((TOP_K, tm * rows, LANES), lambda b, i: (0, b * ns + i, 0)),
                pl.BlockSpec((1, TOP_K, tm), lambda b, i: (b, 0, i)),
                pl.BlockSpec((1, tm, D), lambda b, i: (b, i, 0)),
                pl.BlockSpec((1, 1, D), lambda b, i: (b + b0, 0, 0))]
    args = [y4.reshape(TOP_K, B * S * rows, LANES), gates, x1, g2]
    aliases = {}
    if out_prev is not None:
        in_specs.append(pl.BlockSpec(memory_space=pl.ANY))
        args.append(out_prev)
        aliases = {len(args) - 1: 0}
    return pl.pallas_call(
        _combine_kernel,
        grid=(B, ns),
        in_specs=in_specs,
        out_specs=pl.BlockSpec((1, tm, D), lambda b, i: (b + b0, i, 0)),
        out_shape=jax.ShapeDtypeStruct((n_batch, S, D), F32),
        input_output_aliases=aliases,
        compiler_params=pltpu.CompilerParams(dimension_semantics=("arbitrary", "arbitrary"),
                                             vmem_limit_bytes=VMEM_LIMIT),
        name="combine",
    )(*args)


def _route(top_idx, rank, counts):
    B, K, S = top_idx.shape
    T = B * S
    n_exp = counts.shape[0]
    nblk = (counts + EB - 1) // EB
    bend = jnp.cumsum(nblk)
    bstart = bend - nblk
    n_blocks = -(-(T * K) // EB) + n_exp
    blk = jnp.arange(n_blocks, dtype=jnp.int32)
    block_e = jnp.minimum(jnp.sum(blk[:, None] >= bend[None, :], axis=1), n_exp - 1).astype(jnp.int32)
    nv = jnp.where(blk < bend[-1], jnp.clip(counts[block_e] - (blk - bstart[block_e]) * EB, 0, EB), 0)
    first_row = (bstart * EB).astype(jnp.int32)
    start_row = sum(jnp.where(top_idx == e, first_row[e], 0) for e in range(n_exp))
    pos = (start_row + rank).transpose(1, 0, 2).reshape(K, T).astype(jnp.int32)
    return block_e, nv.astype(jnp.int32), bend[-1:].astype(jnp.int32), pos


def _lambda_init(layer_idx):
    return 0.8 - 0.6 * math.exp(-0.3 * layer_idx)


def _block_diag(w):
    n, c, d = w.shape
    eye = jnp.eye(n, dtype=w.dtype)
    return (eye[:, None, :, None] * w[:, :, None, :]).reshape(n * c, n * d)


def kernel(x, c, w_ada, b_ada, norm1_g, w_in, q_norm_g, k_norm_g, lambda_q1, lambda_k1, lambda_q2, lambda_k2, subln_g, conv_w, conv_b, w_rg_a, b_rg_a, w_rg_x, b_rg_x, lru_lambda, w_out, norm2_g, w_router, b_router, w_moe1, b_moe1, w_moe2, b_moe2):
    B, S, D = x.shape
    depth = w_ada.shape[0]
    lru_w = conv_w.shape[2]
    attn_w = (w_in.shape[2] - 2 * lru_w) // 3
    n_exp = w_router.shape[2]
    n_heads = attn_w // HEAD_DIM
    assert attn_w % LANES == 0 and S % CHUNK == 0

    half = HEAD_DIM // 2
    inv = ROPE_THETA ** (-jnp.arange(half, dtype=F32) / half)
    ang = jnp.arange(S, dtype=jnp.int32).astype(F32)[:, None] * inv[None, :]
    cos_t = jnp.tile(jnp.concatenate([jnp.cos(ang), jnp.cos(ang)], axis=1), (1, n_heads))
    sin_t = jnp.tile(jnp.concatenate([-jnp.sin(ang), jnp.sin(ang)], axis=1), (1, n_heads))
    group = jnp.arange(attn_w, dtype=jnp.int32) // HEAD_DIM
    gsum = (group[:, None] == group[None, :]).astype(BF16)

    for l in range(depth):
        mod = _ada(c, w_ada[l], b_ada[l])
        sh1, sc1, g1, sh2, sc2, g2 = [m.reshape(B, 1, D) for m in jnp.split(mod, 6, axis=-1)]

        qt, k, vt, xr, gr = _inproj(
            x, norm1_g[l].reshape(1, D), sc1, sh1, w_in[l].astype(BF16), gsum,
            jnp.tile(q_norm_g[l], n_heads).reshape(1, attn_w), jnp.tile(k_norm_g[l], n_heads).reshape(1, attn_w),
            cos_t, sin_t, attn_w, lru_w)

        score_bound = (HEAD_DIM ** 0.5 * LOG2_E * BF16_SLACK) * jnp.max(jnp.abs(q_norm_g[l])) * jnp.max(jnp.abs(k_norm_g[l]))
        bounded = (score_bound <= SCORE_BOUND).astype(jnp.int32).reshape(1)
        att = _attn(bounded, qt, k, vt, lambda_q1[l].reshape(1, -1), lambda_k1[l].reshape(1, -1),
                    lambda_q2[l].reshape(1, -1), lambda_k2[l].reshape(1, -1),
                    subln_g[l].reshape(-1, 1), _lambda_init(l))

        rec = _lru(xr, gr, conv_w[l], conv_b[l].reshape(1, lru_w),
                   _block_diag(w_rg_a[l]).astype(BF16), b_rg_a[l].reshape(1, lru_w),
                   _block_diag(w_rg_x[l]).astype(BF16), b_rg_x[l].reshape(1, lru_w),
                   lru_lambda[l].reshape(1, lru_w))

        n_groups = MOE_GROUPS if B % MOE_GROUPS == 0 else 1
        nb = B // n_groups
        T = nb * S
        wo_b = w_out[l].astype(BF16)
        x_in, x_out = x, None
        for grp in range(n_groups):
            b0 = grp * nb
            x1, hm2d, top_idx, gates, rank, counts = _outproj(
                att, rec, x_in, wo_b, g1, norm2_g[l].reshape(1, D), sc2, sh2,
                w_router[l].T, b_router[l].reshape(n_exp, 1), n_exp, b0, nb)
            block_e, nv, n_used, pos = _route(top_idx, rank, counts.reshape(n_exp))
            pos_chunks = pos.reshape(TOP_K, T // SC_CHUNK, SC_CHUNK).transpose(1, 0, 2)
            xs = _sc_scatter_rows(hm2d.reshape(T, YROWS, LANES), pos_chunks, block_e.shape[0] * EB)
            ys = _moe(block_e, nv, n_used, xs.reshape(-1, LANES),
                      w_moe1[l], b_moe1[l].reshape(n_exp, 1, -1), w_moe2[l], b_moe2[l].reshape(n_exp, 1, -1))
            y4 = _sc_gather_rows(ys.reshape(-1, YROWS, LANES), pos.reshape(TOP_K * T))
            x_out = _combine(y4, gates, x1, g2, b0, B, x_out)
        x = x_out
    return x
```
